```python
import jax
import jax.numpy as jnp
from jax import lax
import numpy as np

D_MODEL = 1024
BATCH = 16
SEQ = 4096
DEPTH = 2
DEC_BATCH = 4
DEC_SEQ = 4096
PAST_LEN = 128

GRID_W = 64
HEAD_DIM = 64
NA_HEADS = 8
NA_WIN_R = 8
NA_WIN_C = 16
DIL_HEADS = 8
DIL_BRANCHES = ((128, 1), (512, 4), (2048, 16))
DIL_QBLOCK = 128
RET_HEADS = 4
RET_QK_DIM = 256
RET_V_DIM = 512
RET_CHUNK = 128
FFN_DIM = 2816
CONV_WIDTH = 3
ROPE_THETA = 10000.0
NORM_EPS = 1e-6
NEG_INF = -1e30
EVEN_MIX = (NA_HEADS + DIL_HEADS) * HEAD_DIM
EVEN_IN = 3 * EVEN_MIX
RET_IN = 2 * RET_HEADS * RET_QK_DIM + 2 * RET_HEADS * RET_V_DIM
RET_MIX = RET_HEADS * RET_V_DIM

kernel_name = 'hybrid_na_dilated_retention_encoder'


def rmsnorm(x, g):
    xf = x.astype(jnp.float32)
    y = xf * lax.rsqrt(jnp.mean(jnp.square(xf), axis=-1, keepdims=True) + NORM_EPS)
    return (y * g.astype(jnp.float32)).astype(x.dtype)


def rotary(x):
    t, dh = x.shape[1], x.shape[-1]
    inv = 1.0 / (ROPE_THETA ** (jnp.arange(0, dh, 2, dtype=jnp.float32) / dh))
    ang = jnp.arange(t, dtype=jnp.float32)[:, None] * inv[None, :]
    cos = jnp.cos(ang)[None, :, None, :]
    sin = jnp.sin(ang)[None, :, None, :]
    xf = x.astype(jnp.float32)
    x1, x2 = xf[..., :dh // 2], xf[..., dh // 2:]
    return jnp.concatenate([x1 * cos - x2 * sin, x1 * sin + x2 * cos], axis=-1).astype(x.dtype)


def neighbourhood_attention(q, k, v, rpb):
    b, t, h, dh = q.shape
    rows = t // GRID_W
    wr = min(NA_WIN_R, rows)
    wc = NA_WIN_C
    qg = q.reshape(b, rows, GRID_W, h, dh)
    kg = k.reshape(b, rows, GRID_W, h, dh)
    vg = v.reshape(b, rows, GRID_W, h, dh)
    col = jnp.arange(GRID_W)
    col_start = jnp.clip(col - wc // 2, 0, GRID_W - wc)
    col_idx = col_start[:, None] + jnp.arange(wc)[None, :]
    col_off = col_idx - col[:, None] + (NA_WIN_C - 1)
    scale = dh ** -0.5
    rpb32 = rpb.astype(jnp.float32)

    def one_row(r):
        row_start = jnp.clip(r - wr // 2, 0, rows - wr)
        qr = lax.dynamic_index_in_dim(qg, r, axis=1, keepdims=False)
        kr = lax.dynamic_slice_in_dim(kg, row_start, wr, axis=1)
        vr = lax.dynamic_slice_in_dim(vg, row_start, wr, axis=1)
        kn = jnp.take(kr, col_idx, axis=2)
        vn = jnp.take(vr, col_idx, axis=2)
        s = jnp.einsum('bqhd,biqjhd->bhqij', qr, kn).astype(jnp.float32) * scale
        row_off = row_start + jnp.arange(wr) - r + (NA_WIN_R - 1)
        bias = rpb32[:, row_off][:, :, col_off].transpose(0, 2, 1, 3)
        s = s + bias[None]
        p = jax.nn.softmax(s.reshape(b, h, GRID_W, wr * wc), axis=-1).reshape(b, h, GRID_W, wr, wc)
        return jnp.einsum('bhqij,biqjhd->bqhd', p.astype(v.dtype), vn)

    out = lax.map(one_row, jnp.arange(rows))
    return out.transpose(1, 0, 2, 3, 4).reshape(b, t, h, dh)


def dilated_branch(q, k, v, dilation, half):
    b, t, h, dh = q.shape
    L = t // dilation
    nblk = -(-L // DIL_QBLOCK)
    Lp = nblk * DIL_QBLOCK
    kw = DIL_QBLOCK + 2 * half

    def to_res(a):
        return a.reshape(b, L, dilation, h, dh).transpose(0, 2, 1, 3, 4).reshape(b * dilation, L, h, dh)

    qr = jnp.pad(to_res(q), ((0, 0), (0, Lp - L), (0, 0), (0, 0)))
    kr = jnp.pad(to_res(k), ((0, 0), (half, Lp - L + half), (0, 0), (0, 0)))
    vr = jnp.pad(to_res(v), ((0, 0), (half, Lp - L + half), (0, 0), (0, 0)))
    qi = jnp.arange(DIL_QBLOCK)
    kj = jnp.arange(kw)
    rel = kj[None, :] - half - qi[:, None]
    scale = dh ** -0.5

    def one_block(blk):
        s0 = blk * DIL_QBLOCK
        qb = lax.dynamic_slice_in_dim(qr, s0, DIL_QBLOCK, axis=1)
        kb = lax.dynamic_slice_in_dim(kr, s0, kw, axis=1)
        vb = lax.dynamic_slice_in_dim(vr, s0, kw, axis=1)
        kpos = s0 - half + kj
        valid = (jnp.abs(rel) <= half) & ((kpos >= 0) & (kpos < L))[None, :]
        s = jnp.einsum('nqhd,nkhd->nhqk', qb, kb).astype(jnp.float32) * scale
        s = jnp.where(valid[None, None], s, NEG_INF)
        m = jnp.max(s, axis=-1, keepdims=True)
        e = jnp.exp(s - m)
        z = jnp.sum(e, axis=-1)
        o = jnp.einsum('nhqk,nkhd->nqhd', (e / z[..., None]).astype(v.dtype), vb)
        lse = (m[..., 0] + jnp.log(z)).transpose(0, 2, 1)
        return o, lse

    o, lse = lax.map(one_block, jnp.arange(nblk))
    o = o.transpose(1, 0, 2, 3, 4).reshape(b * dilation, Lp, h, dh)[:, :L]
    lse = lse.transpose(1, 0, 2, 3).reshape(b * dilation, Lp, h)[:, :L]
    o = o.reshape(b, dilation, L, h, dh).transpose(0, 2, 1, 3, 4).reshape(b, t, h, dh)
    lse = lse.reshape(b, dilation, L, h).transpose(0, 2, 1, 3).reshape(b, t, h)
    return o, lse


def dilated_attention(q, k, v):
    outs, lses = [], []
    for window, dilation in DIL_BRANCHES:
        o, lse = dilated_branch(q, k, v, dilation, (window // 2) // dilation)
        outs.append(o)
        lses.append(lse)
    wts = jax.nn.softmax(jnp.stack(lses, axis=0), axis=0)
    return jnp.einsum('nbth,nbthd->bthd', wts.astype(q.dtype), jnp.stack(outs, axis=0))


def even_mixer(h, w_in, rpb, w_out):
    b, t, _ = h.shape
    proj = h @ w_in
    n_na = 3 * NA_HEADS * HEAD_DIM
    na = proj[..., :n_na].reshape(b, t, 3, NA_HEADS, HEAD_DIM)
    dl = proj[..., n_na:].reshape(b, t, 3, DIL_HEADS, HEAD_DIM)
    oa = neighbourhood_attention(na[:, :, 0], na[:, :, 1], na[:, :, 2], rpb)
    ob = dilated_attention(rotary(dl[:, :, 0]), rotary(dl[:, :, 1]), dl[:, :, 2])
    o = jnp.concatenate([oa.reshape(b, t, NA_HEADS * HEAD_DIM), ob.reshape(b, t, DIL_HEADS * HEAD_DIM)], axis=-1)
    return o @ w_out


def chunk_retention(q, k, v, log_gamma, include_diag):
    b, t, h, dk = q.shape
    dv = v.shape[-1]
    c = RET_CHUNK
    n = t // c

    def chunks(a):
        return a.reshape(b, n, c, h, a.shape[-1]).transpose(1, 0, 3, 2, 4)

    pos = jnp.arange(c, dtype=jnp.float32)
    diff = pos[:, None] - pos[None, :]
    mask = (diff >= 0) if include_diag else (diff > 0)
    lg = log_gamma.astype(jnp.float32)
    decay_in = jnp.where(mask, jnp.exp(lg[:, None, None] * jnp.where(mask, diff, 0.0)), 0.0)[None].astype(q.dtype)
    q_decay = jnp.exp(lg[:, None] * (pos + 1.0))[None, :, :, None].astype(q.dtype)
    k_decay = jnp.exp(lg[:, None] * (c - 1.0 - pos))[None, :, :, None].astype(q.dtype)
    s_decay = jnp.exp(lg * c)[None, :, None, None].astype(q.dtype)

    def step(state, inp):
        qc, kc, vc = inp
        inner = jnp.einsum('bhqd,bhkd->bhqk', qc, kc) * decay_in
        out = jnp.einsum('bhqk,bhkv->bhqv', inner, vc) + jnp.einsum('bhqd,bhdv->bhqv', qc * q_decay, state)
        state = state * s_decay + jnp.einsum('bhkd,bhkv->bhdv', kc * k_decay, vc)
        return state, out

    state0 = jnp.zeros((b, h, dk, dv), q.dtype)
    _, out = lax.scan(step, state0, (chunks(q), chunks(k), chunks(v)))
    return out.transpose(1, 0, 3, 2, 4).reshape(b, t, h, dv)


def retention_mixer(h, w_in, decay_fwd_raw, decay_bwd_raw, w_out):
    b, t, _ = h.shape
    nq = RET_HEADS * RET_QK_DIM
    nv = RET_HEADS * RET_V_DIM
    proj = h @ w_in
    q = proj[..., :nq].reshape(b, t, RET_HEADS, RET_QK_DIM)
    k = proj[..., nq:2 * nq].reshape(b, t, RET_HEADS, RET_QK_DIM)
    v = proj[..., 2 * nq:2 * nq + nv].reshape(b, t, RET_HEADS, RET_V_DIM)
    g = proj[..., 2 * nq + nv:]
    q = rotary(q) * (RET_QK_DIM ** -0.5)
    k = rotary(k)
    log_g_f = -jax.nn.softplus(decay_fwd_raw.astype(jnp.float32))
    log_g_b = -jax.nn.softplus(decay_bwd_raw.astype(jnp.float32))
    fwd = chunk_retention(q, k, v, log_g_f, True)
    bwd = jnp.flip(chunk_retention(jnp.flip(q, 1), jnp.flip(k, 1), jnp.flip(v, 1), log_g_b, False), 1)
    r = (fwd + bwd).astype(jnp.float32)
    mu = jnp.mean(r, axis=-1, keepdims=True)
    var = jnp.mean(jnp.square(r - mu), axis=-1, keepdims=True)
    r = ((r - mu) * lax.rsqrt(var + NORM_EPS)).astype(h.dtype).reshape(b, t, nv)
    return (jax.nn.silu(g) * r) @ w_out


def conv_ffn(h, w_up, conv_w, conv_b, w_down):
    t = h.shape[1]
    a = h @ w_up
    pad = CONV_WIDTH // 2
    ap = jnp.pad(a, ((0, 0), (pad, pad), (0, 0)))
    y = conv_b
    for j in range(CONV_WIDTH):
        y = y + ap[:, j:j + t] * conv_w[j]
    u, gate = y[..., :FFN_DIM], y[..., FFN_DIM:]
    return (u * jax.nn.gelu(gate)) @ w_down


def encoder_trunk(x, attn_norm, even_w_in, na_rpb, even_w_out, ret_w_in, ret_decay_fwd, ret_decay_bwd,
                  ret_w_out, ffn_norm, ffn_w_up, ffn_conv_w, ffn_conv_b, ffn_w_down, final_norm):
    for layer in range(DEPTH):
        h = rmsnorm(x, attn_norm[layer])
        if layer % 2 == 0:
            e = layer // 2
            x = x + even_mixer(h, even_w_in[e], na_rpb[e], even_w_out[e])
        else:
            o = layer // 2
            x = x + retention_mixer(h, ret_w_in[o], ret_decay_fwd[o], ret_decay_bwd[o], ret_w_out[o])
        h = rmsnorm(x, ffn_norm[layer])
        x = x + conv_ffn(h, ffn_w_up[layer], ffn_conv_w[layer], ffn_conv_b[layer], ffn_w_down[layer])
    return rmsnorm(x, final_norm)


def setup_inputs(seed: int = 0) -> dict:
    key = jax.random.key(seed)
    ks = jax.random.split(key, 16)
    f32 = jnp.float32
    n_even = (DEPTH + 1) // 2
    n_odd = DEPTH // 2

    def normal(k, shape, scale):
        return jax.random.normal(k, shape, f32) * scale

    neg_log_gamma = -jnp.log1p(-(2.0 ** (-5.0 - jnp.arange(RET_HEADS, dtype=f32))))
    decay_base = jnp.log(jnp.expm1(neg_log_gamma))
    return {
        'x_prompt': normal(ks[0], (BATCH, SEQ, D_MODEL), 1.0),
        'x_sample': normal(ks[1], (DEC_BATCH, DEC_SEQ, D_MODEL), 1.0),
        'attn_norm': 1.0 + normal(ks[2], (DEPTH, D_MODEL), 0.01),
        'even_w_in': normal(ks[3], (n_even, D_MODEL, EVEN_IN), D_MODEL ** -0.5),
        'na_rpb': normal(ks[4], (n_even, NA_HEADS, 2 * NA_WIN_R - 1, 2 * NA_WIN_C - 1), 0.1),
        'even_w_out': normal(ks[5], (n_even, EVEN_MIX, D_MODEL), EVEN_MIX ** -0.5),
        'ret_w_in': normal(ks[6], (n_odd, D_MODEL, RET_IN), D_MODEL ** -0.5),
        'ret_decay_fwd': decay_base[None, :] + normal(ks[7], (n_odd, RET_HEADS), 0.1),
        'ret_decay_bwd': decay_base[None, :] + normal(ks[8], (n_odd, RET_HEADS), 0.1),
        'ret_w_out': normal(ks[9], (n_odd, RET_MIX, D_MODEL), RET_MIX ** -0.5),
        'ffn_norm': 1.0 + normal(ks[10], (DEPTH, D_MODEL), 0.01),
        'ffn_w_up': normal(ks[11], (DEPTH, D_MODEL, 2 * FFN_DIM), D_MODEL ** -0.5),
        'ffn_conv_w': normal(ks[12], (DEPTH, CONV_WIDTH, 2 * FFN_DIM), CONV_WIDTH ** -0.5),
        'ffn_conv_b': normal(ks[13], (DEPTH, 2 * FFN_DIM), 0.01),
        'ffn_w_down': normal(ks[14], (DEPTH, FFN_DIM, D_MODEL), FFN_DIM ** -0.5),
        'final_norm': 1.0 + normal(ks[15], (D_MODEL,), 0.01),
    }


def reference(x_prompt, x_sample, attn_norm, even_w_in, na_rpb, even_w_out, ret_w_in, ret_decay_fwd,
              ret_decay_bwd, ret_w_out, ffn_norm, ffn_w_up, ffn_conv_w, ffn_conv_b, ffn_w_down, final_norm):
    y_prompt = encoder_trunk(x_prompt, attn_norm, even_w_in, na_rpb, even_w_out, ret_w_in, ret_decay_fwd,
                             ret_decay_bwd, ret_w_out, ffn_norm, ffn_w_up, ffn_conv_w, ffn_conv_b, ffn_w_down,
                             final_norm)
    y_sample = encoder_trunk(x_sample, attn_norm, even_w_in, na_rpb, even_w_out, ret_w_in, ret_decay_fwd,
                             ret_decay_bwd, ret_w_out, ffn_norm, ffn_w_up, ffn_conv_w, ffn_conv_b, ffn_w_down,
                             final_norm)
    return (y_prompt, y_sample)
```

```python
import functools

import jax
import jax.numpy as jnp
from jax import lax
from jax.experimental import pallas as pl
from jax.experimental.pallas import tpu as pltpu

D_MODEL = 1024
GRID_W = 64
HEAD_DIM = 64
NA_HEADS = 8
NA_WIN_R = 8
NA_WIN_C = 16
DIL_HEADS = 8
DIL_BRANCHES = ((128, 1), (512, 4), (2048, 16))
DIL_QBLOCK = 128
RET_HEADS = 4
RET_QK_DIM = 256
RET_V_DIM = 512
RET_CHUNK = 128
FFN_DIM = 2816
CONV_WIDTH = 3
ROPE_THETA = 10000.0
NORM_EPS = 1e-6
NEG_INF = -1e30

LANES = 128
MIX = NA_HEADS * HEAD_DIM
EVEN_IN = 6 * MIX
RET_Q = RET_HEADS * RET_QK_DIM
RET_V = RET_HEADS * RET_V_DIM
FFN_CHUNK = 256
N_FFN_CHUNKS = FFN_DIM // FFN_CHUNK
TOKEN_TILE = 512
HALO = 8
VMEM_LIMIT = 56 * 1024 * 1024

F32 = jnp.float32
BF16 = jnp.bfloat16


def _params(sem, vmem=VMEM_LIMIT):
    return pltpu.CompilerParams(dimension_semantics=sem, vmem_limit_bytes=vmem)


def _resident(shape):
    nd = len(shape)
    return pl.BlockSpec(shape, lambda *_: (0,) * nd, pipeline_mode=pl.Buffered(1))


def _rms(x, g):
    ms = jnp.mean(x * x, axis=-1, keepdims=True)
    return x * lax.rsqrt(ms + NORM_EPS) * g


def _even_proj_kernel(x_ref, g_ref, w_ref, cos_ref, sin_ref, o_ref):
    h = _rms(x_ref[...], g_ref[...]).astype(BF16)
    for c in range(EVEN_IN // MIX):
        a = jnp.dot(h, w_ref[:, c * MIX:(c + 1) * MIX], preferred_element_type=F32)
        if c in (3, 4):
            cos, sin = cos_ref[...], sin_ref[...]
            parts = []
            for j in range(MIX // LANES):
                aj = a[:, j * LANES:(j + 1) * LANES]
                parts.append(aj * cos + pltpu.roll(aj, LANES // 2, axis=1) * sin)
            a = jnp.concatenate(parts, axis=1)
        o_ref[:, c * MIX:(c + 1) * MIX] = a.astype(BF16)


def _even_proj(x2d, g, w, cos, sin, seq):
    n = x2d.shape[0]
    tm = TOKEN_TILE
    tps = seq // tm
    return pl.pallas_call(
        _even_proj_kernel,
        grid=(n // tm,),
        in_specs=[
            pl.BlockSpec((tm, D_MODEL), lambda i: (i, 0)),
            _resident((1, D_MODEL)),
            _resident((D_MODEL, EVEN_IN)),
            pl.BlockSpec((tm, LANES), lambda i: (i % tps, 0)),
            pl.BlockSpec((tm, LANES), lambda i: (i % tps, 0)),
        ],
        out_specs=pl.BlockSpec((tm, EVEN_IN), lambda i: (i, 0)),
        out_shape=jax.ShapeDtypeStruct((n, EVEN_IN), BF16),
        compiler_params=_params(("parallel",)),
        name="even_proj",
    )(x2d, g, w, cos, sin)


def _na_kernel(q_ref, k_ref, v_ref, b_ref, o_ref, *, rows):
    lane = lax.broadcasted_iota(jnp.int32, (GRID_W, LANES), 1)
    is_a = lane < HEAD_DIM
    win = NA_WIN_R * GRID_W

    def body(r, carry):
        rs = jnp.clip(r - NA_WIN_R // 2, 0, rows - NA_WIN_R)
        pat = jnp.where(r < NA_WIN_R // 2, r, jnp.where(r > rows - NA_WIN_R // 2, r - (rows - NA_WIN_R), NA_WIN_R // 2))
        q0 = pl.multiple_of(r * GRID_W, GRID_W)
        k0 = pl.multiple_of(rs * GRID_W, GRID_W)
        q = q_ref[pl.ds(q0, GRID_W), :]
        zero = jnp.zeros_like(q)
        q2 = jnp.concatenate([jnp.where(is_a, q, zero), jnp.where(is_a, zero, q)], axis=0)
        kw = k_ref[pl.ds(k0, win), :]
        vw = v_ref[pl.ds(k0, win), :]
        s = lax.dot_general(q2, kw, (((1,), (1,)), ((), ())), preferred_element_type=F32)
        s = s + b_ref[pat]
        m = jnp.max(s, axis=1, keepdims=True)
        e = jnp.exp(s - m)
        l = jnp.sum(e, axis=1, keepdims=True)
        pv = jnp.dot(e.astype(BF16), vw, preferred_element_type=F32) / l
        o_ref[pl.ds(q0, GRID_W), :] = jnp.where(is_a, pv[:GRID_W], pv[GRID_W:]).astype(BF16)
        return carry

    lax.fori_loop(0, rows, body, 0)


def _na_attention(qkv, bias, seq):
    b = qkv.shape[0]
    rows = seq // GRID_W
    npair = NA_HEADS // 2
    blk = lambda off: pl.BlockSpec((None, seq, LANES), lambda p, i: (i, 0, off + p))
    return pl.pallas_call(
        functools.partial(_na_kernel, rows=rows),
        grid=(npair, b),
        in_specs=[blk(0), blk(npair), blk(2 * npair),
                  pl.BlockSpec((NA_WIN_R, None, 2 * GRID_W, NA_WIN_R * GRID_W), lambda p, i: (0, p, 0, 0))],
        out_specs=pl.BlockSpec((None, seq, LANES), lambda p, i: (i, 0, p)),
        out_shape=jax.ShapeDtypeStruct((b, seq, MIX), BF16),
        compiler_params=_params(("parallel", "parallel")),
        name="na_attention",
    )(qkv, qkv, qkv, bias)


def _na_bias_table(rpb, rows):
    c = jnp.arange(GRID_W)
    cs = jnp.clip(c - NA_WIN_C // 2, 0, GRID_W - NA_WIN_C)
    j = jnp.arange(GRID_W)
    valid = (j[None, :] >= cs[:, None]) & (j[None, :] < cs[:, None] + NA_WIN_C)
    coff = jnp.clip(j[None, :] - c[:, None] + NA_WIN_C - 1, 0, 2 * NA_WIN_C - 2)
    rpb32 = rpb.astype(F32)
    half = NA_WIN_R // 2
    reps = list(range(half)) + [half] + list(range(rows - half + 1, rows))
    tabs = []
    for r in reps:
        rs = min(max(r - half, 0), rows - NA_WIN_R)
        roff = rs + jnp.arange(NA_WIN_R) - r + NA_WIN_R - 1
        t = rpb32[:, roff][:, :, coff]
        t = jnp.where(valid[None, None], t, NEG_INF).transpose(0, 2, 1, 3)
        tabs.append(t.reshape(NA_HEADS // 2, 2 * GRID_W, NA_WIN_R * GRID_W))
    return jnp.stack(tabs, axis=0)


def _dil_kernel(q_ref, k_ref, v_ref, o_ref, lse_ref, *, length, pairs, half, grid_pairs):
    qb_rows = DIL_QBLOCK
    kw_rows = DIL_QBLOCK + 2 * half
    row = lax.broadcasted_iota(jnp.int32, (2 * qb_rows, kw_rows), 0)
    col = lax.broadcasted_iota(jnp.int32, (2 * qb_rows, kw_rows), 1)
    base = col - (row & (qb_rows - 1))
    lane = lax.broadcasted_iota(jnp.int32, (qb_rows, LANES), 1)
    is_a_out = lane < HEAD_DIM
    is_a_rot = (lane & (HEAD_DIM - 1)) < HEAD_DIM // 2
    pair0 = pl.program_id(2) if grid_pairs else 0

    if grid_pairs:
        @pl.when(pair0 == 0)
        def _():
            lse_ref[...] = jnp.zeros_like(lse_ref)

    def body(i, carry):
        q0 = pl.multiple_of(i * qb_rows, qb_rows)
        ws = pl.multiple_of(jnp.clip(i * qb_rows - half, 0, length - kw_rows), half)
        valid = jnp.abs(base + (ws - i * qb_rows)) <= half
        lse_tile = lse_ref[pl.ds(q0, qb_rows), :] if grid_pairs else jnp.zeros((qb_rows, LANES), F32)
        for j in range(pairs):
            cols = slice(j * LANES, (j + 1) * LANES)
            q = q_ref[pl.ds(q0, qb_rows), cols]
            zero = jnp.zeros_like(q)
            q2 = jnp.concatenate([jnp.where(is_a_rot, q, zero), jnp.where(is_a_rot, zero, q)], axis=0)
            kw = k_ref[pl.ds(ws, kw_rows), cols]
            vw = v_ref[pl.ds(ws, kw_rows), cols]
            s = lax.dot_general(q2, kw, (((1,), (1,)), ((), ())), preferred_element_type=F32)
            s = jnp.where(valid, s, NEG_INF)
            m = jnp.max(s, axis=1, keepdims=True)
            e = jnp.exp(s - m)
            l = jnp.sum(e, axis=1, keepdims=True)
            pv = jnp.dot(e.astype(BF16), vw, preferred_element_type=F32) / l
            o_ref[pl.ds(q0, qb_rows), cols] = jnp.where(is_a_out, pv[:qb_rows], pv[qb_rows:])
            lse = m + jnp.log(l)
            head = 2 * (pair0 + j)
            lse_tile = jnp.where(lane == head, lse[:qb_rows],
                                 jnp.where(lane == head + 1, lse[qb_rows:], lse_tile))
        lse_ref[pl.ds(q0, qb_rows), :] = lse_tile
        return carry

    lax.fori_loop(0, length // qb_rows, body, 0)


def _dil_branch(qkv, seq, dilation, half):
    b = qkv.shape[0]
    length = seq // dilation
    npair = DIL_HEADS // 2
    grid_pairs = length * MIX * 4 > 4 * 1024 * 1024
    pairs = 1 if grid_pairs else npair
    width = pairs * LANES
    per_pos = EVEN_IN // width
    qoff = (3 * MIX) // width
    step = MIX // width
    view = qkv.reshape(b, length, dilation * EVEN_IN)
    blk = lambda off: pl.BlockSpec((None, length, width), lambda i, r, p: (i, 0, r * per_pos + off + p))
    grid = (b, dilation, npair if grid_pairs else 1)
    o, lse = pl.pallas_call(
        functools.partial(_dil_kernel, length=length, pairs=pairs, half=half, grid_pairs=grid_pairs),
        grid=grid,
        in_specs=[blk(qoff), blk(qoff + step), blk(qoff + 2 * step)],
        out_specs=[pl.BlockSpec((None, length, width), lambda i, r, p: (i, 0, r * step + p)),
                   pl.BlockSpec((None, length, LANES), lambda i, r, p: (i, 0, r))],
        out_shape=[jax.ShapeDtypeStruct((b, length, dilation * MIX), F32),
                   jax.ShapeDtypeStruct((b, length, dilation * LANES), F32)],
        compiler_params=_params(("parallel", "parallel", "arbitrary")),
        name=f"dilated_d{dilation}",
    )(view, view, view)
    return o.reshape(b * seq, MIX), lse.reshape(b * seq, LANES)


def _even_out_kernel(x_ref, oa_ref, o1_ref, o2_ref, o3_ref, l1_ref, l2_ref, l3_ref, e_ref, w_ref, y_ref):
    l1, l2, l3 = l1_ref[...], l2_ref[...], l3_ref[...]
    mx = jnp.maximum(jnp.maximum(l1, l2), l3)
    e1, e2, e3 = jnp.exp(l1 - mx), jnp.exp(l2 - mx), jnp.exp(l3 - mx)
    den = e1 + e2 + e3
    expand = e_ref[...]

    def spread(wt):
        hi = wt.astype(BF16)
        lo = (wt - hi.astype(F32)).astype(BF16)
        return (jnp.dot(hi, expand, preferred_element_type=F32)
                + jnp.dot(lo, expand, preferred_element_type=F32))

    ob = spread(e1 / den) * o1_ref[...] + spread(e2 / den) * o2_ref[...] + spread(e3 / den) * o3_ref[...]
    o = jnp.concatenate([oa_ref[...], ob.astype(BF16)], axis=1)
    y_ref[...] = x_ref[...] + jnp.dot(o, w_ref[...], preferred_element_type=F32)


def _even_out(x2d, oa, os_, ls, expand, w):
    n = x2d.shape[0]
    tm = TOKEN_TILE
    row = lambda width: pl.BlockSpec((tm, width), lambda i: (i, 0))
    return pl.pallas_call(
        _even_out_kernel,
        grid=(n // tm,),
        in_specs=[row(D_MODEL), row(MIX), row(MIX), row(MIX), row(MIX), row(LANES), row(LANES), row(LANES),
                  _resident((LANES, MIX)), _resident((2 * MIX, D_MODEL))],
        out_specs=row(D_MODEL),
        out_shape=jax.ShapeDtypeStruct((n, D_MODEL), F32),
        compiler_params=_params(("parallel",)),
        name="even_out",
    )(x2d, oa, *os_, *ls, expand, w)


def _ffn_kernel(x_ref, xn_ref, xp_ref, g_ref, wu_ref, cp_ref, wd_ref, fg_ref, o_ref, h_ref, act_ref, *,
                tm, tiles_per_seq, final):
    pos = pl.program_id(0) % tiles_per_seq
    g = g_ref[...]
    x = x_ref[...]
    keep_prev = jnp.where(pos == 0, 0.0, 1.0)
    keep_next = jnp.where(pos == tiles_per_seq - 1, 0.0, 1.0)
    h_ref[0:tm, :] = _rms(x, g).astype(BF16)
    halo = jnp.concatenate([_rms(xn_ref[...], g) * keep_next, _rms(xp_ref[...], g) * keep_prev], axis=0)
    h_ref[tm:tm + 2 * HALO, :] = halo.astype(BF16)
    h = h_ref[...]
    ext = tm + 2 * HALO
    width = 2 * FFN_CHUNK
    for c in range(N_FFN_CHUNKS):
        a = jnp.dot(h, wu_ref[:, c * width:(c + 1) * width], preferred_element_type=F32)
        cp = cp_ref[:, c * width:(c + 1) * width]
        a_prev = pltpu.roll(a, 1, axis=0)[:tm]
        a_next = pltpu.roll(a, ext - 1, axis=0)[:tm]
        y = cp[3:4] + a_prev * cp[0:1]
        y = y + a[:tm] * cp[1:2]
        y = y + a_next * cp[2:3]
        act = y[:, :FFN_CHUNK] * jax.nn.gelu(y[:, FFN_CHUNK:])
        act_ref[:, c * FFN_CHUNK:(c + 1) * FFN_CHUNK] = act.astype(BF16)
    out = x + jnp.dot(act_ref[...], wd_ref[...], preferred_element_type=F32)
    if final:
        out = _rms(out, fg_ref[...])
    o_ref[...] = out


def _ffn(x2d, g, wu, cp, wd, fg, seq, final):
    n = x2d.shape[0]
    tm = TOKEN_TILE
    tps = seq // tm
    hb = tm // HALO
    last = n // HALO - 1
    return pl.pallas_call(
        functools.partial(_ffn_kernel, tm=tm, tiles_per_seq=tps, final=final),
        grid=(n // tm,),
        in_specs=[
            pl.BlockSpec((tm, D_MODEL), lambda i: (i, 0)),
            pl.BlockSpec((HALO, D_MODEL), lambda i: (jnp.minimum((i + 1) * hb, last), 0)),
            pl.BlockSpec((HALO, D_MODEL), lambda i: (jnp.maximum(i * hb - 1, 0), 0)),
            _resident((1, D_MODEL)),
            _resident((D_MODEL, 2 * FFN_DIM)),
            _resident((8, 2 * FFN_DIM)),
            _resident((FFN_DIM, D_MODEL)),
            _resident((1, D_MODEL)),
        ],
        out_specs=pl.BlockSpec((tm, D_MODEL), lambda i: (i, 0)),
        out_shape=jax.ShapeDtypeStruct((n, D_MODEL), F32),
        scratch_shapes=[pltpu.VMEM((tm + 2 * HALO, D_MODEL), BF16), pltpu.VMEM((tm, FFN_DIM), BF16)],
        compiler_params=_params(("parallel",)),
        name="conv_ffn",
    )(x2d, x2d, x2d, g, wu, cp, wd, fg)


def _ret_proj_kernel(x_ref, g_ref, w_ref, cos_ref, sin_ref, qkv_ref, gate_ref):
    h = _rms(x_ref[...], g_ref[...]).astype(BF16)
    n_qkv = (2 * RET_Q + RET_V) // MIX
    for c in range((2 * RET_Q + 2 * RET_V) // MIX):
        a = jnp.dot(h, w_ref[:, c * MIX:(c + 1) * MIX], preferred_element_type=F32)
        if c < 2 * RET_Q // MIX:
            cos, sin = cos_ref[...], sin_ref[...]
            parts = []
            for j in range(MIX // RET_QK_DIM):
                x1 = a[:, j * RET_QK_DIM:j * RET_QK_DIM + LANES]
                x2 = a[:, j * RET_QK_DIM + LANES:(j + 1) * RET_QK_DIM]
                parts += [x1 * cos - x2 * sin, x1 * sin + x2 * cos]
            a = jnp.concatenate(parts, axis=1)
        if c < n_qkv:
            qkv_ref[:, c * MIX:(c + 1) * MIX] = a.astype(BF16)
        else:
            gate_ref[:, (c - n_qkv) * MIX:(c - n_qkv + 1) * MIX] = a


def _ret_proj(x2d, g, w, cos, sin, seq):
    n = x2d.shape[0]
    tm = TOKEN_TILE
    tps = seq // tm
    return pl.pallas_call(
        _ret_proj_kernel,
        grid=(n // tm,),
        in_specs=[
            pl.BlockSpec((tm, D_MODEL), lambda i: (i, 0)),
            _resident((1, D_MODEL)),
            _resident((D_MODEL, 2 * RET_Q + 2 * RET_V)),
            pl.BlockSpec((tm, LANES), lambda i: (i % tps, 0)),
            pl.BlockSpec((tm, LANES), lambda i: (i % tps, 0)),
        ],
        out_specs=[pl.BlockSpec((tm, 2 * RET_Q + RET_V), lambda i: (i, 0)),
                   pl.BlockSpec((tm, RET_V), lambda i: (i, 0))],
        out_shape=[jax.ShapeDtypeStruct((n, 2 * RET_Q + RET_V), BF16),
                   jax.ShapeDtypeStruct((n, RET_V), F32)],
        compiler_params=_params(("parallel",)),
        name="ret_proj",
    )(x2d, g, w, cos, sin)


def _ret_kernel(lg_ref, q_ref, k_ref, v_ref, o_ref, sf_ref, sb_ref, *, seq):
    c = RET_CHUNK
    n = seq // c
    hd = pl.program_id(1)
    lgf = lg_ref[0, hd]
    lgb = lg_ref[1, hd]
    pr = lax.broadcasted_iota(jnp.int32, (c, c), 0).astype(F32)
    pc = lax.broadcasted_iota(jnp.int32, (c, c), 1).astype(F32)
    diff = pr - pc
    lower = diff >= 0
    d_f = jnp.where(lower, jnp.exp(lgf * jnp.where(lower, diff, 0.0)), 0.0)
    d_b = jnp.where(lower, 0.0, jnp.exp(lgb * jnp.where(lower, 0.0, -diff)))
    d_fb = d_f + d_b
    pos = lax.broadcasted_iota(jnp.int32, (c, 1), 0).astype(F32)
    qd_f = jnp.exp(lgf * (pos + 1.0))
    kd_f = jnp.exp(lgf * (c - 1.0 - pos))
    qd_b = jnp.exp(lgb * (c - pos))
    kd_b = jnp.exp(lgb * pos)
    ones = jnp.ones((1, LANES), F32)
    sd_f = jnp.exp(lgf * c * ones)[:, :1]
    sd_b = jnp.exp(lgb * c * ones)[:, :1]
    tn = (((0,), (0,)), ((), ()))
    nt = (((1,), (1,)), ((), ()))

    def fwd_part(i, accumulate):
        r0 = pl.multiple_of(i * c, c)
        q = q_ref[pl.ds(r0, c), :]
        k = k_ref[pl.ds(r0, c), :]
        v = v_ref[pl.ds(r0, c), :]
        s = lax.dot_general(q, k, nt, preferred_element_type=F32)
        inner = (s * d_fb).astype(BF16)
        qd = (q.astype(F32) * qd_f).astype(BF16)
        o = (jnp.dot(inner, v, preferred_element_type=F32)
             + jnp.dot(qd, sf_ref[...].astype(BF16), preferred_element_type=F32))
        if accumulate:
            o_ref[pl.ds(r0, c), :] += o
        else:
            o_ref[pl.ds(r0, c), :] = o
        kd = (k.astype(F32) * kd_f).astype(BF16)
        sf_ref[...] = sf_ref[...] * sd_f + lax.dot_general(kd, v, tn, preferred_element_type=F32)

    def bwd_part(i, accumulate):
        r0 = pl.multiple_of(i * c, c)
        q = q_ref[pl.ds(r0, c), :]
        k = k_ref[pl.ds(r0, c), :]
        v = v_ref[pl.ds(r0, c), :]
        qd = (q.astype(F32) * qd_b).astype(BF16)
        o = jnp.dot(qd, sb_ref[...].astype(BF16), preferred_element_type=F32)
        if accumulate:
            o_ref[pl.ds(r0, c), :] += o
        else:
            o_ref[pl.ds(r0, c), :] = o
        kd = (k.astype(F32) * kd_b).astype(BF16)
        sb_ref[...] = sb_ref[...] * sd_b + lax.dot_general(kd, v, tn, preferred_element_type=F32)

    sf_ref[...] = jnp.zeros_like(sf_ref)
    sb_ref[...] = jnp.zeros_like(sb_ref)

    def first_half(j, carry):
        fwd_part(j, False)
        bwd_part(n - 1 - j, False)
        return carry

    def second_half(j, carry):
        fwd_part(j, True)
        bwd_part(n - 1 - j, True)
        return carry

    lax.fori_loop(0, n // 2, first_half, 0)
    lax.fori_loop(n // 2, n, second_half, 0)


def _retention(qkv, log_gamma, seq):
    b = qkv.shape[0]
    kblk = RET_Q // RET_QK_DIM
    vblk = 2 * RET_Q // RET_V_DIM
    return pl.pallas_call(
        functools.partial(_ret_kernel, seq=seq),
        grid=(b, RET_HEADS),
        in_specs=[
            pl.BlockSpec(memory_space=pltpu.SMEM),
            pl.BlockSpec((None, seq, RET_QK_DIM), lambda i, h: (i, 0, h)),
            pl.BlockSpec((None, seq, RET_QK_DIM), lambda i, h: (i, 0, kblk + h)),
            pl.BlockSpec((None, seq, RET_V_DIM), lambda i, h: (i, 0, vblk + h)),
        ],
        out_specs=pl.BlockSpec((None, seq, RET_V_DIM), lambda i, h: (i, 0, h)),
        out_shape=jax.ShapeDtypeStruct((b, seq, RET_V), F32),
        scratch_shapes=[pltpu.VMEM((RET_QK_DIM, RET_V_DIM), F32), pltpu.VMEM((RET_QK_DIM, RET_V_DIM), F32)],
        compiler_params=_params(("parallel", "parallel")),
        name="retention",
    )(log_gamma, qkv, qkv, qkv)


def _ret_out_kernel(x_ref, r_ref, g_ref, w_ref, y_ref):
    parts = []
    for hd in range(RET_HEADS):
        cols = slice(hd * RET_V_DIM, (hd + 1) * RET_V_DIM)
        r = r_ref[:, cols]
        mu = jnp.mean(r, axis=-1, keepdims=True)
        var = jnp.mean(jnp.square(r - mu), axis=-1, keepdims=True)
        rn = (r - mu) * lax.rsqrt(var + NORM_EPS)
        gate = g_ref[:, cols]
        parts.append((gate / (1.0 + jnp.exp(-gate)) * rn).astype(BF16))
    act = jnp.concatenate(parts, axis=1)
    y_ref[...] = x_ref[...] + jnp.dot(act, w_ref[...], preferred_element_type=F32)


def _ret_out(x2d, r, gate, w):
    n = x2d.shape[0]
    tm = TOKEN_TILE
    row = lambda width: pl.BlockSpec((tm, width), lambda i: (i, 0))
    return pl.pallas_call(
        _ret_out_kernel,
        grid=(n // tm,),
        in_specs=[row(D_MODEL), row(RET_V), row(RET_V), _resident((RET_V, D_MODEL))],
        out_specs=row(D_MODEL),
        out_shape=jax.ShapeDtypeStruct((n, D_MODEL), F32),
        compiler_params=_params(("parallel",)),
        name="ret_out",
    )(x2d, r, gate, w)


def _rope_tables(seq, dh, lane_freq, sign):
    inv = 1.0 / (ROPE_THETA ** (jnp.arange(0, dh, 2, dtype=F32) / dh))
    ang = jnp.arange(seq, dtype=F32)[:, None] * inv[None, :]
    return jnp.cos(ang)[:, lane_freq], jnp.sin(ang)[:, lane_freq] * sign[None, :]


def _prepare(attn_norm, even_w_in, na_rpb, even_w_out, ret_w_in, ret_decay_fwd, ret_decay_bwd, ret_w_out,
             ffn_norm, ffn_w_up, ffn_conv_w, ffn_conv_b, ffn_w_down, final_norm, seq):
    lane = jnp.arange(LANES)
    half = HEAD_DIM // 2
    seg = lane // half
    pair_cols = (seg % 2) * HEAD_DIM + (seg // 2) * half + lane % half
    rot_cols = (jnp.arange(DIL_HEADS // 2)[:, None] * LANES + pair_cols[None, :]).reshape(-1)
    w_in = even_w_in[0]
    blocks = [w_in[:, i * MIX:(i + 1) * MIX] for i in range(6)]
    scale = HEAD_DIM ** -0.5
    blocks[0] = blocks[0] * scale
    blocks[3] = blocks[3][:, rot_cols] * scale
    blocks[4] = blocks[4][:, rot_cols]
    p = {}
    p["even_w_in"] = jnp.concatenate(blocks, axis=1).astype(BF16)
    p["even_cos"], p["even_sin"] = _rope_tables(seq, HEAD_DIM, lane % half,
                                                jnp.where(lane < LANES // 2, -1.0, 1.0))
    p["na_bias"] = _na_bias_table(na_rpb[0], seq // GRID_W)
    p["expand"] = (lane[:, None] == (jnp.arange(MIX) // HEAD_DIM)[None, :]).astype(BF16)
    p["even_w_out"] = even_w_out[0].astype(BF16)

    rw = ret_w_in[0]
    rscale = RET_QK_DIM ** -0.5
    p["ret_w_in"] = jnp.concatenate([rw[:, :RET_Q] * rscale, rw[:, RET_Q:]], axis=1).astype(BF16)
    p["ret_cos"], p["ret_sin"] = _rope_tables(seq, RET_QK_DIM, lane, jnp.ones((LANES,), F32))
    p["ret_log_gamma"] = jnp.stack([-jax.nn.softplus(ret_decay_fwd[0].astype(F32)),
                                    -jax.nn.softplus(ret_decay_bwd[0].astype(F32))], axis=0)
    p["ret_w_out"] = ret_w_out[0].astype(BF16)

    ch = jnp.arange(N_FFN_CHUNKS)[:, None] * FFN_CHUNK + jnp.arange(FFN_CHUNK)[None, :]
    ffn_cols = jnp.concatenate([ch, ch + FFN_DIM], axis=1).reshape(-1)
    p["ffn_w_up"] = [ffn_w_up[l][:, ffn_cols].astype(BF16) for l in range(2)]
    p["ffn_conv"] = [jnp.concatenate([ffn_conv_w[l][:, ffn_cols], ffn_conv_b[l][None, ffn_cols],
                                      jnp.zeros((8 - CONV_WIDTH - 1, 2 * FFN_DIM), F32)], axis=0) for l in range(2)]
    p["ffn_w_down"] = [ffn_w_down[l].astype(BF16) for l in range(2)]
    p["attn_norm"] = [attn_norm[l][None, :] for l in range(2)]
    p["ffn_norm"] = [ffn_norm[l][None, :] for l in range(2)]
    p["final_norm"] = final_norm[None, :]
    return p


def _trunk(x, p):
    b, seq, _ = x.shape
    x2d = x.reshape(b * seq, D_MODEL)
    qkv = _even_proj(x2d, p["attn_norm"][0], p["even_w_in"], p["even_cos"], p["even_sin"], seq)
    qkv = qkv.reshape(b, seq, EVEN_IN)
    oa = _na_attention(qkv, p["na_bias"], seq).reshape(b * seq, MIX)
    os_, ls = [], []
    for window, dilation in DIL_BRANCHES:
        o, lse = _dil_branch(qkv, seq, dilation, (window // 2) // dilation)
        os_.append(o)
        ls.append(lse)
    x2d = _even_out(x2d, oa, os_, ls, p["expand"], p["even_w_out"])
    x2d = _ffn(x2d, p["ffn_norm"][0], p["ffn_w_up"][0], p["ffn_conv"][0], p["ffn_w_down"][0],
               p["final_norm"], seq, final=False)
    rqkv, gate = _ret_proj(x2d, p["attn_norm"][1], p["ret_w_in"], p["ret_cos"], p["ret_sin"], seq)
    r = _retention(rqkv.reshape(b, seq, 2 * RET_Q + RET_V), p["ret_log_gamma"], seq)
    x2d = _ret_out(x2d, r.reshape(b * seq, RET_V), gate, p["ret_w_out"])
    x2d = _ffn(x2d, p["ffn_norm"][1], p["ffn_w_up"][1], p["ffn_conv"][1], p["ffn_w_down"][1],
               p["final_norm"], seq, final=True)
    return x2d.reshape(b, seq, D_MODEL)


def kernel(x_prompt, x_sample, attn_norm, even_w_in, na_rpb, even_w_out, ret_w_in, ret_decay_fwd, ret_decay_bwd,
           ret_w_out, ffn_norm, ffn_w_up, ffn_conv_w, ffn_conv_b, ffn_w_down, final_norm):
    assert x_prompt.shape[1] == x_sample.shape[1]
    p = _prepare(attn_norm, even_w_in, na_rpb, even_w_out, ret_w_in, ret_decay_fwd, ret_decay_bwd, ret_w_out,
                 ffn_norm, ffn_w_up, ffn_conv_w, ffn_conv_b, ffn_w_down, final_norm, x_prompt.shape[1])
    return _trunk(x_prompt, p), _trunk(x_sample, p)
```

```python
import functools

import jax
import jax.numpy as jnp
from jax import lax
from jax.experimental import pallas as pl
from jax.experimental.pallas import tpu as pltpu

D_MODEL = 1024
GRID_W = 64
HEAD_DIM = 64
NA_HEADS = 8
NA_WIN_R = 8
NA_WIN_C = 16
DIL_HEADS = 8
DIL_BRANCHES = ((128, 1), (512, 4), (2048, 16))
DIL_QBLOCK = 128
RET_HEADS = 4
RET_QK_DIM = 256
RET_V_DIM = 512
FFN_DIM = 2816
CONV_WIDTH = 3
ROPE_THETA = 10000.0
NORM_EPS = 1e-6
NEG_INF = -1e30

LANES = 128
SUBLANES = 8
MIX = NA_HEADS * HEAD_DIM
RET_Q = RET_HEADS * RET_QK_DIM
RET_V = RET_HEADS * RET_V_DIM
RET_BLOCK = 256
FFN_CHUNK = 256
N_FFN_CHUNKS = FFN_DIM // FFN_CHUNK
TOKEN_TILE = 512
ATTN_GROUP = 4
HALO = SUBLANES
RESIDUES = max(d for _, d in DIL_BRANCHES)
VMEM_LIMIT = 56 * 1024 * 1024
GELU_C0 = 0.7978845608028654
GELU_C1 = GELU_C0 * 0.044715

F32 = jnp.float32
BF16 = jnp.bfloat16
NT_DIMS = (((1,), (1,)), ((), ()))
TN_DIMS = (((0,), (0,)), ((), ()))


def _params(sem, vmem=VMEM_LIMIT):
    return pltpu.CompilerParams(dimension_semantics=sem, vmem_limit_bytes=vmem)


def _resident(shape):
    nd = len(shape)
    return pl.BlockSpec(shape, lambda *_: (0,) * nd, pipeline_mode=pl.Buffered(1))


def _rms(x, g):
    ms = jnp.mean(x * x, axis=-1, keepdims=True)
    return x * lax.rsqrt(ms + NORM_EPS) * g


def _software_pipeline(n_groups, scores, softmax, values):
    assert n_groups % 2 == 0 and n_groups >= 4
    scores(0, 0)
    scores(1, 1)
    softmax(0, 0)

    def body(gg, carry):
        g = 2 * gg
        scores(g + 2, 0)
        softmax(g + 1, 1)
        values(g, 0)
        scores(g + 3, 1)
        softmax(g + 2, 0)
        values(g + 1, 1)
        return carry

    lax.fori_loop(0, n_groups // 2 - 1, body, 0)
    softmax(n_groups - 1, 1)
    values(n_groups - 2, 0)
    values(n_groups - 1, 1)


def _even_proj_kernel(x_ref, g_ref, w_ref, cos_ref, sin_ref, na_ref, perm_ref, slab_ref, *, tm):
    h = _rms(x_ref[...], g_ref[...]).astype(BF16)
    per_res = tm // RESIDUES
    tiles = MIX // LANES
    for c in range(6):
        a = jnp.dot(h, w_ref[:, c * MIX:(c + 1) * MIX], preferred_element_type=F32)
        if c in (3, 4):
            cos, sin = cos_ref[...], sin_ref[...]
            parts = []
            for j in range(tiles):
                aj = a[:, j * LANES:(j + 1) * LANES]
                parts.append(aj * cos + pltpu.roll(aj, LANES // 2, axis=1) * sin)
            a = jnp.concatenate(parts, axis=1)
        if c < 3:
            na_ref[:, c * MIX:(c + 1) * MIX] = a.astype(BF16)
        else:
            slab = slab_ref.at[c % 2]
            for j in range(tiles):
                slab[j] = a[:, j * LANES:(j + 1) * LANES]
            for r in range(RESIDUES):
                for j in range(tiles):
                    col = (c - 3) * MIX + j * LANES
                    rows = slab[j, pl.ds(r, per_res, stride=RESIDUES), :]
                    perm_ref[r, :, col:col + LANES] = rows.astype(BF16)


def _even_proj(x2d, g, w, cos, sin, seq):
    n = x2d.shape[0]
    b = n // seq
    tm = TOKEN_TILE
    tps = seq // tm
    per_res = tm // RESIDUES
    return pl.pallas_call(
        functools.partial(_even_proj_kernel, tm=tm),
        grid=(n // tm,),
        in_specs=[
            pl.BlockSpec((tm, D_MODEL), lambda i: (i, 0)),
            _resident((1, D_MODEL)),
            _resident((D_MODEL, 6 * MIX)),
            pl.BlockSpec((tm, LANES), lambda i: (i % tps, 0)),
            pl.BlockSpec((tm, LANES), lambda i: (i % tps, 0)),
        ],
        out_specs=[pl.BlockSpec((tm, 3 * MIX), lambda i: (i, 0)),
                   pl.BlockSpec((None, RESIDUES, per_res, 3 * MIX), lambda i: (i // tps, 0, i % tps, 0))],
        out_shape=[jax.ShapeDtypeStruct((n, 3 * MIX), BF16),
                   jax.ShapeDtypeStruct((b, RESIDUES, seq // RESIDUES, 3 * MIX), BF16)],
        scratch_shapes=[pltpu.VMEM((2, MIX // LANES, tm, LANES), F32)],
        compiler_params=_params(("parallel",)),
        name="even_proj",
    )(x2d, g, w, cos, sin)


def _na_kernel(q_ref, k_ref, v_ref, b_ref, o_ref, s0_ref, s1_ref, p0_ref, p1_ref, r0_ref, r1_ref, *, rows):
    lane = lax.broadcasted_iota(jnp.int32, (GRID_W, LANES), 1)
    is_a = lane < HEAD_DIM
    win = NA_WIN_R * GRID_W
    s_bufs, p_bufs, r_bufs = (s0_ref, s1_ref), (p0_ref, p1_ref), (r0_ref, r1_ref)

    def window_start(r):
        rs = jnp.clip(r - NA_WIN_R // 2, 0, rows - NA_WIN_R)
        return pl.multiple_of(rs * GRID_W, GRID_W)

    def scores(g, slot):
        for u in range(ATTN_GROUP):
            r = jnp.int32(g * ATTN_GROUP + u)
            pat = jnp.where(r < NA_WIN_R // 2, r,
                            jnp.where(r > rows - NA_WIN_R // 2, r - (rows - NA_WIN_R), NA_WIN_R // 2))
            q = q_ref[pl.ds(pl.multiple_of(r * GRID_W, GRID_W), GRID_W), :]
            zero = jnp.zeros_like(q)
            q2 = jnp.concatenate([jnp.where(is_a, q, zero), jnp.where(is_a, zero, q)], axis=0)
            kw = k_ref[pl.ds(window_start(r), win), :]
            s = lax.dot_general(q2, kw, NT_DIMS, preferred_element_type=F32)
            s_bufs[slot][u] = s + b_ref[pat]

    def softmax(g, slot):
        for u in range(ATTN_GROUP):
            s = s_bufs[slot][u]
            m = jnp.max(s, axis=1, keepdims=True)
            e = jnp.exp(s - m)
            l = jnp.sum(e, axis=1, keepdims=True)
            p_bufs[slot][u] = e.astype(BF16)
            r_bufs[slot][u] = jnp.broadcast_to(1.0 / l, (2 * GRID_W, LANES))

    def values(g, slot):
        for u in range(ATTN_GROUP):
            r = jnp.int32(g * ATTN_GROUP + u)
            vw = v_ref[pl.ds(window_start(r), win), :]
            pv = jnp.dot(p_bufs[slot][u], vw, preferred_element_type=F32) * r_bufs[slot][u]
            o_ref[pl.ds(pl.multiple_of(r * GRID_W, GRID_W), GRID_W), :] = (
                jnp.where(is_a, pv[:GRID_W], pv[GRID_W:]).astype(BF16))

    _software_pipeline(rows // ATTN_GROUP, scores, softmax, values)


def _na_attention(qkv, bias, seq):
    b = qkv.shape[0]
    rows = seq // GRID_W
    npair = NA_HEADS // 2
    win = NA_WIN_R * GRID_W
    blk = lambda off: pl.BlockSpec((None, seq, LANES), lambda p, i: (i, 0, off + p))
    s_buf = pltpu.VMEM((ATTN_GROUP, 2 * GRID_W, win), F32)
    p_buf = pltpu.VMEM((ATTN_GROUP, 2 * GRID_W, win), BF16)
    r_buf = pltpu.VMEM((ATTN_GROUP, 2 * GRID_W, LANES), F32)
    return pl.pallas_call(
        functools.partial(_na_kernel, rows=rows),
        grid=(npair, b),
        in_specs=[blk(0), blk(npair), blk(2 * npair),
                  pl.BlockSpec((NA_WIN_R, None, 2 * GRID_W, win), lambda p, i: (0, p, 0, 0))],
        out_specs=pl.BlockSpec((None, seq, LANES), lambda p, i: (i, 0, p)),
        out_shape=jax.ShapeDtypeStruct((b, seq, MIX), BF16),
        scratch_shapes=[s_buf, s_buf, p_buf, p_buf, r_buf, r_buf],
        compiler_params=_params(("parallel", "parallel")),
        name="na_attention",
    )(qkv, qkv, qkv, bias)


def _na_bias_table(rpb, rows):
    c = jnp.arange(GRID_W)
    cs = jnp.clip(c - NA_WIN_C // 2, 0, GRID_W - NA_WIN_C)
    j = jnp.arange(GRID_W)
    valid = (j[None, :] >= cs[:, None]) & (j[None, :] < cs[:, None] + NA_WIN_C)
    coff = jnp.clip(j[None, :] - c[:, None] + NA_WIN_C - 1, 0, 2 * NA_WIN_C - 2)
    rpb32 = rpb.astype(F32)
    half = NA_WIN_R // 2
    reps = list(range(half)) + [half] + list(range(rows - half + 1, rows))
    tabs = []
    for r in reps:
        rs = min(max(r - half, 0), rows - NA_WIN_R)
        roff = rs + jnp.arange(NA_WIN_R) - r + NA_WIN_R - 1
        t = rpb32[:, roff][:, :, coff]
        t = jnp.where(valid[None, None], t, NEG_INF).transpose(0, 2, 1, 3)
        tabs.append(t.reshape(NA_HEADS // 2, 2 * GRID_W, NA_WIN_R * GRID_W))
    return jnp.stack(tabs, axis=0)


def _dil_kernel(q_ref, k_ref, v_ref, o_ref, qf_ref, kf_ref, vf_ref, acc_ref, m_ref, l_ref, nat_ref, mask_ref,
                s0_ref, s1_ref, p0_ref, p1_ref, m0_ref, m1_ref, l0_ref, l1_ref, *, slab_len):
    qb, kw_rows = DIL_QBLOCK, 2 * DIL_QBLOCK
    s_bufs, p_bufs = (s0_ref, s1_ref), (p0_ref, p1_ref)
    m_bufs, l_bufs = (m0_ref, m1_ref), (l0_ref, l1_ref)
    lane = lax.broadcasted_iota(jnp.int32, (qb, LANES), 1)
    is_a_out = lane < HEAD_DIM
    is_a_rot = (lane & (HEAD_DIM - 1)) < HEAD_DIM // 2
    row = lax.broadcasted_iota(jnp.int32, (2 * qb, kw_rows), 0) & (qb - 1)
    col = lax.broadcasted_iota(jnp.int32, (2 * qb, kw_rows), 1)

    qf_ref[...] = q_ref[...].astype(F32)
    kf_ref[...] = k_ref[...].astype(F32)
    vf_ref[...] = v_ref[...].astype(F32)

    branches = sorted(DIL_BRANCHES, key=lambda wd: -wd[1])
    for bi, (window, d) in enumerate(branches):
        g_cnt = RESIDUES // d
        rq, rk = qb // g_cnt, kw_rows // g_cnt
        half = (window // 2) // d
        lead = (rk - rq) // 2
        nblk = slab_len // rq
        base = g_cnt * ((col % rk) - (row % rq)) + (col // rk) - (row // rq)
        for pat, off in enumerate((0, -lead, rq - rk)):
            mask_ref[3 * bi + pat] = jnp.where(jnp.abs(base + g_cnt * off) <= half, 1.0, 0.0)

        def tile(g, u, d=d, g_cnt=g_cnt, rq=rq, rk=rk, lead=lead, nblk=nblk):
            it = jnp.int32(g * ATTN_GROUP + u)
            i = it % nblk
            q0 = i * rq
            ws = jnp.clip(q0 - lead, 0, slab_len - rk)
            pat = jnp.where(i == 0, 0, jnp.where(i == nblk - 1, 2, 1))
            slabs = [it // nblk + d * g_ for g_ in range(g_cnt)]
            return slabs, q0, pl.multiple_of(q0, rq), ws, pat

        def window(ref, ref32, slabs, ws, rq=rq, rk=rk):
            if rq % 16 == 0:
                wsa = pl.multiple_of(ws, 16)
                return jnp.concatenate([ref[s_, pl.ds(wsa, rk), :] for s_ in slabs], axis=0)
            return jnp.concatenate([ref32[s_, pl.ds(ws, rk), :] for s_ in slabs], axis=0).astype(BF16)

        def scores(g, slot, bi=bi, rq=rq, tile=tile, window=window):
            for u in range(ATTN_GROUP):
                slabs, _, q0a, ws, pat = tile(g, u)
                if rq % 16 == 0:
                    q = jnp.concatenate([q_ref[s_, pl.ds(q0a, rq), :] for s_ in slabs], axis=0)
                else:
                    q = jnp.concatenate([qf_ref[s_, pl.ds(q0a, rq), :] for s_ in slabs], axis=0).astype(BF16)
                kw = window(k_ref, kf_ref, slabs, ws)
                zero = jnp.zeros_like(q)
                q2 = jnp.concatenate([jnp.where(is_a_rot, q, zero), jnp.where(is_a_rot, zero, q)], axis=0)
                s = lax.dot_general(q2, kw, NT_DIMS, preferred_element_type=F32)
                s_bufs[slot][u] = jnp.where(mask_ref[3 * bi + pat] > 0.5, s, NEG_INF)

        def softmax(g, slot):
            for u in range(ATTN_GROUP):
                s = s_bufs[slot][u]
                m = jnp.max(s, axis=1, keepdims=True)
                e = jnp.exp(s - m)
                l = jnp.sum(e, axis=1, keepdims=True)
                p_bufs[slot][u] = e.astype(BF16)
                m_bufs[slot][u] = jnp.where(is_a_out, m[:qb], m[qb:])
                l_bufs[slot][u] = jnp.where(is_a_out, l[:qb], l[qb:])

        def values(g, slot, bi=bi, rq=rq, tile=tile, window=window):
            for u in range(ATTN_GROUP):
                slabs, q0, q0a, ws, _ = tile(g, u)
                vw = window(v_ref, vf_ref, slabs, ws)
                pv = jnp.dot(p_bufs[slot][u], vw, preferred_element_type=F32)
                pv_t = jnp.where(is_a_out, pv[:qb], pv[qb:])
                m_t = m_bufs[slot][u]
                l_t = l_bufs[slot][u]
                if bi > 0:
                    m_old = jnp.concatenate([m_ref[s_, pl.ds(q0a, rq), :] for s_ in slabs], axis=0)
                    l_old = jnp.concatenate([l_ref[s_, pl.ds(q0a, rq), :] for s_ in slabs], axis=0)
                    a_old = jnp.concatenate([acc_ref[s_, pl.ds(q0a, rq), :] for s_ in slabs], axis=0)
                    m_new = jnp.maximum(m_old, m_t)
                    w_old = jnp.exp(m_old - m_new)
                    w_new = jnp.exp(m_t - m_new)
                    l_t = l_old * w_old + l_t * w_new
                    pv_t = a_old * w_old + pv_t * w_new
                    m_t = m_new
                if bi < len(branches) - 1:
                    for g_, s_ in enumerate(slabs):
                        rows = slice(g_ * rq, (g_ + 1) * rq)
                        m_ref[s_, pl.ds(q0a, rq), :] = m_t[rows]
                        l_ref[s_, pl.ds(q0a, rq), :] = l_t[rows]
                        acc_ref[s_, pl.ds(q0a, rq), :] = pv_t[rows]
                else:
                    out = pv_t / l_t
                    t0 = q0 * RESIDUES
                    for g_, s_ in enumerate(slabs):
                        nat_ref[pl.ds(t0 + s_, rq, stride=RESIDUES), :] = out[g_ * rq:(g_ + 1) * rq]
                    t0a = pl.multiple_of(t0, qb)
                    o_ref[pl.ds(t0a, qb), :] = nat_ref[pl.ds(t0a, qb), :].astype(BF16)

        _software_pipeline(d * nblk // ATTN_GROUP, scores, softmax, values)


def _dil_attention(perm, seq):
    b = perm.shape[0]
    slab_len = seq // RESIDUES
    npair = DIL_HEADS // 2
    blk = lambda off: pl.BlockSpec((None, RESIDUES, slab_len, LANES), lambda i, p: (i, 0, 0, off + p))
    slab = pltpu.VMEM((RESIDUES, slab_len, LANES), F32)
    s_buf = pltpu.VMEM((ATTN_GROUP, 2 * DIL_QBLOCK, 2 * DIL_QBLOCK), F32)
    p_buf = pltpu.VMEM((ATTN_GROUP, 2 * DIL_QBLOCK, 2 * DIL_QBLOCK), BF16)
    stat = pltpu.VMEM((ATTN_GROUP, DIL_QBLOCK, LANES), F32)
    return pl.pallas_call(
        functools.partial(_dil_kernel, slab_len=slab_len),
        grid=(b, npair),
        in_specs=[blk(0), blk(npair), blk(2 * npair)],
        out_specs=pl.BlockSpec((None, seq, LANES), lambda i, p: (i, 0, p)),
        out_shape=jax.ShapeDtypeStruct((b, seq, MIX), BF16),
        scratch_shapes=[slab, slab, slab, slab, slab, slab,
                        pltpu.VMEM((seq, LANES), F32),
                        pltpu.VMEM((3 * len(DIL_BRANCHES), 2 * DIL_QBLOCK, 2 * DIL_QBLOCK), F32),
                        s_buf, s_buf, p_buf, p_buf, stat, stat, stat, stat],
        compiler_params=_params(("parallel", "parallel")),
        name="dilated_attention",
    )(perm, perm, perm)


def _even_out_kernel(x_ref, oa_ref, ob_ref, w_ref, y_ref):
    o = jnp.concatenate([oa_ref[...], ob_ref[...]], axis=1)
    y_ref[...] = x_ref[...] + jnp.dot(o, w_ref[...], preferred_element_type=F32)


def _even_out(x2d, oa, ob, w):
    n = x2d.shape[0]
    tm = TOKEN_TILE
    row = lambda width: pl.BlockSpec((tm, width), lambda i: (i, 0))
    return pl.pallas_call(
        _even_out_kernel,
        grid=(n // tm,),
        in_specs=[row(D_MODEL), row(MIX), row(MIX), _resident((2 * MIX, D_MODEL))],
        out_specs=row(D_MODEL),
        out_shape=jax.ShapeDtypeStruct((n, D_MODEL), F32),
        compiler_params=_params(("parallel",)),
        name="even_out",
    )(x2d, oa, ob, w)


def _ffn_kernel(x_ref, xn_ref, xp_ref, g_ref, wu_ref, cp_ref, wd_ref, fg_ref, o_ref, h_ref, a_ref, act_ref, *,
                tm, tiles_per_seq, final):
    pos = pl.program_id(0) % tiles_per_seq
    g = g_ref[...]
    x = x_ref[...]
    keep_prev = jnp.where(pos == 0, 0.0, 1.0)
    keep_next = jnp.where(pos == tiles_per_seq - 1, 0.0, 1.0)
    h_ref[0:tm, :] = _rms(x, g).astype(BF16)
    halo = jnp.concatenate([_rms(xn_ref[...], g) * keep_next, _rms(xp_ref[...], g) * keep_prev], axis=0)
    h_ref[tm:tm + 2 * HALO, :] = halo.astype(BF16)
    h = h_ref[...]
    width = 2 * FFN_CHUNK
    tiles = width // LANES
    for c in range(N_FFN_CHUNKS):
        a = jnp.dot(h, wu_ref[:, c * width:(c + 1) * width], preferred_element_type=F32)
        buf = a_ref.at[c % 2]
        ys = []
        for j in range(tiles):
            cols = slice(j * LANES, (j + 1) * LANES)
            buf[j, HALO:tm + 2 * HALO, :] = a[:tm + HALO, cols]
            buf[j, 0:HALO, :] = a[tm + HALO:, cols]
        for j in range(tiles):
            cols = slice(c * width + j * LANES, c * width + (j + 1) * LANES)
            y = cp_ref[3:4, cols] + buf[j, HALO - 1:HALO - 1 + tm, :] * cp_ref[0:1, cols]
            y = y + buf[j, HALO:HALO + tm, :] * cp_ref[1:2, cols]
            y = y + buf[j, HALO + 1:HALO + 1 + tm, :] * cp_ref[2:3, cols]
            ys.append(y)
        half_u = jnp.concatenate(ys[:tiles // 2], axis=1)
        gate = jnp.concatenate(ys[tiles // 2:], axis=1)
        t = jnp.tanh(gate * (GELU_C0 + GELU_C1 * (gate * gate)))
        w = half_u * gate
        act_ref[:, c * FFN_CHUNK:(c + 1) * FFN_CHUNK] = (w + w * t).astype(BF16)
    out = x + jnp.dot(act_ref[...], wd_ref[...], preferred_element_type=F32)
    if final:
        out = _rms(out, fg_ref[...])
    o_ref[...] = out


def _ffn(x2d, g, wu, cp, wd, fg, seq, final):
    n = x2d.shape[0]
    tm = TOKEN_TILE
    tps = seq // tm
    hb = tm // HALO
    last = n // HALO - 1
    return pl.pallas_call(
        functools.partial(_ffn_kernel, tm=tm, tiles_per_seq=tps, final=final),
        grid=(n // tm,),
        in_specs=[
            pl.BlockSpec((tm, D_MODEL), lambda i: (i, 0)),
            pl.BlockSpec((HALO, D_MODEL), lambda i: (jnp.minimum((i + 1) * hb, last), 0)),
            pl.BlockSpec((HALO, D_MODEL), lambda i: (jnp.maximum(i * hb - 1, 0), 0)),
            _resident((1, D_MODEL)),
            _resident((D_MODEL, 2 * FFN_DIM)),
            _resident((SUBLANES, 2 * FFN_DIM)),
            _resident((FFN_DIM, D_MODEL)),
            _resident((1, D_MODEL)),
        ],
        out_specs=pl.BlockSpec((tm, D_MODEL), lambda i: (i, 0)),
        out_shape=jax.ShapeDtypeStruct((n, D_MODEL), F32),
        scratch_shapes=[pltpu.VMEM((tm + 2 * HALO, D_MODEL), BF16),
                        pltpu.VMEM((2, 2 * FFN_CHUNK // LANES, tm + 2 * HALO, LANES), F32),
                        pltpu.VMEM((tm, FFN_DIM), BF16)],
        compiler_params=_params(("parallel",)),
        name="conv_ffn",
    )(x2d, x2d, x2d, g, wu, cp, wd, fg)


def _ret_proj_kernel(x_ref, g_ref, w_ref, cos_ref, sin_ref, qkv_ref, gate_ref):
    h = _rms(x_ref[...], g_ref[...]).astype(BF16)
    n_qkv = (2 * RET_Q + RET_V) // MIX
    for c in range((2 * RET_Q + 2 * RET_V) // MIX):
        a = jnp.dot(h, w_ref[:, c * MIX:(c + 1) * MIX], preferred_element_type=F32)
        if c < 2 * RET_Q // MIX:
            cos, sin = cos_ref[...], sin_ref[...]
            parts = []
            for j in range(MIX // RET_QK_DIM):
                x1 = a[:, j * RET_QK_DIM:j * RET_QK_DIM + LANES]
                x2 = a[:, j * RET_QK_DIM + LANES:(j + 1) * RET_QK_DIM]
                parts += [x1 * cos - x2 * sin, x1 * sin + x2 * cos]
            a = jnp.concatenate(parts, axis=1)
        if c < n_qkv:
            qkv_ref[:, c * MIX:(c + 1) * MIX] = a.astype(BF16)
        else:
            gate_ref[:, (c - n_qkv) * MIX:(c - n_qkv + 1) * MIX] = a


def _ret_proj(x2d, g, w, cos, sin, seq):
    n = x2d.shape[0]
    tm = TOKEN_TILE
    tps = seq // tm
    return pl.pallas_call(
        _ret_proj_kernel,
        grid=(n // tm,),
        in_specs=[
            pl.BlockSpec((tm, D_MODEL), lambda i: (i, 0)),
            _resident((1, D_MODEL)),
            _resident((D_MODEL, 2 * RET_Q + 2 * RET_V)),
            pl.BlockSpec((tm, LANES), lambda i: (i % tps, 0)),
            pl.BlockSpec((tm, LANES), lambda i: (i % tps, 0)),
        ],
        out_specs=[pl.BlockSpec((tm, 2 * RET_Q + RET_V), lambda i: (i, 0)),
                   pl.BlockSpec((tm, RET_V), lambda i: (i, 0))],
        out_shape=[jax.ShapeDtypeStruct((n, 2 * RET_Q + RET_V), BF16),
                   jax.ShapeDtypeStruct((n, RET_V), F32)],
        compiler_params=_params(("parallel",)),
        name="ret_proj",
    )(x2d, g, w, cos, sin)


def _ret_kernel(lg_ref, q_ref, k_ref, v_ref, o_ref, sf_ref, sb_ref, d_ref, *, seq):
    c = RET_BLOCK
    n = seq // c
    hd = pl.program_id(1)
    lgf = lg_ref[0, hd]
    lgb = lg_ref[1, hd]
    pr = lax.broadcasted_iota(jnp.int32, (c, c), 0).astype(F32)
    pc = lax.broadcasted_iota(jnp.int32, (c, c), 1).astype(F32)
    diff = pr - pc
    lower = diff >= 0
    d_f = jnp.where(lower, jnp.exp(lgf * jnp.where(lower, diff, 0.0)), 0.0)
    d_b = jnp.where(lower, 0.0, jnp.exp(lgb * jnp.where(lower, 0.0, -diff)))
    d_ref[...] = d_f + d_b
    pos = lax.broadcasted_iota(jnp.int32, (c, 1), 0).astype(F32)
    qd_f = jnp.exp(lgf * (pos + 1.0))
    kd_f = jnp.exp(lgf * (c - 1.0 - pos))
    qd_b = jnp.exp(lgb * (c - pos))
    kd_b = jnp.exp(lgb * pos)
    ones = jnp.ones((1, LANES), F32)
    sd_f = jnp.exp(lgf * c * ones)[:, :1]
    sd_b = jnp.exp(lgb * c * ones)[:, :1]

    def fwd_part(i, accumulate):
        r0 = pl.multiple_of(i * c, c)
        q = q_ref[pl.ds(r0, c), :]
        k = k_ref[pl.ds(r0, c), :]
        v = v_ref[pl.ds(r0, c), :]
        s = lax.dot_general(q, k, NT_DIMS, preferred_element_type=F32)
        inner = (s * d_ref[...]).astype(BF16)
        qd = (q.astype(F32) * qd_f).astype(BF16)
        o = (jnp.dot(inner, v, preferred_element_type=F32)
             + jnp.dot(qd, sf_ref[...].astype(BF16), preferred_element_type=F32))
        if accumulate:
            o_ref[pl.ds(r0, c), :] += o
        else:
            o_ref[pl.ds(r0, c), :] = o
        kd = (k.astype(F32) * kd_f).astype(BF16)
        sf_ref[...] = sf_ref[...] * sd_f + lax.dot_general(kd, v, TN_DIMS, preferred_element_type=F32)

    def bwd_part(i, accumulate):
        r0 = pl.multiple_of(i * c, c)
        q = q_ref[pl.ds(r0, c), :]
        k = k_ref[pl.ds(r0, c), :]
        v = v_ref[pl.ds(r0, c), :]
        qd = (q.astype(F32) * qd_b).astype(BF16)
        o = jnp.dot(qd, sb_ref[...].astype(BF16), preferred_element_type=F32)
        if accumulate:
            o_ref[pl.ds(r0, c), :] += o
        else:
            o_ref[pl.ds(r0, c), :] = o
        kd = (k.astype(F32) * kd_b).astype(BF16)
        sb_ref[...] = sb_ref[...] * sd_b + lax.dot_general(kd, v, TN_DIMS, preferred_element_type=F32)

    sf_ref[...] = jnp.zeros_like(sf_ref)
    sb_ref[...] = jnp.zeros_like(sb_ref)

    def first_half(j, carry):
        fwd_part(j, False)
        bwd_part(n - 1 - j, False)
        return carry

    def second_half(j, carry):
        fwd_part(j, True)
        bwd_part(n - 1 - j, True)
        return carry

    lax.fori_loop(0, n // 2, first_half, 0)
    lax.fori_loop(n // 2, n, second_half, 0)


def _retention(qkv, log_gamma, seq):
    b = qkv.shape[0]
    kblk = RET_Q // RET_QK_DIM
    vblk = 2 * RET_Q // RET_V_DIM
    state = pltpu.VMEM((RET_QK_DIM, RET_V_DIM), F32)
    return pl.pallas_call(
        functools.partial(_ret_kernel, seq=seq),
        grid=(b, RET_HEADS),
        in_specs=[
            pl.BlockSpec(memory_space=pltpu.SMEM),
            pl.BlockSpec((None, seq, RET_QK_DIM), lambda i, h: (i, 0, h)),
            pl.BlockSpec((None, seq, RET_QK_DIM), lambda i, h: (i, 0, kblk + h)),
            pl.BlockSpec((None, seq, RET_V_DIM), lambda i, h: (i, 0, vblk + h)),
        ],
        out_specs=pl.BlockSpec((None, seq, RET_V_DIM), lambda i, h: (i, 0, h)),
        out_shape=jax.ShapeDtypeStruct((b, seq, RET_V), F32),
        scratch_shapes=[state, state, pltpu.VMEM((RET_BLOCK, RET_BLOCK), F32)],
        compiler_params=_params(("parallel", "parallel")),
        name="retention",
    )(log_gamma, qkv, qkv, qkv)


def _ret_out_kernel(x_ref, r_ref, g_ref, w_ref, y_ref):
    parts = []
    for hd in range(RET_HEADS):
        cols = slice(hd * RET_V_DIM, (hd + 1) * RET_V_DIM)
        r = r_ref[:, cols]
        mu = jnp.mean(r, axis=-1, keepdims=True)
        var = jnp.mean(jnp.square(r - mu), axis=-1, keepdims=True)
        rn = (r - mu) * lax.rsqrt(var + NORM_EPS)
        gate = g_ref[:, cols]
        parts.append((gate / (1.0 + jnp.exp(-gate)) * rn).astype(BF16))
    act = jnp.concatenate(parts, axis=1)
    y_ref[...] = x_ref[...] + jnp.dot(act, w_ref[...], preferred_element_type=F32)


def _ret_out(x2d, r, gate, w):
    n = x2d.shape[0]
    tm = TOKEN_TILE
    row = lambda width: pl.BlockSpec((tm, width), lambda i: (i, 0))
    return pl.pallas_call(
        _ret_out_kernel,
        grid=(n // tm,),
        in_specs=[row(D_MODEL), row(RET_V), row(RET_V), _resident((RET_V, D_MODEL))],
        out_specs=row(D_MODEL),
        out_shape=jax.ShapeDtypeStruct((n, D_MODEL), F32),
        compiler_params=_params(("parallel",)),
        name="ret_out",
    )(x2d, r, gate, w)


def _rope_tables(seq, dh, lane_freq, sign):
    inv = 1.0 / (ROPE_THETA ** (jnp.arange(0, dh, 2, dtype=F32) / dh))
    ang = jnp.arange(seq, dtype=F32)[:, None] * inv[None, :]
    return jnp.cos(ang)[:, lane_freq], jnp.sin(ang)[:, lane_freq] * sign[None, :]


def _pair_rotary_layout(w):
    rows = w.shape[0]
    w = w.reshape(rows, DIL_HEADS // 2, 2, 2, HEAD_DIM // 2)
    return w.transpose(0, 1, 3, 2, 4).reshape(rows, MIX)


def _ffn_chunk_layout(w):
    rows = w.shape[0]
    w = w.reshape(rows, 2, N_FFN_CHUNKS, FFN_CHUNK)
    return w.transpose(0, 2, 1, 3).reshape(rows, 2 * FFN_DIM)


def _prepare(attn_norm, even_w_in, na_rpb, even_w_out, ret_w_in, ret_decay_fwd, ret_decay_bwd, ret_w_out,
             ffn_norm, ffn_w_up, ffn_conv_w, ffn_conv_b, ffn_w_down, final_norm, seq):
    lane = jnp.arange(LANES)
    half = HEAD_DIM // 2
    w_in = even_w_in[0]
    blocks = [w_in[:, i * MIX:(i + 1) * MIX] for i in range(6)]
    scale = HEAD_DIM ** -0.5
    blocks[0] = blocks[0] * scale
    blocks[3] = _pair_rotary_layout(blocks[3]) * scale
    blocks[4] = _pair_rotary_layout(blocks[4])
    p = {}
    p["even_w_in"] = jnp.concatenate(blocks, axis=1).astype(BF16)
    p["even_cos"], p["even_sin"] = _rope_tables(seq, HEAD_DIM, lane % half,
                                                jnp.where(lane < LANES // 2, -1.0, 1.0))
    p["na_bias"] = _na_bias_table(na_rpb[0], seq // GRID_W)
    p["even_w_out"] = even_w_out[0].astype(BF16)

    rw = ret_w_in[0]
    rscale = RET_QK_DIM ** -0.5
    p["ret_w_in"] = jnp.concatenate([rw[:, :RET_Q] * rscale, rw[:, RET_Q:]], axis=1).astype(BF16)
    p["ret_cos"], p["ret_sin"] = _rope_tables(seq, RET_QK_DIM, lane, jnp.ones((LANES,), F32))
    p["ret_log_gamma"] = jnp.stack([-jax.nn.softplus(ret_decay_fwd[0].astype(F32)),
                                    -jax.nn.softplus(ret_decay_bwd[0].astype(F32))], axis=0)
    p["ret_w_out"] = ret_w_out[0].astype(BF16)

    u_half = jnp.concatenate([jnp.full((FFN_DIM,), 0.5, F32), jnp.ones((FFN_DIM,), F32)])
    p["ffn_w_up"], p["ffn_conv"] = [], []
    for layer in range(2):
        p["ffn_w_up"].append(_ffn_chunk_layout(ffn_w_up[layer]).astype(BF16))
        taps = jnp.concatenate([ffn_conv_w[layer], ffn_conv_b[layer][None, :]], axis=0) * u_half[None, :]
        taps = jnp.concatenate([taps, jnp.zeros((SUBLANES - CONV_WIDTH - 1, 2 * FFN_DIM), F32)], axis=0)
        p["ffn_conv"].append(_ffn_chunk_layout(taps))
    p["ffn_w_down"] = [ffn_w_down[layer].astype(BF16) for layer in range(2)]
    p["attn_norm"] = [attn_norm[layer][None, :] for layer in range(2)]
    p["ffn_norm"] = [ffn_norm[layer][None, :] for layer in range(2)]
    p["final_norm"] = final_norm[None, :]
    return p


def _trunk(x, p):
    b, seq, _ = x.shape
    x2d = x.reshape(b * seq, D_MODEL)
    na_qkv, dil_qkv = _even_proj(x2d, p["attn_norm"][0], p["even_w_in"], p["even_cos"], p["even_sin"], seq)
    oa = _na_attention(na_qkv.reshape(b, seq, 3 * MIX), p["na_bias"], seq).reshape(b * seq, MIX)
    ob = _dil_attention(dil_qkv, seq).reshape(b * seq, MIX)
    x2d = _even_out(x2d, oa, ob, p["even_w_out"])
    x2d = _ffn(x2d, p["ffn_norm"][0], p["ffn_w_up"][0], p["ffn_conv"][0], p["ffn_w_down"][0],
               p["final_norm"], seq, final=False)
    rqkv, gate = _ret_proj(x2d, p["attn_norm"][1], p["ret_w_in"], p["ret_cos"], p["ret_sin"], seq)
    r = _retention(rqkv.reshape(b, seq, 2 * RET_Q + RET_V), p["ret_log_gamma"], seq)
    x2d = _ret_out(x2d, r.reshape(b * seq, RET_V), gate, p["ret_w_out"])
    x2d = _ffn(x2d, p["ffn_norm"][1], p["ffn_w_up"][1], p["ffn_conv"][1], p["ffn_w_down"][1],
               p["final_norm"], seq, final=True)
    return x2d.reshape(b, seq, D_MODEL)


def kernel(x_prompt, x_sample, attn_norm, even_w_in, na_rpb, even_w_out, ret_w_in, ret_decay_fwd, ret_decay_bwd,
           ret_w_out, ffn_norm, ffn_w_up, ffn_conv_w, ffn_conv_b, ffn_w_down, final_norm):
    assert x_prompt.shape[1] == x_sample.shape[1]
    p = _prepare(attn_norm, even_w_in, na_rpb, even_w_out, ret_w_in, ret_decay_fwd, ret_decay_bwd, ret_w_out,
                 ffn_norm, ffn_w_up, ffn_conv_w, ffn_conv_b, ffn_w_down, final_norm, x_prompt.shape[1])
    return _trunk(x_prompt, p), _trunk(x_sample, p)
```

```python
import functools

import jax
import jax.numpy as jnp
from jax import lax
from jax.experimental import pallas as pl
from jax.experimental.pallas import tpu as pltpu

D_MODEL = 1024
GRID_W = 64
HEAD_DIM = 64
NA_HEADS = 8
NA_WIN_R = 8
NA_WIN_C = 16
DIL_HEADS = 8
DIL_BRANCHES = ((128, 1), (512, 4), (2048, 16))
DIL_QBLOCK = 128
RET_HEADS = 4
RET_QK_DIM = 256
RET_V_DIM = 512
FFN_DIM = 2816
CONV_WIDTH = 3
ROPE_THETA = 10000.0
NORM_EPS = 1e-6
NEG_INF = -1e30

LANES = 128
SUBLANES = 8
MIX = NA_HEADS * HEAD_DIM
RET_Q = RET_HEADS * RET_QK_DIM
RET_V = RET_HEADS * RET_V_DIM
RET_IN = 2 * RET_Q + 2 * RET_V
RET_BLOCK = 256
FFN_CHUNK = 256
N_FFN_CHUNKS = FFN_DIM // FFN_CHUNK
TOKEN_TILE = 512
FFN_TILE = 512
ATTN_GROUP = 4
HALO = SUBLANES
RESIDUES = max(d for _, d in DIL_BRANCHES)
VMEM_LIMIT = 56 * 1024 * 1024
LOG2_E = 1.4426950408889634
GELU_C0 = 0.7978845608028654
GELU_C1 = GELU_C0 * 0.044715

F32 = jnp.float32
BF16 = jnp.bfloat16
NT_DIMS = (((1,), (1,)), ((), ()))
TN_DIMS = (((0,), (0,)), ((), ()))


def _params(sem, vmem=VMEM_LIMIT):
    return pltpu.CompilerParams(dimension_semantics=sem, vmem_limit_bytes=vmem)


def _resident(shape):
    nd = len(shape)
    return pl.BlockSpec(shape, lambda *_: (0,) * nd, pipeline_mode=pl.Buffered(1))


def _rms(x, g):
    ms = jnp.mean(x * x, axis=-1, keepdims=True)
    return x * lax.rsqrt(ms + NORM_EPS) * g


def _software_pipeline(n_groups, scores, softmax, values):
    assert n_groups % 2 == 0 and n_groups >= 4
    scores(0, 0)
    scores(1, 1)
    softmax(0, 0)

    def body(gg, carry):
        g = 2 * gg
        scores(g + 2, 0)
        softmax(g + 1, 1)
        values(g, 0)
        scores(g + 3, 1)
        softmax(g + 2, 0)
        values(g + 1, 1)
        return carry

    lax.fori_loop(0, n_groups // 2 - 1, body, 0)
    softmax(n_groups - 1, 1)
    values(n_groups - 2, 0)
    values(n_groups - 1, 1)


def _even_proj_kernel(x_ref, g_ref, w_ref, cos_ref, sin_ref, na_ref, perm_ref, slab_ref, *, tm):
    h = _rms(x_ref[...], g_ref[...]).astype(BF16)
    per_res = tm // RESIDUES
    tiles = MIX // LANES
    for c in (3, 4, 5, 0, 1, 2):
        a = jnp.dot(h, w_ref[:, c * MIX:(c + 1) * MIX], preferred_element_type=F32)
        if c in (3, 4):
            cos, sin = cos_ref[...], sin_ref[...]
            parts = []
            for j in range(tiles):
                aj = a[:, j * LANES:(j + 1) * LANES]
                parts.append(aj * cos + pltpu.roll(aj, LANES // 2, axis=1) * sin)
            a = jnp.concatenate(parts, axis=1)
        if c < 3:
            na_ref[:, c * MIX:(c + 1) * MIX] = a.astype(BF16)
        else:
            slab = slab_ref.at[c % 2]
            for j in range(tiles):
                slab[j] = a[:, j * LANES:(j + 1) * LANES]
            for r in range(RESIDUES):
                for j in range(tiles):
                    col = (c - 3) * MIX + j * LANES
                    rows = slab[j, pl.ds(r, per_res, stride=RESIDUES), :]
                    perm_ref[r, :, col:col + LANES] = rows.astype(BF16)


def _even_proj(x2d, g, w, cos, sin, seq):
    n = x2d.shape[0]
    b = n // seq
    tm = TOKEN_TILE
    tps = seq // tm
    per_res = tm // RESIDUES
    return pl.pallas_call(
        functools.partial(_even_proj_kernel, tm=tm),
        grid=(n // tm,),
        in_specs=[
            pl.BlockSpec((tm, D_MODEL), lambda i: (i, 0)),
            _resident((1, D_MODEL)),
            _resident((D_MODEL, 6 * MIX)),
            pl.BlockSpec((tm, LANES), lambda i: (i % tps, 0)),
            pl.BlockSpec((tm, LANES), lambda i: (i % tps, 0)),
        ],
        out_specs=[pl.BlockSpec((tm, 3 * MIX), lambda i: (i, 0)),
                   pl.BlockSpec((None, RESIDUES, per_res, 3 * MIX), lambda i: (i // tps, 0, i % tps, 0))],
        out_shape=[jax.ShapeDtypeStruct((n, 3 * MIX), BF16),
                   jax.ShapeDtypeStruct((b, RESIDUES, seq // RESIDUES, 3 * MIX), BF16)],
        scratch_shapes=[pltpu.VMEM((2, MIX // LANES, tm, LANES), F32)],
        compiler_params=_params(("parallel",)),
        name="even_proj",
    )(x2d, g, w, cos, sin)


def _na_kernel(q_ref, k_ref, v_ref, b_ref, o_ref, s0_ref, s1_ref, p0_ref, p1_ref, r0_ref, r1_ref, *, rows):
    lane = lax.broadcasted_iota(jnp.int32, (GRID_W, LANES), 1)
    is_a = lane < HEAD_DIM
    win = NA_WIN_R * GRID_W
    s_bufs, p_bufs, r_bufs = (s0_ref, s1_ref), (p0_ref, p1_ref), (r0_ref, r1_ref)

    def window_start(r):
        rs = jnp.clip(r - NA_WIN_R // 2, 0, rows - NA_WIN_R)
        return pl.multiple_of(rs * GRID_W, GRID_W)

    def scores(g, slot):
        for u in range(ATTN_GROUP):
            r = jnp.int32(g * ATTN_GROUP + u)
            pat = jnp.where(r < NA_WIN_R // 2, r,
                            jnp.where(r > rows - NA_WIN_R // 2, r - (rows - NA_WIN_R), NA_WIN_R // 2))
            q = q_ref[pl.ds(pl.multiple_of(r * GRID_W, GRID_W), GRID_W), :]
            zero = jnp.zeros_like(q)
            q2 = jnp.concatenate([jnp.where(is_a, q, zero), jnp.where(is_a, zero, q)], axis=0)
            kw = k_ref[pl.ds(window_start(r), win), :]
            s = lax.dot_general(q2, kw, NT_DIMS, preferred_element_type=F32)
            s_bufs[slot][u] = s + b_ref[pat]

    def softmax(g, slot):
        for u in range(ATTN_GROUP):
            s = s_bufs[slot][u]
            m = jnp.max(s, axis=1, keepdims=True)
            e = jnp.exp2(s - m)
            l = jnp.sum(e, axis=1, keepdims=True)
            p_bufs[slot][u] = e.astype(BF16)
            r_bufs[slot][u] = jnp.broadcast_to(1.0 / l, (2 * GRID_W, LANES))

    def values(g, slot):
        for u in range(ATTN_GROUP):
            r = jnp.int32(g * ATTN_GROUP + u)
            vw = v_ref[pl.ds(window_start(r), win), :]
            pv = jnp.dot(p_bufs[slot][u], vw, preferred_element_type=F32) * r_bufs[slot][u]
            o_ref[pl.ds(pl.multiple_of(r * GRID_W, GRID_W), GRID_W), :] = (
                jnp.where(is_a, pv[:GRID_W], pv[GRID_W:]).astype(BF16))

    _software_pipeline(rows // ATTN_GROUP, scores, softmax, values)


def _na_attention(qkv, bias, seq):
    b = qkv.shape[0]
    rows = seq // GRID_W
    npair = NA_HEADS // 2
    win = NA_WIN_R * GRID_W
    blk = lambda off: pl.BlockSpec((None, seq, LANES), lambda p, i: (i, 0, off + p))
    s_buf = pltpu.VMEM((ATTN_GROUP, 2 * GRID_W, win), F32)
    p_buf = pltpu.VMEM((ATTN_GROUP, 2 * GRID_W, win), BF16)
    r_buf = pltpu.VMEM((ATTN_GROUP, 2 * GRID_W, LANES), F32)
    return pl.pallas_call(
        functools.partial(_na_kernel, rows=rows),
        grid=(npair, b),
        in_specs=[blk(0), blk(npair), blk(2 * npair),
                  pl.BlockSpec((NA_WIN_R, None, 2 * GRID_W, win), lambda p, i: (0, p, 0, 0))],
        out_specs=pl.BlockSpec((None, seq, LANES), lambda p, i: (i, 0, p)),
        out_shape=jax.ShapeDtypeStruct((b, seq, MIX), BF16),
        scratch_shapes=[s_buf, s_buf, p_buf, p_buf, r_buf, r_buf],
        compiler_params=_params(("parallel", "parallel")),
        name="na_attention",
    )(qkv, qkv, qkv, bias)


def _na_bias_table(rpb, rows):
    c = jnp.arange(GRID_W)
    cs = jnp.clip(c - NA_WIN_C // 2, 0, GRID_W - NA_WIN_C)
    j = jnp.arange(GRID_W)
    valid = (j[None, :] >= cs[:, None]) & (j[None, :] < cs[:, None] + NA_WIN_C)
    k = jnp.arange(2 * NA_WIN_C - 1)
    pick = (k[None, None, :] == (j[None, :, None] - c[:, None, None] + NA_WIN_C - 1)).astype(F32)
    full = jnp.einsum("hrk,cjk->hcrj", rpb.astype(F32), pick, precision=lax.Precision.HIGHEST)
    half = NA_WIN_R // 2
    reps = list(range(half)) + [half] + list(range(rows - half + 1, rows))
    tabs = []
    for r in reps:
        rs = min(max(r - half, 0), rows - NA_WIN_R)
        first = rs - r + NA_WIN_R - 1
        t = jnp.where(valid[None, :, None, :], full[:, :, first:first + NA_WIN_R, :] * LOG2_E, NEG_INF)
        tabs.append(t.reshape(NA_HEADS // 2, 2 * GRID_W, NA_WIN_R * GRID_W))
    return jnp.stack(tabs, axis=0)


def _dil_kernel(q_ref, k_ref, v_ref, o_ref, qf_ref, kf_ref, vf_ref, acc_ref, m_ref, l_ref, nat_ref, mask_ref,
                s0_ref, s1_ref, p0_ref, p1_ref, m0_ref, m1_ref, l0_ref, l1_ref, *, slab_len):
    qb, kw_rows = DIL_QBLOCK, 2 * DIL_QBLOCK
    s_bufs, p_bufs = (s0_ref, s1_ref), (p0_ref, p1_ref)
    m_bufs, l_bufs = (m0_ref, m1_ref), (l0_ref, l1_ref)
    lane = lax.broadcasted_iota(jnp.int32, (qb, LANES), 1)
    is_a_out = lane < HEAD_DIM
    is_a_rot = (lane & (HEAD_DIM - 1)) < HEAD_DIM // 2
    row = lax.broadcasted_iota(jnp.int32, (2 * qb, kw_rows), 0) & (qb - 1)
    col = lax.broadcasted_iota(jnp.int32, (2 * qb, kw_rows), 1)

    qf_ref[...] = q_ref[...].astype(F32)
    kf_ref[...] = k_ref[...].astype(F32)
    vf_ref[...] = v_ref[...].astype(F32)

    branches = sorted(DIL_BRANCHES, key=lambda wd: -wd[1])
    for bi, (window, d) in enumerate(branches):
        g_cnt = RESIDUES // d
        rq, rk = qb // g_cnt, kw_rows // g_cnt
        base = g_cnt * ((col % rk) - (row % rq)) + (col // rk) - (row // rq)
        for pat, off in enumerate((0, (rq - rk) // 2, rq - rk)):
            on_band = jnp.abs(base + g_cnt * off) <= (window // 2) // d
            mask_ref[3 * bi + pat] = jnp.where(on_band, jnp.inf, NEG_INF)

    for bi, (window, d) in enumerate(branches):
        g_cnt = RESIDUES // d
        rq, rk = qb // g_cnt, kw_rows // g_cnt
        lead = (rk - rq) // 2
        nblk = slab_len // rq

        def tile(g, u, d=d, g_cnt=g_cnt, rq=rq, rk=rk, lead=lead, nblk=nblk):
            it = jnp.int32(g * ATTN_GROUP + u)
            i = it % nblk
            q0 = i * rq
            ws = jnp.clip(q0 - lead, 0, slab_len - rk)
            pat = jnp.where(i == 0, 0, jnp.where(i == nblk - 1, 2, 1))
            slabs = [it // nblk + d * g_ for g_ in range(g_cnt)]
            return slabs, q0, pl.multiple_of(q0, rq), ws, pat

        def window(ref, ref32, slabs, ws, rq=rq, rk=rk):
            if rq % 16 == 0:
                wsa = pl.multiple_of(ws, 16)
                return jnp.concatenate([ref[s_, pl.ds(wsa, rk), :] for s_ in slabs], axis=0)
            return jnp.concatenate([ref32[s_, pl.ds(ws, rk), :] for s_ in slabs], axis=0).astype(BF16)

        def scores(g, slot, bi=bi, rq=rq, tile=tile, window=window):
            for u in range(ATTN_GROUP):
                slabs, _, q0a, ws, pat = tile(g, u)
                if rq % 16 == 0:
                    q = jnp.concatenate([q_ref[s_, pl.ds(q0a, rq), :] for s_ in slabs], axis=0)
                else:
                    q = jnp.concatenate([qf_ref[s_, pl.ds(q0a, rq), :] for s_ in slabs], axis=0).astype(BF16)
                kw = window(k_ref, kf_ref, slabs, ws)
                zero = jnp.zeros_like(q)
                q2 = jnp.concatenate([jnp.where(is_a_rot, q, zero), jnp.where(is_a_rot, zero, q)], axis=0)
                s = lax.dot_general(q2, kw, NT_DIMS, preferred_element_type=F32)
                s_bufs[slot][u] = jnp.minimum(s, mask_ref[3 * bi + pat])

        def softmax(g, slot):
            for u in range(ATTN_GROUP):
                s = s_bufs[slot][u]
                m = jnp.max(s, axis=1, keepdims=True)
                e = jnp.exp2(s - m)
                l = jnp.sum(e, axis=1, keepdims=True)
                p_bufs[slot][u] = e.astype(BF16)
                m_bufs[slot][u] = jnp.where(is_a_out, m[:qb], m[qb:])
                l_bufs[slot][u] = jnp.where(is_a_out, l[:qb], l[qb:])

        def values(g, slot, bi=bi, rq=rq, tile=tile, window=window):
            for u in range(ATTN_GROUP):
                slabs, q0, q0a, ws, _ = tile(g, u)
                vw = window(v_ref, vf_ref, slabs, ws)
                pv = jnp.dot(p_bufs[slot][u], vw, preferred_element_type=F32)
                pv_t = jnp.where(is_a_out, pv[:qb], pv[qb:])
                m_t = m_bufs[slot][u]
                l_t = l_bufs[slot][u]
                if bi > 0:
                    m_old = jnp.concatenate([m_ref[s_, pl.ds(q0a, rq), :] for s_ in slabs], axis=0)
                    l_old = jnp.concatenate([l_ref[s_, pl.ds(q0a, rq), :] for s_ in slabs], axis=0)
                    a_old = jnp.concatenate([acc_ref[s_, pl.ds(q0a, rq), :] for s_ in slabs], axis=0)
                    m_new = jnp.maximum(m_old, m_t)
                    w_old = jnp.exp2(m_old - m_new)
                    w_new = jnp.exp2(m_t - m_new)
                    l_t = l_old * w_old + l_t * w_new
                    pv_t = a_old * w_old + pv_t * w_new
                    m_t = m_new
                if bi < len(branches) - 1:
                    for g_, s_ in enumerate(slabs):
                        rows = slice(g_ * rq, (g_ + 1) * rq)
                        m_ref[s_, pl.ds(q0a, rq), :] = m_t[rows]
                        l_ref[s_, pl.ds(q0a, rq), :] = l_t[rows]
                        acc_ref[s_, pl.ds(q0a, rq), :] = pv_t[rows]
                else:
                    out = pv_t / l_t
                    t0 = q0 * RESIDUES
                    for g_, s_ in enumerate(slabs):
                        nat_ref[pl.ds(t0 + s_, rq, stride=RESIDUES), :] = out[g_ * rq:(g_ + 1) * rq]
                    t0a = pl.multiple_of(t0, qb)
                    o_ref[pl.ds(t0a, qb), :] = nat_ref[pl.ds(t0a, qb), :].astype(BF16)

        _software_pipeline(d * nblk // ATTN_GROUP, scores, softmax, values)


def _dil_attention(perm, seq):
    b = perm.shape[0]
    slab_len = seq // RESIDUES
    npair = DIL_HEADS // 2
    blk = lambda off: pl.BlockSpec((None, RESIDUES, slab_len, LANES), lambda i, p: (i, 0, 0, off + p))
    slab = pltpu.VMEM((RESIDUES, slab_len, LANES), F32)
    s_buf = pltpu.VMEM((ATTN_GROUP, 2 * DIL_QBLOCK, 2 * DIL_QBLOCK), F32)
    p_buf = pltpu.VMEM((ATTN_GROUP, 2 * DIL_QBLOCK, 2 * DIL_QBLOCK), BF16)
    stat = pltpu.VMEM((ATTN_GROUP, DIL_QBLOCK, LANES), F32)
    return pl.pallas_call(
        functools.partial(_dil_kernel, slab_len=slab_len),
        grid=(b, npair),
        in_specs=[blk(0), blk(npair), blk(2 * npair)],
        out_specs=pl.BlockSpec((None, seq, LANES), lambda i, p: (i, 0, p)),
        out_shape=jax.ShapeDtypeStruct((b, seq, MIX), BF16),
        scratch_shapes=[slab, slab, slab, slab, slab, slab,
                        pltpu.VMEM((seq, LANES), F32),
                        pltpu.VMEM((3 * len(DIL_BRANCHES), 2 * DIL_QBLOCK, 2 * DIL_QBLOCK), F32),
                        s_buf, s_buf, p_buf, p_buf, stat, stat, stat, stat],
        compiler_params=_params(("parallel", "parallel")),
        name="dilated_attention",
    )(perm, perm, perm)


def _even_out_kernel(x_ref, oa_ref, ob_ref, w_ref, y_ref):
    o = jnp.concatenate([oa_ref[...], ob_ref[...]], axis=1)
    y_ref[...] = x_ref[...] + jnp.dot(o, w_ref[...], preferred_element_type=F32)


def _even_out(x2d, oa, ob, w):
    n = x2d.shape[0]
    tm = TOKEN_TILE
    row = lambda width: pl.BlockSpec((tm, width), lambda i: (i, 0))
    return pl.pallas_call(
        _even_out_kernel,
        grid=(n // tm,),
        in_specs=[row(D_MODEL), row(MIX), row(MIX), _resident((2 * MIX, D_MODEL))],
        out_specs=row(D_MODEL),
        out_shape=jax.ShapeDtypeStruct((n, D_MODEL), F32),
        compiler_params=_params(("parallel",)),
        name="even_out",
    )(x2d, oa, ob, w)


def _ffn_kernel(x_ref, xn_ref, xp_ref, g_ref, wu_ref, cp_ref, wd_ref, fg_ref, o_ref, h_ref, a_ref, act_ref, *,
                tm, tiles_per_seq, final):
    pos = pl.program_id(0) % tiles_per_seq
    g = g_ref[...]
    x = x_ref[...]
    keep_prev = jnp.where(pos == 0, 0.0, 1.0)
    keep_next = jnp.where(pos == tiles_per_seq - 1, 0.0, 1.0)
    h_ref[0:tm, :] = _rms(x, g).astype(BF16)
    halo = jnp.concatenate([_rms(xn_ref[...], g) * keep_next, _rms(xp_ref[...], g) * keep_prev], axis=0)
    h_ref[tm:tm + 2 * HALO, :] = halo.astype(BF16)
    width = 2 * FFN_CHUNK
    tiles = width // LANES
    for c in range(N_FFN_CHUNKS):
        a = jnp.dot(h_ref[...], wu_ref[:, c * width:(c + 1) * width], preferred_element_type=F32)
        buf = a_ref.at[c % 2]
        ys = []
        for j in range(tiles):
            cols = slice(j * LANES, (j + 1) * LANES)
            buf[j, HALO:tm + 2 * HALO, :] = a[:tm + HALO, cols]
            buf[j, 0:HALO, :] = a[tm + HALO:, cols]
        for j in range(tiles):
            cols = slice(c * width + j * LANES, c * width + (j + 1) * LANES)
            y = cp_ref[3:4, cols] + buf[j, HALO - 1:HALO - 1 + tm, :] * cp_ref[0:1, cols]
            y = y + buf[j, HALO:HALO + tm, :] * cp_ref[1:2, cols]
            y = y + buf[j, HALO + 1:HALO + 1 + tm, :] * cp_ref[2:3, cols]
            ys.append(y)
        half_u = jnp.concatenate(ys[:tiles // 2], axis=1)
        gate = jnp.concatenate(ys[tiles // 2:], axis=1)
        t = jnp.tanh(gate * (GELU_C0 + GELU_C1 * (gate * gate)))
        w = half_u * gate
        act_ref[:, c * FFN_CHUNK:(c + 1) * FFN_CHUNK] = (w + w * t).astype(BF16)
    out = x + jnp.dot(act_ref[...], wd_ref[...], preferred_element_type=F32)
    if final:
        out = _rms(out, fg_ref[...])
    o_ref[...] = out


def _ffn(x2d, g, wu, cp, wd, fg, seq, final):
    n = x2d.shape[0]
    tm = FFN_TILE
    tps = seq // tm
    hb = tm // HALO
    last = n // HALO - 1
    return pl.pallas_call(
        functools.partial(_ffn_kernel, tm=tm, tiles_per_seq=tps, final=final),
        grid=(n // tm,),
        in_specs=[
            pl.BlockSpec((tm, D_MODEL), lambda i: (i, 0)),
            pl.BlockSpec((HALO, D_MODEL), lambda i: (jnp.minimum((i + 1) * hb, last), 0)),
            pl.BlockSpec((HALO, D_MODEL), lambda i: (jnp.maximum(i * hb - 1, 0), 0)),
            _resident((1, D_MODEL)),
            _resident((D_MODEL, 2 * FFN_DIM)),
            _resident((SUBLANES, 2 * FFN_DIM)),
            _resident((FFN_DIM, D_MODEL)),
            _resident((1, D_MODEL)),
        ],
        out_specs=pl.BlockSpec((tm, D_MODEL), lambda i: (i, 0)),
        out_shape=jax.ShapeDtypeStruct((n, D_MODEL), F32),
        scratch_shapes=[pltpu.VMEM((tm + 2 * HALO, D_MODEL), BF16),
                        pltpu.VMEM((2, 2 * FFN_CHUNK // LANES, tm + 2 * HALO, LANES), F32),
                        pltpu.VMEM((tm, FFN_DIM), BF16)],
        compiler_params=_params(("parallel",)),
        name="conv_ffn",
    )(x2d, x2d, x2d, g, wu, cp, wd, fg)


def _ret_proj_kernel(x_ref, g_ref, w_ref, cos_ref, sin_ref, o_ref):
    h = _rms(x_ref[...], g_ref[...]).astype(BF16)
    for c in range(RET_IN // MIX):
        a = jnp.dot(h, w_ref[:, c * MIX:(c + 1) * MIX], preferred_element_type=F32)
        if c < 2 * RET_Q // MIX:
            cos, sin = cos_ref[...], sin_ref[...]
            parts = []
            for j in range(MIX // RET_QK_DIM):
                x1 = a[:, j * RET_QK_DIM:j * RET_QK_DIM + LANES]
                x2 = a[:, j * RET_QK_DIM + LANES:(j + 1) * RET_QK_DIM]
                parts += [x1 * cos - x2 * sin, x1 * sin + x2 * cos]
            a = jnp.concatenate(parts, axis=1)
        o_ref[:, c * MIX:(c + 1) * MIX] = a.astype(BF16)


def _ret_proj(x2d, g, w, cos, sin, seq):
    n = x2d.shape[0]
    tm = TOKEN_TILE
    tps = seq // tm
    return pl.pallas_call(
        _ret_proj_kernel,
        grid=(n // tm,),
        in_specs=[
            pl.BlockSpec((tm, D_MODEL), lambda i: (i, 0)),
            _resident((1, D_MODEL)),
            _resident((D_MODEL, RET_IN)),
            pl.BlockSpec((tm, LANES), lambda i: (i % tps, 0)),
            pl.BlockSpec((tm, LANES), lambda i: (i % tps, 0)),
        ],
        out_specs=pl.BlockSpec((tm, RET_IN), lambda i: (i, 0)),
        out_shape=jax.ShapeDtypeStruct((n, RET_IN), BF16),
        compiler_params=_params(("parallel",)),
        name="ret_proj",
    )(x2d, g, w, cos, sin)


def _ret_kernel(lg_ref, q_ref, k_ref, v_ref, out_ref, o_ref, sf_ref, sb_ref, d_ref, *, seq):
    c = RET_BLOCK
    n = seq // c
    hd = pl.program_id(1)
    lgf = lg_ref[0, hd]
    lgb = lg_ref[1, hd]
    pr = lax.broadcasted_iota(jnp.int32, (c, c), 0).astype(F32)
    pc = lax.broadcasted_iota(jnp.int32, (c, c), 1).astype(F32)
    diff = pr - pc
    lower = diff >= 0
    d_f = jnp.where(lower, jnp.exp(lgf * jnp.where(lower, diff, 0.0)), 0.0)
    d_b = jnp.where(lower, 0.0, jnp.exp(lgb * jnp.where(lower, 0.0, -diff)))
    d_ref[...] = d_f + d_b
    pos = lax.broadcasted_iota(jnp.int32, (c, 1), 0).astype(F32)
    qd_f = jnp.exp(lgf * (pos + 1.0))
    kd_f = jnp.exp(lgf * (c - 1.0 - pos))
    qd_b = jnp.exp(lgb * (c - pos))
    kd_b = jnp.exp(lgb * pos)
    ones = jnp.ones((1, LANES), F32)
    sd_f = jnp.exp(lgf * c * ones)[:, :1]
    sd_b = jnp.exp(lgb * c * ones)[:, :1]

    def emit(r0, o, accumulate):
        if not accumulate:
            o_ref[pl.ds(r0, c), :] = o
            return
        out_ref[pl.ds(r0, c), :] = (o_ref[pl.ds(r0, c), :] + o).astype(BF16)

    def fwd_part(i, accumulate):
        r0 = pl.multiple_of(i * c, c)
        q = q_ref[pl.ds(r0, c), :]
        k = k_ref[pl.ds(r0, c), :]
        v = v_ref[pl.ds(r0, c), :]
        s = lax.dot_general(q, k, NT_DIMS, preferred_element_type=F32)
        inner = (s * d_ref[...]).astype(BF16)
        qd = (q.astype(F32) * qd_f).astype(BF16)
        o = (jnp.dot(inner, v, preferred_element_type=F32)
             + jnp.dot(qd, sf_ref[...].astype(BF16), preferred_element_type=F32))
        emit(r0, o, accumulate)
        kd = (k.astype(F32) * kd_f).astype(BF16)
        sf_ref[...] = sf_ref[...] * sd_f + lax.dot_general(kd, v, TN_DIMS, preferred_element_type=F32)

    def bwd_part(i, accumulate):
        r0 = pl.multiple_of(i * c, c)
        q = q_ref[pl.ds(r0, c), :]
        k = k_ref[pl.ds(r0, c), :]
        v = v_ref[pl.ds(r0, c), :]
        qd = (q.astype(F32) * qd_b).astype(BF16)
        o = jnp.dot(qd, sb_ref[...].astype(BF16), preferred_element_type=F32)
        emit(r0, o, accumulate)
        kd = (k.astype(F32) * kd_b).astype(BF16)
        sb_ref[...] = sb_ref[...] * sd_b + lax.dot_general(kd, v, TN_DIMS, preferred_element_type=F32)

    sf_ref[...] = jnp.zeros_like(sf_ref)
    sb_ref[...] = jnp.zeros_like(sb_ref)

    def first_half(j, carry):
        fwd_part(j, False)
        bwd_part(n - 1 - j, False)
        return carry

    def second_half(j, carry):
        fwd_part(j, True)
        bwd_part(n - 1 - j, True)
        return carry

    lax.fori_loop(0, n // 2, first_half, 0)
    lax.fori_loop(n // 2, n, second_half, 0)


def _retention(qkv, log_gamma, seq):
    b = qkv.shape[0]
    kblk = RET_Q // RET_QK_DIM
    vblk = 2 * RET_Q // RET_V_DIM
    state = pltpu.VMEM((RET_QK_DIM, RET_V_DIM), F32)
    return pl.pallas_call(
        functools.partial(_ret_kernel, seq=seq),
        grid=(b, RET_HEADS),
        in_specs=[
            pl.BlockSpec(memory_space=pltpu.SMEM),
            pl.BlockSpec((None, seq, RET_QK_DIM), lambda i, h: (i, 0, h)),
            pl.BlockSpec((None, seq, RET_QK_DIM), lambda i, h: (i, 0, kblk + h)),
            pl.BlockSpec((None, seq, RET_V_DIM), lambda i, h: (i, 0, vblk + h)),
        ],
        out_specs=pl.BlockSpec((None, seq, RET_V_DIM), lambda i, h: (i, 0, h)),
        out_shape=jax.ShapeDtypeStruct((b, seq, RET_V), BF16),
        scratch_shapes=[pltpu.VMEM((seq, RET_V_DIM), F32), state, state,
                        pltpu.VMEM((RET_BLOCK, RET_BLOCK), F32)],
        compiler_params=_params(("parallel", "parallel")),
        name="retention",
    )(log_gamma, qkv, qkv, qkv)


def _ret_out_kernel(x_ref, r_ref, g_ref, w_ref, y_ref):
    parts = []
    for hd in range(RET_HEADS):
        cols = slice(hd * RET_V_DIM, (hd + 1) * RET_V_DIM)
        r = r_ref[:, cols].astype(F32)
        mu = jnp.mean(r, axis=-1, keepdims=True)
        var = jnp.mean(jnp.square(r - mu), axis=-1, keepdims=True)
        rn = (r - mu) * lax.rsqrt(var + NORM_EPS)
        half_gate = g_ref[:, cols].astype(F32) * 0.5
        parts.append(((half_gate + half_gate * jnp.tanh(half_gate)) * rn).astype(BF16))
    act = jnp.concatenate(parts, axis=1)
    y_ref[...] = x_ref[...] + jnp.dot(act, w_ref[...], preferred_element_type=F32)


def _ret_out(x2d, r, proj, w):
    n = x2d.shape[0]
    tm = TOKEN_TILE
    row = lambda width: pl.BlockSpec((tm, width), lambda i: (i, 0))
    gate_blk = (2 * RET_Q + RET_V) // RET_V
    return pl.pallas_call(
        _ret_out_kernel,
        grid=(n // tm,),
        in_specs=[row(D_MODEL), row(RET_V), pl.BlockSpec((tm, RET_V), lambda i: (i, gate_blk)),
                  _resident((RET_V, D_MODEL))],
        out_specs=row(D_MODEL),
        out_shape=jax.ShapeDtypeStruct((n, D_MODEL), F32),
        compiler_params=_params(("parallel",)),
        name="ret_out",
    )(x2d, r, proj, w)


def _rope_tables(seq, dh, lane_freq, sign):
    inv = 1.0 / (ROPE_THETA ** (jnp.arange(0, dh, 2, dtype=F32) / dh))
    ang = jnp.arange(seq, dtype=F32)[:, None] * inv[None, :]
    return jnp.cos(ang)[:, lane_freq], jnp.sin(ang)[:, lane_freq] * sign[None, :]


def _pair_rotary_layout(w):
    rows = w.shape[0]
    w = w.reshape(rows, DIL_HEADS // 2, 2, 2, HEAD_DIM // 2)
    return w.transpose(0, 1, 3, 2, 4).reshape(rows, MIX)


def _ffn_chunk_layout(w):
    rows = w.shape[0]
    w = w.reshape(rows, 2, N_FFN_CHUNKS, FFN_CHUNK)
    return w.transpose(0, 2, 1, 3).reshape(rows, 2 * FFN_DIM)


def _prepare(attn_norm, even_w_in, na_rpb, even_w_out, ret_w_in, ret_decay_fwd, ret_decay_bwd, ret_w_out,
             ffn_norm, ffn_w_up, ffn_conv_w, ffn_conv_b, ffn_w_down, final_norm, seq):
    lane = jnp.arange(LANES)
    half = HEAD_DIM // 2
    w_in = even_w_in[0]
    blocks = [w_in[:, i * MIX:(i + 1) * MIX] for i in range(6)]
    scale = HEAD_DIM ** -0.5 * LOG2_E
    blocks[0] = blocks[0] * scale
    blocks[3] = _pair_rotary_layout(blocks[3]) * scale
    blocks[4] = _pair_rotary_layout(blocks[4])
    p = {}
    p["even_w_in"] = jnp.concatenate(blocks, axis=1).astype(BF16)
    p["even_cos"], p["even_sin"] = _rope_tables(seq, HEAD_DIM, lane % half,
                                                jnp.where(lane < LANES // 2, -1.0, 1.0))
    p["na_bias"] = _na_bias_table(na_rpb[0], seq // GRID_W)
    p["even_w_out"] = even_w_out[0].astype(BF16)

    rw = ret_w_in[0]
    rscale = RET_QK_DIM ** -0.5
    p["ret_w_in"] = jnp.concatenate([rw[:, :RET_Q] * rscale, rw[:, RET_Q:]], axis=1).astype(BF16)
    p["ret_cos"], p["ret_sin"] = _rope_tables(seq, RET_QK_DIM, lane, jnp.ones((LANES,), F32))
    p["ret_log_gamma"] = jnp.stack([-jax.nn.softplus(ret_decay_fwd[0].astype(F32)),
                                    -jax.nn.softplus(ret_decay_bwd[0].astype(F32))], axis=0)
    p["ret_w_out"] = ret_w_out[0].astype(BF16)

    u_half = jnp.concatenate([jnp.full((FFN_DIM,), 0.5, F32), jnp.ones((FFN_DIM,), F32)])
    p["ffn_w_up"], p["ffn_conv"] = [], []
    for layer in range(2):
        p["ffn_w_up"].append(_ffn_chunk_layout(ffn_w_up[layer]).astype(BF16))
        taps = jnp.concatenate([ffn_conv_w[layer], ffn_conv_b[layer][None, :]], axis=0) * u_half[None, :]
        taps = jnp.concatenate([taps, jnp.zeros((SUBLANES - CONV_WIDTH - 1, 2 * FFN_DIM), F32)], axis=0)
        p["ffn_conv"].append(_ffn_chunk_layout(taps))
    p["ffn_w_down"] = [ffn_w_down[layer].astype(BF16) for layer in range(2)]
    p["attn_norm"] = [attn_norm[layer][None, :] for layer in range(2)]
    p["ffn_norm"] = [ffn_norm[layer][None, :] for layer in range(2)]
    p["final_norm"] = final_norm[None, :]
    return p


def _trunk(x, p):
    b, seq, _ = x.shape
    x2d = x.reshape(b * seq, D_MODEL)
    na_qkv, dil_qkv = _even_proj(x2d, p["attn_norm"][0], p["even_w_in"], p["even_cos"], p["even_sin"], seq)
    oa = _na_attention(na_qkv.reshape(b, seq, 3 * MIX), p["na_bias"], seq).reshape(b * seq, MIX)
    ob = _dil_attention(dil_qkv, seq).reshape(b * seq, MIX)
    x2d = _even_out(x2d, oa, ob, p["even_w_out"])
    x2d = _ffn(x2d, p["ffn_norm"][0], p["ffn_w_up"][0], p["ffn_conv"][0], p["ffn_w_down"][0],
               p["final_norm"], seq, final=False)
    rproj = _ret_proj(x2d, p["attn_norm"][1], p["ret_w_in"], p["ret_cos"], p["ret_sin"], seq)
    r = _retention(rproj.reshape(b, seq, RET_IN), p["ret_log_gamma"], seq)
    x2d = _ret_out(x2d, r.reshape(b * seq, RET_V), rproj, p["ret_w_out"])
    x2d = _ffn(x2d, p["ffn_norm"][1], p["ffn_w_up"][1], p["ffn_conv"][1], p["ffn_w_down"][1],
               p["final_norm"], seq, final=True)
    return x2d.reshape(b, seq, D_MODEL)


def kernel(x_prompt, x_sample, attn_norm, even_w_in, na_rpb, even_w_out, ret_w_in, ret_decay_fwd, ret_decay_bwd,
           ret_w_out, ffn_norm, ffn_w_up, ffn_conv_w, ffn_conv_b, ffn_w_down, final_norm):
    assert x_prompt.shape[1] == x_sample.shape[1]
    p = _prepare(attn_norm, even_w_in, na_rpb, even_w_out, ret_w_in, ret_decay_fwd, ret_decay_bwd, ret_w_out,
                 ffn_norm, ffn_w_up, ffn_conv_w, ffn_conv_b, ffn_w_down, final_norm, x_prompt.shape[1])
    return _trunk(x_prompt, p), _trunk(x_sample, p)
```

```python
import functools

import jax
import jax.numpy as jnp
from jax import lax
from jax.experimental import pallas as pl
from jax.experimental.pallas import tpu as pltpu

D_MODEL = 1024
GRID_W = 64
HEAD_DIM = 64
NA_HEADS = 8
NA_WIN_R = 8
NA_WIN_C = 16
DIL_HEADS = 8
DIL_BRANCHES = ((128, 1), (512, 4), (2048, 16))
DIL_QBLOCK = 128
RET_HEADS = 4
RET_QK_DIM = 256
RET_V_DIM = 512
FFN_DIM = 2816
CONV_WIDTH = 3
ROPE_THETA = 10000.0
NORM_EPS = 1e-6
NEG_INF = -1e30

LANES = 128
SUBLANES = 8
MIX = NA_HEADS * HEAD_DIM
RET_Q = RET_HEADS * RET_QK_DIM
RET_V = RET_HEADS * RET_V_DIM
RET_IN = 2 * RET_Q + 2 * RET_V
RET_BLOCK = 256
FFN_CHUNK = 256
N_FFN_CHUNKS = FFN_DIM // FFN_CHUNK
TOKEN_TILE = 512
FFN_TILE = 512
ATTN_GROUP = 4
NA_GROUP = 4
HALO = SUBLANES
RESIDUES = max(d for _, d in DIL_BRANCHES)
SLAB_PITCH = RESIDUES + 4
VMEM_LIMIT = 56 * 1024 * 1024
LOG2_E = 1.4426950408889634
GELU_C0 = 0.7978845608028654
GELU_C1 = GELU_C0 * 0.044715

F32 = jnp.float32
BF16 = jnp.bfloat16
NT_DIMS = (((1,), (1,)), ((), ()))
TN_DIMS = (((0,), (0,)), ((), ()))


def _params(sem, vmem=VMEM_LIMIT):
    return pltpu.CompilerParams(dimension_semantics=sem, vmem_limit_bytes=vmem)


def _resident(shape):
    nd = len(shape)
    return pl.BlockSpec(shape, lambda *_: (0,) * nd, pipeline_mode=pl.Buffered(1))


def _rms(x, g):
    ms = jnp.mean(x * x, axis=-1, keepdims=True)
    return x * lax.rsqrt(ms + NORM_EPS) * g


def _software_pipeline(n_groups, scores, softmax, values):
    assert n_groups % 2 == 0 and n_groups >= 4
    scores(0, 0)
    scores(1, 1)
    softmax(0, 0)

    def body(gg, carry):
        g = 2 * gg
        scores(g + 2, 0)
        softmax(g + 1, 1)
        values(g, 0)
        scores(g + 3, 1)
        softmax(g + 2, 0)
        values(g + 1, 1)
        return carry

    lax.fori_loop(0, n_groups // 2 - 1, body, 0)
    softmax(n_groups - 1, 1)
    values(n_groups - 2, 0)
    values(n_groups - 1, 1)


def _even_proj_kernel(x_ref, g_ref, w_ref, cos_ref, sin_ref, na_ref, perm_ref, slab_ref, *, tm):
    h = _rms(x_ref[...], g_ref[...]).astype(BF16)
    per_res = tm // RESIDUES
    tiles = MIX // LANES
    for c in (3, 4, 5, 0, 1, 2):
        a = jnp.dot(h, w_ref[:, c * MIX:(c + 1) * MIX], preferred_element_type=F32)
        if c in (3, 4):
            cos, sin = cos_ref[...], sin_ref[...]
            parts = []
            for j in range(tiles):
                aj = a[:, j * LANES:(j + 1) * LANES]
                parts.append(aj * cos + pltpu.roll(aj, LANES // 2, axis=1) * sin)
            a = jnp.concatenate(parts, axis=1)
        if c < 3:
            na_ref[:, c * MIX:(c + 1) * MIX] = a.astype(BF16)
        else:
            slab = slab_ref.at[c % 2]
            for j in range(tiles):
                for m in range(per_res):
                    slab[j, m * SLAB_PITCH:m * SLAB_PITCH + RESIDUES, :] = (
                        a[m * RESIDUES:(m + 1) * RESIDUES, j * LANES:(j + 1) * LANES])
            for r in range(RESIDUES):
                for j in range(tiles):
                    col = (c - 3) * MIX + j * LANES
                    rows = slab[j, pl.ds(r, per_res, stride=SLAB_PITCH), :]
                    perm_ref[r, :, col:col + LANES] = rows.astype(BF16)


def _even_proj(x2d, g, w, cos, sin, seq):
    n = x2d.shape[0]
    b = n // seq
    tm = TOKEN_TILE
    tps = seq // tm
    per_res = tm // RESIDUES
    return pl.pallas_call(
        functools.partial(_even_proj_kernel, tm=tm),
        grid=(n // tm,),
        in_specs=[
            pl.BlockSpec((tm, D_MODEL), lambda i: (i, 0)),
            _resident((1, D_MODEL)),
            _resident((D_MODEL, 6 * MIX)),
            pl.BlockSpec((tm, LANES), lambda i: (i % tps, 0)),
            pl.BlockSpec((tm, LANES), lambda i: (i % tps, 0)),
        ],
        out_specs=[pl.BlockSpec((tm, 3 * MIX), lambda i: (i, 0)),
                   pl.BlockSpec((None, RESIDUES, per_res, 3 * MIX), lambda i: (i // tps, 0, i % tps, 0))],
        out_shape=[jax.ShapeDtypeStruct((n, 3 * MIX), BF16),
                   jax.ShapeDtypeStruct((b, RESIDUES, seq // RESIDUES, 3 * MIX), BF16)],
        scratch_shapes=[pltpu.VMEM((2, MIX // LANES, per_res * SLAB_PITCH, LANES), F32)],
        compiler_params=_params(("parallel",)),
        name="even_proj",
    )(x2d, g, w, cos, sin)


def _na_kernel(q_ref, k_ref, v_ref, b_ref, o_ref, s0_ref, s1_ref, p0_ref, p1_ref, r0_ref, r1_ref, *, rows):
    lane = lax.broadcasted_iota(jnp.int32, (GRID_W, LANES), 1)
    is_a = lane < HEAD_DIM
    win = NA_WIN_R * GRID_W
    s_bufs, p_bufs, r_bufs = (s0_ref, s1_ref), (p0_ref, p1_ref), (r0_ref, r1_ref)

    def window_start(r):
        rs = jnp.clip(r - NA_WIN_R // 2, 0, rows - NA_WIN_R)
        return pl.multiple_of(rs * GRID_W, GRID_W)

    def scores(g, slot):
        for u in range(NA_GROUP):
            r = jnp.int32(g * NA_GROUP + u)
            pat = jnp.where(r < NA_WIN_R // 2, r,
                            jnp.where(r > rows - NA_WIN_R // 2, r - (rows - NA_WIN_R), NA_WIN_R // 2))
            q = q_ref[pl.ds(pl.multiple_of(r * GRID_W, GRID_W), GRID_W), :]
            zero = jnp.zeros_like(q)
            q2 = jnp.concatenate([jnp.where(is_a, q, zero), jnp.where(is_a, zero, q)], axis=0)
            kw = k_ref[pl.ds(window_start(r), win), :]
            s = lax.dot_general(q2, kw, NT_DIMS, preferred_element_type=F32)
            s_bufs[slot][u] = s + b_ref[pat]

    def softmax(g, slot):
        for u in range(NA_GROUP):
            s = s_bufs[slot][u]
            m = jnp.max(s, axis=1, keepdims=True)
            e = jnp.exp2(s - m)
            l = jnp.sum(e, axis=1, keepdims=True)
            p_bufs[slot][u] = e.astype(BF16)
            r_bufs[slot][u] = jnp.broadcast_to(1.0 / l, (2 * GRID_W, LANES))

    def values(g, slot):
        for u in range(NA_GROUP):
            r = jnp.int32(g * NA_GROUP + u)
            vw = v_ref[pl.ds(window_start(r), win), :]
            pv = jnp.dot(p_bufs[slot][u], vw, preferred_element_type=F32) * r_bufs[slot][u]
            o_ref[pl.ds(pl.multiple_of(r * GRID_W, GRID_W), GRID_W), :] = (
                jnp.where(is_a, pv[:GRID_W], pv[GRID_W:]).astype(BF16))

    _software_pipeline(rows // NA_GROUP, scores, softmax, values)


def _na_attention(qkv, bias, seq):
    b = qkv.shape[0]
    rows = seq // GRID_W
    npair = NA_HEADS // 2
    win = NA_WIN_R * GRID_W
    blk = lambda off: pl.BlockSpec((None, seq, LANES), lambda p, i: (i, 0, off + p))
    s_buf = pltpu.VMEM((NA_GROUP, 2 * GRID_W, win), F32)
    p_buf = pltpu.VMEM((NA_GROUP, 2 * GRID_W, win), BF16)
    r_buf = pltpu.VMEM((NA_GROUP, 2 * GRID_W, LANES), F32)
    return pl.pallas_call(
        functools.partial(_na_kernel, rows=rows),
        grid=(npair, b),
        in_specs=[blk(0), blk(npair), blk(2 * npair),
                  pl.BlockSpec((NA_WIN_R, None, 2 * GRID_W, win), lambda p, i: (0, p, 0, 0))],
        out_specs=pl.BlockSpec((None, seq, LANES), lambda p, i: (i, 0, p)),
        out_shape=jax.ShapeDtypeStruct((b, seq, MIX), BF16),
        scratch_shapes=[s_buf, s_buf, p_buf, p_buf, r_buf, r_buf],
        compiler_params=_params(("parallel", "parallel")),
        name="na_attention",
    )(qkv, qkv, qkv, bias)


def _na_bias_table(rpb, rows):
    c = jnp.arange(GRID_W)
    cs = jnp.clip(c - NA_WIN_C // 2, 0, GRID_W - NA_WIN_C)
    j = jnp.arange(GRID_W)
    valid = (j[None, :] >= cs[:, None]) & (j[None, :] < cs[:, None] + NA_WIN_C)
    k = jnp.arange(2 * NA_WIN_C - 1)
    pick = (k[None, None, :] == (j[None, :, None] - c[:, None, None] + NA_WIN_C - 1)).astype(F32)
    full = jnp.einsum("hrk,cjk->hcrj", rpb.astype(F32), pick, precision=lax.Precision.HIGHEST)
    half = NA_WIN_R // 2
    reps = list(range(half)) + [half] + list(range(rows - half + 1, rows))
    tabs = []
    for r in reps:
        rs = min(max(r - half, 0), rows - NA_WIN_R)
        first = rs - r + NA_WIN_R - 1
        t = jnp.where(valid[None, :, None, :], full[:, :, first:first + NA_WIN_R, :] * LOG2_E, NEG_INF)
        tabs.append(t.reshape(NA_HEADS // 2, 2 * GRID_W, NA_WIN_R * GRID_W))
    return jnp.stack(tabs, axis=0)


def _dil_kernel(q_ref, k_ref, v_ref, o_ref, qf_ref, kf_ref, vf_ref, acc_ref, m_ref, l_ref, nat_ref, mask_ref,
                s0_ref, s1_ref, p0_ref, p1_ref, m0_ref, m1_ref, l0_ref, l1_ref, *, slab_len):
    qb, kw_rows = DIL_QBLOCK, 2 * DIL_QBLOCK
    s_bufs, p_bufs = (s0_ref, s1_ref), (p0_ref, p1_ref)
    m_bufs, l_bufs = (m0_ref, m1_ref), (l0_ref, l1_ref)
    lane = lax.broadcasted_iota(jnp.int32, (qb, LANES), 1)
    is_a_out = lane < HEAD_DIM
    is_a_rot = (lane & (HEAD_DIM - 1)) < HEAD_DIM // 2
    row = lax.broadcasted_iota(jnp.int32, (2 * qb, kw_rows), 0) & (qb - 1)
    col = lax.broadcasted_iota(jnp.int32, (2 * qb, kw_rows), 1)

    qf_ref[...] = q_ref[...].astype(F32)
    kf_ref[...] = k_ref[...].astype(F32)
    vf_ref[...] = v_ref[...].astype(F32)

    branches = sorted(DIL_BRANCHES, key=lambda wd: -wd[1])
    for bi, (window, d) in enumerate(branches):
        g_cnt = RESIDUES // d
        rq, rk = qb // g_cnt, kw_rows // g_cnt
        base = g_cnt * ((col % rk) - (row % rq)) + (col // rk) - (row // rq)
        for pat, off in enumerate((0, (rq - rk) // 2, rq - rk)):
            on_band = jnp.abs(base + g_cnt * off) <= (window // 2) // d
            mask_ref[3 * bi + pat] = jnp.where(on_band, jnp.inf, NEG_INF)

    for bi, (window, d) in enumerate(branches):
        g_cnt = RESIDUES // d
        rq, rk = qb // g_cnt, kw_rows // g_cnt
        lead = (rk - rq) // 2
        nblk = slab_len // rq

        def tile(g, u, d=d, g_cnt=g_cnt, rq=rq, rk=rk, lead=lead, nblk=nblk):
            it = jnp.int32(g * ATTN_GROUP + u)
            i = it % nblk
            q0 = i * rq
            ws = jnp.clip(q0 - lead, 0, slab_len - rk)
            pat = jnp.where(i == 0, 0, jnp.where(i == nblk - 1, 2, 1))
            slabs = [it // nblk + d * g_ for g_ in range(g_cnt)]
            return slabs, q0, pl.multiple_of(q0, rq), ws, pat

        def window(ref, ref32, slabs, ws, rq=rq, rk=rk):
            if rq % 16 == 0:
                wsa = pl.multiple_of(ws, 16)
                return jnp.concatenate([ref[s_, pl.ds(wsa, rk), :] for s_ in slabs], axis=0)
            return jnp.concatenate([ref32[s_, pl.ds(ws, rk), :] for s_ in slabs], axis=0).astype(BF16)

        def scores(g, slot, bi=bi, rq=rq, tile=tile, window=window):
            for u in range(ATTN_GROUP):
                slabs, _, q0a, ws, pat = tile(g, u)
                if rq % 16 == 0:
                    q = jnp.concatenate([q_ref[s_, pl.ds(q0a, rq), :] for s_ in slabs], axis=0)
                else:
                    q = jnp.concatenate([qf_ref[s_, pl.ds(q0a, rq), :] for s_ in slabs], axis=0).astype(BF16)
                kw = window(k_ref, kf_ref, slabs, ws)
                zero = jnp.zeros_like(q)
                q2 = jnp.concatenate([jnp.where(is_a_rot, q, zero), jnp.where(is_a_rot, zero, q)], axis=0)
                s = lax.dot_general(q2, kw, NT_DIMS, preferred_element_type=F32)
                s_bufs[slot][u] = jnp.minimum(s, mask_ref[3 * bi + pat])

        def softmax(g, slot):
            for u in range(ATTN_GROUP):
                s = s_bufs[slot][u]
                m = jnp.max(s, axis=1, keepdims=True)
                e = jnp.exp2(s - m)
                l = jnp.sum(e, axis=1, keepdims=True)
                p_bufs[slot][u] = e.astype(BF16)
                m_bufs[slot][u] = jnp.where(is_a_out, m[:qb], m[qb:])
                l_bufs[slot][u] = jnp.where(is_a_out, l[:qb], l[qb:])

        def values(g, slot, bi=bi, rq=rq, tile=tile, window=window):
            for u in range(ATTN_GROUP):
                slabs, q0, q0a, ws, _ = tile(g, u)
                vw = window(v_ref, vf_ref, slabs, ws)
                pv = jnp.dot(p_bufs[slot][u], vw, preferred_element_type=F32)
                pv_t = jnp.where(is_a_out, pv[:qb], pv[qb:])
                m_t = m_bufs[slot][u]
                l_t = l_bufs[slot][u]
                if bi > 0:
                    m_old = jnp.concatenate([m_ref[s_, pl.ds(q0a, rq), :] for s_ in slabs], axis=0)
                    l_old = jnp.concatenate([l_ref[s_, pl.ds(q0a, rq), :] for s_ in slabs], axis=0)
                    a_old = jnp.concatenate([acc_ref[s_, pl.ds(q0a, rq), :] for s_ in slabs], axis=0)
                    m_new = jnp.maximum(m_old, m_t)
                    w_old = jnp.exp2(m_old - m_new)
                    w_new = jnp.exp2(m_t - m_new)
                    l_t = l_old * w_old + l_t * w_new
                    pv_t = a_old * w_old + pv_t * w_new
                    m_t = m_new
                if bi < len(branches) - 1:
                    for g_, s_ in enumerate(slabs):
                        rows = slice(g_ * rq, (g_ + 1) * rq)
                        m_ref[s_, pl.ds(q0a, rq), :] = m_t[rows]
                        l_ref[s_, pl.ds(q0a, rq), :] = l_t[rows]
                        acc_ref[s_, pl.ds(q0a, rq), :] = pv_t[rows]
                else:
                    out = pv_t / l_t
                    t0 = q0 * RESIDUES
                    for g_, s_ in enumerate(slabs):
                        nat_ref[pl.ds(t0 + s_, rq, stride=RESIDUES), :] = out[g_ * rq:(g_ + 1) * rq]
                    t0a = pl.multiple_of(t0, qb)
                    o_ref[pl.ds(t0a, qb), :] = nat_ref[pl.ds(t0a, qb), :].astype(BF16)

        _software_pipeline(d * nblk // ATTN_GROUP, scores, softmax, values)


def _dil_attention(perm, seq):
    b = perm.shape[0]
    slab_len = seq // RESIDUES
    npair = DIL_HEADS // 2
    blk = lambda off: pl.BlockSpec((None, RESIDUES, slab_len, LANES), lambda i, p: (i, 0, 0, off + p))
    slab = pltpu.VMEM((RESIDUES, slab_len, LANES), F32)
    s_buf = pltpu.VMEM((ATTN_GROUP, 2 * DIL_QBLOCK, 2 * DIL_QBLOCK), F32)
    p_buf = pltpu.VMEM((ATTN_GROUP, 2 * DIL_QBLOCK, 2 * DIL_QBLOCK), BF16)
    stat = pltpu.VMEM((ATTN_GROUP, DIL_QBLOCK, LANES), F32)
    return pl.pallas_call(
        functools.partial(_dil_kernel, slab_len=slab_len),
        grid=(b, npair),
        in_specs=[blk(0), blk(npair), blk(2 * npair)],
        out_specs=pl.BlockSpec((None, seq, LANES), lambda i, p: (i, 0, p)),
        out_shape=jax.ShapeDtypeStruct((b, seq, MIX), BF16),
        scratch_shapes=[slab, slab, slab, slab, slab, slab,
                        pltpu.VMEM((seq, LANES), F32),
                        pltpu.VMEM((3 * len(DIL_BRANCHES), 2 * DIL_QBLOCK, 2 * DIL_QBLOCK), F32),
                        s_buf, s_buf, p_buf, p_buf, stat, stat, stat, stat],
        compiler_params=_params(("parallel", "parallel")),
        name="dilated_attention",
    )(perm, perm, perm)


def _even_out_kernel(x_ref, oa_ref, ob_ref, w_ref, y_ref):
    o = jnp.concatenate([oa_ref[...], ob_ref[...]], axis=1)
    y_ref[...] = x_ref[...] + jnp.dot(o, w_ref[...], preferred_element_type=F32)


def _even_out(x2d, oa, ob, w):
    n = x2d.shape[0]
    tm = TOKEN_TILE
    row = lambda width: pl.BlockSpec((tm, width), lambda i: (i, 0))
    return pl.pallas_call(
        _even_out_kernel,
        grid=(n // tm,),
        in_specs=[row(D_MODEL), row(MIX), row(MIX), _resident((2 * MIX, D_MODEL))],
        out_specs=row(D_MODEL),
        out_shape=jax.ShapeDtypeStruct((n, D_MODEL), F32),
        compiler_params=_params(("parallel",)),
        name="even_out",
    )(x2d, oa, ob, w)


def _ffn_kernel(x_ref, xn_ref, xp_ref, g_ref, wu_ref, cp_ref, wd_ref, fg_ref, o_ref, h_ref, a_ref, act_ref, *,
                tm, tiles_per_seq, final):
    pos = pl.program_id(0) % tiles_per_seq
    g = g_ref[...]
    keep_prev = jnp.where(pos == 0, 0.0, 1.0)
    keep_next = jnp.where(pos == tiles_per_seq - 1, 0.0, 1.0)
    h_ref[0:tm, :] = _rms(x_ref[...], g).astype(BF16)
    halo = jnp.concatenate([_rms(xn_ref[...], g) * keep_next, _rms(xp_ref[...], g) * keep_prev], axis=0)
    h_ref[tm:tm + 2 * HALO, :] = halo.astype(BF16)
    tiles = FFN_CHUNK // LANES
    for c in range(N_FFN_CHUNKS):
        buf = a_ref.at[c % 2]
        ys = []
        for part in range(2):
            first = part * FFN_DIM + c * FFN_CHUNK
            a = jnp.dot(h_ref[...], wu_ref[:, first:first + FFN_CHUNK], preferred_element_type=F32)
            for j in range(tiles):
                cols = slice(j * LANES, (j + 1) * LANES)
                buf[part * tiles + j, HALO:tm + 2 * HALO, :] = a[:tm + HALO, cols]
                buf[part * tiles + j, 0:HALO, :] = a[tm + HALO:, cols]
        for part in range(2):
            for j in range(tiles):
                slab = buf.at[part * tiles + j]
                first = part * FFN_DIM + c * FFN_CHUNK + j * LANES
                cols = slice(first, first + LANES)
                y = cp_ref[3:4, cols] + slab[HALO - 1:HALO - 1 + tm, :] * cp_ref[0:1, cols]
                y = y + slab[HALO:HALO + tm, :] * cp_ref[1:2, cols]
                y = y + slab[HALO + 1:HALO + 1 + tm, :] * cp_ref[2:3, cols]
                ys.append(y)
        half_u = jnp.concatenate(ys[:tiles], axis=1)
        gate = jnp.concatenate(ys[tiles:], axis=1)
        t = jnp.tanh(gate * (GELU_C0 + GELU_C1 * (gate * gate)))
        w = half_u * gate
        act_ref[:, c * FFN_CHUNK:(c + 1) * FFN_CHUNK] = (w + w * t).astype(BF16)
    out = x_ref[...] + jnp.dot(act_ref[...], wd_ref[...], preferred_element_type=F32)
    if final:
        out = _rms(out, fg_ref[...])
    o_ref[...] = out


def _ffn(x2d, g, wu, cp, wd, fg, seq, final):
    n = x2d.shape[0]
    tm = FFN_TILE
    tps = seq // tm
    hb = tm // HALO
    last = n // HALO - 1
    return pl.pallas_call(
        functools.partial(_ffn_kernel, tm=tm, tiles_per_seq=tps, final=final),
        grid=(n // tm,),
        in_specs=[
            pl.BlockSpec((tm, D_MODEL), lambda i: (i, 0)),
            pl.BlockSpec((HALO, D_MODEL), lambda i: (jnp.minimum((i + 1) * hb, last), 0)),
            pl.BlockSpec((HALO, D_MODEL), lambda i: (jnp.maximum(i * hb - 1, 0), 0)),
            _resident((1, D_MODEL)),
            _resident((D_MODEL, 2 * FFN_DIM)),
            _resident((SUBLANES, 2 * FFN_DIM)),
            _resident((FFN_DIM, D_MODEL)),
            _resident((1, D_MODEL)),
        ],
        out_specs=pl.BlockSpec((tm, D_MODEL), lambda i: (i, 0)),
        out_shape=jax.ShapeDtypeStruct((n, D_MODEL), F32),
        scratch_shapes=[pltpu.VMEM((tm + 2 * HALO, D_MODEL), BF16),
                        pltpu.VMEM((2, 2 * FFN_CHUNK // LANES, tm + 2 * HALO, LANES), F32),
                        pltpu.VMEM((tm, FFN_DIM), BF16)],
        compiler_params=_params(("parallel",)),
        name="conv_ffn",
    )(x2d, x2d, x2d, g, wu, cp, wd, fg)


def _ret_proj_kernel(x_ref, g_ref, w_ref, cos_ref, sin_ref, o_ref):
    h = _rms(x_ref[...], g_ref[...]).astype(BF16)
    for c in range(RET_IN // MIX):
        a = jnp.dot(h, w_ref[:, c * MIX:(c + 1) * MIX], preferred_element_type=F32)
        if c < 2 * RET_Q // MIX:
            cos, sin = cos_ref[...], sin_ref[...]
            parts = []
            for j in range(MIX // RET_QK_DIM):
                x1 = a[:, j * RET_QK_DIM:j * RET_QK_DIM + LANES]
                x2 = a[:, j * RET_QK_DIM + LANES:(j + 1) * RET_QK_DIM]
                parts += [x1 * cos - x2 * sin, x1 * sin + x2 * cos]
            a = jnp.concatenate(parts, axis=1)
        o_ref[:, c * MIX:(c + 1) * MIX] = a.astype(BF16)


def _ret_proj(x2d, g, w, cos, sin, seq):
    n = x2d.shape[0]
    tm = TOKEN_TILE
    tps = seq // tm
    return pl.pallas_call(
        _ret_proj_kernel,
        grid=(n // tm,),
        in_specs=[
            pl.BlockSpec((tm, D_MODEL), lambda i: (i, 0)),
            _resident((1, D_MODEL)),
            _resident((D_MODEL, RET_IN)),
            pl.BlockSpec((tm, LANES), lambda i: (i % tps, 0)),
            pl.BlockSpec((tm, LANES), lambda i: (i % tps, 0)),
        ],
        out_specs=pl.BlockSpec((tm, RET_IN), lambda i: (i, 0)),
        out_shape=jax.ShapeDtypeStruct((n, RET_IN), BF16),
        compiler_params=_params(("parallel",)),
        name="ret_proj",
    )(x2d, g, w, cos, sin)


def _ret_kernel(lg_ref, q_ref, k_ref, v_ref, out_ref, o_ref, sf_ref, sb_ref, d_ref, *, seq):
    c = RET_BLOCK
    n = seq // c
    hd = pl.program_id(1)
    lgf = lg_ref[0, hd]
    lgb = lg_ref[1, hd]
    pr = lax.broadcasted_iota(jnp.int32, (c, c), 0).astype(F32)
    pc = lax.broadcasted_iota(jnp.int32, (c, c), 1).astype(F32)
    diff = pr - pc
    lower = diff >= 0
    d_f = jnp.where(lower, jnp.exp(lgf * jnp.where(lower, diff, 0.0)), 0.0)
    d_b = jnp.where(lower, 0.0, jnp.exp(lgb * jnp.where(lower, 0.0, -diff)))
    d_ref[...] = d_f + d_b
    pos = lax.broadcasted_iota(jnp.int32, (c, 1), 0).astype(F32)
    qd_f = jnp.exp(lgf * (pos + 1.0))
    kd_f = jnp.exp(lgf * (c - 1.0 - pos))
    qd_b = jnp.exp(lgb * (c - pos))
    kd_b = jnp.exp(lgb * pos)
    ones = jnp.ones((1, LANES), F32)
    sd_f = jnp.exp(lgf * c * ones)[:, :1]
    sd_b = jnp.exp(lgb * c * ones)[:, :1]

    def emit(r0, o, accumulate):
        if not accumulate:
            o_ref[pl.ds(r0, c), :] = o
            return
        out_ref[pl.ds(r0, c), :] = (o_ref[pl.ds(r0, c), :] + o).astype(BF16)

    def fwd_part(i, accumulate):
        r0 = pl.multiple_of(i * c, c)
        q = q_ref[pl.ds(r0, c), :]
        k = k_ref[pl.ds(r0, c), :]
        v = v_ref[pl.ds(r0, c), :]
        s = lax.dot_general(q, k, NT_DIMS, preferred_element_type=F32)
        inner = (s * d_ref[...]).astype(BF16)
        qd = (q.astype(F32) * qd_f).astype(BF16)
        o = (jnp.dot(inner, v, preferred_element_type=F32)
             + jnp.dot(qd, sf_ref[...].astype(BF16), preferred_element_type=F32))
        emit(r0, o, accumulate)
        kd = (k.astype(F32) * kd_f).astype(BF16)
        sf_ref[...] = sf_ref[...] * sd_f + lax.dot_general(kd, v, TN_DIMS, preferred_element_type=F32)

    def bwd_part(i, accumulate):
        r0 = pl.multiple_of(i * c, c)
        q = q_ref[pl.ds(r0, c), :]
        k = k_ref[pl.ds(r0, c), :]
        v = v_ref[pl.ds(r0, c), :]
        qd = (q.astype(F32) * qd_b).astype(BF16)
        o = jnp.dot(qd, sb_ref[...].astype(BF16), preferred_element_type=F32)
        emit(r0, o, accumulate)
        kd = (k.astype(F32) * kd_b).astype(BF16)
        sb_ref[...] = sb_ref[...] * sd_b + lax.dot_general(kd, v, TN_DIMS, preferred_element_type=F32)

    sf_ref[...] = jnp.zeros_like(sf_ref)
    sb_ref[...] = jnp.zeros_like(sb_ref)

    def first_half(j, carry):
        fwd_part(j, False)
        bwd_part(n - 1 - j, False)
        return carry

    def second_half(j, carry):
        fwd_part(j, True)
        bwd_part(n - 1 - j, True)
        return carry

    lax.fori_loop(0, n // 2, first_half, 0, unroll=2)
    lax.fori_loop(n // 2, n, second_half, 0, unroll=2)


def _retention(qkv, log_gamma, seq):
    b = qkv.shape[0]
    kblk = RET_Q // RET_QK_DIM
    vblk = 2 * RET_Q // RET_V_DIM
    state = pltpu.VMEM((RET_QK_DIM, RET_V_DIM), F32)
    return pl.pallas_call(
        functools.partial(_ret_kernel, seq=seq),
        grid=(b, RET_HEADS),
        in_specs=[
            pl.BlockSpec(memory_space=pltpu.SMEM),
            pl.BlockSpec((None, seq, RET_QK_DIM), lambda i, h: (i, 0, h)),
            pl.BlockSpec((None, seq, RET_QK_DIM), lambda i, h: (i, 0, kblk + h)),
            pl.BlockSpec((None, seq, RET_V_DIM), lambda i, h: (i, 0, vblk + h)),
        ],
        out_specs=pl.BlockSpec((None, seq, RET_V_DIM), lambda i, h: (i, 0, h)),
        out_shape=jax.ShapeDtypeStruct((b, seq, RET_V), BF16),
        scratch_shapes=[pltpu.VMEM((seq, RET_V_DIM), F32), state, state,
                        pltpu.VMEM((RET_BLOCK, RET_BLOCK), F32)],
        compiler_params=_params(("parallel", "parallel")),
        name="retention",
    )(log_gamma, qkv, qkv, qkv)


def _ret_out_kernel(x_ref, r_ref, g_ref, w_ref, y_ref):
    parts = []
    for hd in range(RET_HEADS):
        cols = slice(hd * RET_V_DIM, (hd + 1) * RET_V_DIM)
        r = r_ref[:, cols].astype(F32)
        mu = jnp.mean(r, axis=-1, keepdims=True)
        var = jnp.mean(jnp.square(r - mu), axis=-1, keepdims=True)
        rn = (r - mu) * lax.rsqrt(var + NORM_EPS)
        half_gate = g_ref[:, cols].astype(F32) * 0.5
        parts.append(((half_gate + half_gate * jnp.tanh(half_gate)) * rn).astype(BF16))
    act = jnp.concatenate(parts, axis=1)
    y_ref[...] = x_ref[...] + jnp.dot(act, w_ref[...], preferred_element_type=F32)


def _ret_out(x2d, r, proj, w):
    n = x2d.shape[0]
    tm = TOKEN_TILE
    row = lambda width: pl.BlockSpec((tm, width), lambda i: (i, 0))
    gate_blk = (2 * RET_Q + RET_V) // RET_V
    return pl.pallas_call(
        _ret_out_kernel,
        grid=(n // tm,),
        in_specs=[row(D_MODEL), row(RET_V), pl.BlockSpec((tm, RET_V), lambda i: (i, gate_blk)),
                  _resident((RET_V, D_MODEL))],
        out_specs=row(D_MODEL),
        out_shape=jax.ShapeDtypeStruct((n, D_MODEL), F32),
        compiler_params=_params(("parallel",)),
        name="ret_out",
    )(x2d, r, proj, w)


def _rope_tables(seq, dh, lane_freq, sign):
    inv = 1.0 / (ROPE_THETA ** (jnp.arange(0, dh, 2, dtype=F32) / dh))
    ang = jnp.arange(seq, dtype=F32)[:, None] * inv[None, :]
    return jnp.cos(ang)[:, lane_freq], jnp.sin(ang)[:, lane_freq] * sign[None, :]


def _pair_rotary_layout(w):
    rows = w.shape[0]
    w = w.reshape(rows, DIL_HEADS // 2, 2, 2, HEAD_DIM // 2)
    return w.transpose(0, 1, 3, 2, 4).reshape(rows, MIX)


def _prepare(attn_norm, even_w_in, na_rpb, even_w_out, ret_w_in, ret_decay_fwd, ret_decay_bwd, ret_w_out,
             ffn_norm, ffn_w_up, ffn_conv_w, ffn_conv_b, ffn_w_down, final_norm, seq):
    lane = jnp.arange(LANES)
    half = HEAD_DIM // 2
    w_in = even_w_in[0]
    blocks = [w_in[:, i * MIX:(i + 1) * MIX] for i in range(6)]
    scale = HEAD_DIM ** -0.5 * LOG2_E
    blocks[0] = blocks[0] * scale
    blocks[3] = _pair_rotary_layout(blocks[3]) * scale
    blocks[4] = _pair_rotary_layout(blocks[4])
    p = {}
    p["even_w_in"] = jnp.concatenate(blocks, axis=1).astype(BF16)
    p["even_cos"], p["even_sin"] = _rope_tables(seq, HEAD_DIM, lane % half,
                                                jnp.where(lane < LANES // 2, -1.0, 1.0))
    p["na_bias"] = _na_bias_table(na_rpb[0], seq // GRID_W)
    p["even_w_out"] = even_w_out[0].astype(BF16)

    rw = ret_w_in[0]
    rscale = RET_QK_DIM ** -0.5
    p["ret_w_in"] = jnp.concatenate([rw[:, :RET_Q] * rscale, rw[:, RET_Q:]], axis=1).astype(BF16)
    p["ret_cos"], p["ret_sin"] = _rope_tables(seq, RET_QK_DIM, lane, jnp.ones((LANES,), F32))
    p["ret_log_gamma"] = jnp.stack([-jax.nn.softplus(ret_decay_fwd[0].astype(F32)),
                                    -jax.nn.softplus(ret_decay_bwd[0].astype(F32))], axis=0)
    p["ret_w_out"] = ret_w_out[0].astype(BF16)

    u_half = jnp.concatenate([jnp.full((FFN_DIM,), 0.5, F32), jnp.ones((FFN_DIM,), F32)])
    p["ffn_w_up"], p["ffn_conv"] = [], []
    for layer in range(2):
        p["ffn_w_up"].append(ffn_w_up[layer].astype(BF16))
        taps = jnp.concatenate([ffn_conv_w[layer], ffn_conv_b[layer][None, :]], axis=0) * u_half[None, :]
        p["ffn_conv"].append(
            jnp.concatenate([taps, jnp.zeros((SUBLANES - CONV_WIDTH - 1, 2 * FFN_DIM), F32)], axis=0))
    p["ffn_w_down"] = [ffn_w_down[layer].astype(BF16) for layer in range(2)]
    p["attn_norm"] = [attn_norm[layer][None, :] for layer in range(2)]
    p["ffn_norm"] = [ffn_norm[layer][None, :] for layer in range(2)]
    p["final_norm"] = final_norm[None, :]
    return p


def _trunk(x, p):
    b, seq, _ = x.shape
    x2d = x.reshape(b * seq, D_MODEL)
    na_qkv, dil_qkv = _even_proj(x2d, p["attn_norm"][0], p["even_w_in"], p["even_cos"], p["even_sin"], seq)
    oa = _na_attention(na_qkv.reshape(b, seq, 3 * MIX), p["na_bias"], seq).reshape(b * seq, MIX)
    ob = _dil_attention(dil_qkv, seq).reshape(b * seq, MIX)
    x2d = _even_out(x2d, oa, ob, p["even_w_out"])
    x2d = _ffn(x2d, p["ffn_norm"][0], p["ffn_w_up"][0], p["ffn_conv"][0], p["ffn_w_down"][0],
               p["final_norm"], seq, final=False)
    rproj = _ret_proj(x2d, p["attn_norm"][1], p["ret_w_in"], p["ret_cos"], p["ret_sin"], seq)
    r = _retention(rproj.reshape(b, seq, RET_IN), p["ret_log_gamma"], seq)
    x2d = _ret_out(x2d, r.reshape(b * seq, RET_V), rproj, p["ret_w_out"])
    x2d = _ffn(x2d, p["ffn_norm"][1], p["ffn_w_up"][1], p["ffn_conv"][1], p["ffn_w_down"][1],
               p["final_norm"], seq, final=True)
    return x2d.reshape(b, seq, D_MODEL)


def kernel(x_prompt, x_sample, attn_norm, even_w_in, na_rpb, even_w_out, ret_w_in, ret_decay_fwd, ret_decay_bwd,
           ret_w_out, ffn_norm, ffn_w_up, ffn_conv_w, ffn_conv_b, ffn_w_down, final_norm):
    assert x_prompt.shape[1] == x_sample.shape[1]
    p = _prepare(attn_norm, even_w_in, na_rpb, even_w_out, ret_w_in, ret_decay_fwd, ret_decay_bwd, ret_w_out,
                 ffn_norm, ffn_w_up, ffn_conv_w, ffn_conv_b, ffn_w_down, final_norm, x_prompt.shape[1])
    return _trunk(x_prompt, p), _trunk(x_sample, p)
```

```python
import functools

import jax
import jax.numpy as jnp
from jax import lax
from jax.experimental import pallas as pl
from jax.experimental.pallas import tpu as pltpu

D_MODEL = 1024
GRID_W = 64
HEAD_DIM = 64
NA_HEADS = 8
NA_WIN_R = 8
NA_WIN_C = 16
DIL_HEADS = 8
DIL_BRANCHES = ((128, 1), (512, 4), (2048, 16))
DIL_QBLOCK = 128
RET_HEADS = 4
RET_QK_DIM = 256
RET_V_DIM = 512
FFN_DIM = 2816
CONV_WIDTH = 3
ROPE_THETA = 10000.0
NORM_EPS = 1e-6
NEG_INF = -1e30

LANES = 128
SUBLANES = 8
MIX = NA_HEADS * HEAD_DIM
RET_Q = RET_HEADS * RET_QK_DIM
RET_V = RET_HEADS * RET_V_DIM
RET_IN = 2 * RET_Q + 2 * RET_V
RET_BLOCK = 256
FFN_CHUNK = 256
N_FFN_CHUNKS = FFN_DIM // FFN_CHUNK
TOKEN_TILE = 1024
OUT_TILE = 1024
FFN_TILE = 512
ATTN_GROUP = 4
NA_GROUP = 4
HALO = SUBLANES
RESIDUES = max(d for _, d in DIL_BRANCHES)
SLAB_PITCH = RESIDUES + 4
VMEM_LIMIT = 56 * 1024 * 1024
LOG2_E = 1.4426950408889634
GELU_C0 = 0.7978845608028654
GELU_C1 = GELU_C0 * 0.044715

F32 = jnp.float32
BF16 = jnp.bfloat16
NT_DIMS = (((1,), (1,)), ((), ()))
TN_DIMS = (((0,), (0,)), ((), ()))


def _params(sem, vmem=VMEM_LIMIT):
    return pltpu.CompilerParams(dimension_semantics=sem, vmem_limit_bytes=vmem)


def _resident(shape):
    nd = len(shape)
    return pl.BlockSpec(shape, lambda *_: (0,) * nd, pipeline_mode=pl.Buffered(1))


def _rms(x, g):
    ms = jnp.mean(x * x, axis=-1, keepdims=True)
    return x * lax.rsqrt(ms + NORM_EPS) * g


def _software_pipeline(n_groups, scores, softmax, values):
    assert n_groups % 2 == 0 and n_groups >= 4
    scores(0, 0)
    scores(1, 1)
    softmax(0, 0)

    def body(gg, carry):
        g = 2 * gg
        scores(g + 2, 0)
        softmax(g + 1, 1)
        values(g, 0)
        scores(g + 3, 1)
        softmax(g + 2, 0)
        values(g + 1, 1)
        return carry

    lax.fori_loop(0, n_groups // 2 - 1, body, 0)
    softmax(n_groups - 1, 1)
    values(n_groups - 2, 0)
    values(n_groups - 1, 1)


def _even_proj_kernel(x_ref, g_ref, w_ref, cos_ref, sin_ref, na_ref, perm_ref, slab_ref, *, tm):
    h = _rms(x_ref[...], g_ref[...]).astype(BF16)
    per_res = tm // RESIDUES
    tiles = MIX // LANES
    for c in (3, 4, 5, 0, 1, 2):
        a = jnp.dot(h, w_ref[:, c * MIX:(c + 1) * MIX], preferred_element_type=F32)
        if c in (3, 4):
            cos, sin = cos_ref[...], sin_ref[...]
            parts = []
            for j in range(tiles):
                aj = a[:, j * LANES:(j + 1) * LANES]
                parts.append(aj * cos + pltpu.roll(aj, LANES // 2, axis=1) * sin)
            a = jnp.concatenate(parts, axis=1)
        if c < 3:
            na_ref[:, c * MIX:(c + 1) * MIX] = a.astype(BF16)
        else:
            slab = slab_ref.at[c % 2]
            for j in range(tiles):
                for m in range(per_res):
                    slab[j, m * SLAB_PITCH:m * SLAB_PITCH + RESIDUES, :] = (
                        a[m * RESIDUES:(m + 1) * RESIDUES, j * LANES:(j + 1) * LANES])
            for r in range(RESIDUES):
                for j in range(tiles):
                    col = (c - 3) * MIX + j * LANES
                    rows = slab[j, pl.ds(r, per_res, stride=SLAB_PITCH), :]
                    perm_ref[r, :, col:col + LANES] = rows.astype(BF16)


def _even_proj(x2d, g, w, cos, sin, seq):
    n = x2d.shape[0]
    b = n // seq
    tm = TOKEN_TILE
    tps = seq // tm
    per_res = tm // RESIDUES
    return pl.pallas_call(
        functools.partial(_even_proj_kernel, tm=tm),
        grid=(n // tm,),
        in_specs=[
            pl.BlockSpec((tm, D_MODEL), lambda i: (i, 0)),
            _resident((1, D_MODEL)),
            _resident((D_MODEL, 6 * MIX)),
            pl.BlockSpec((tm, LANES), lambda i: (i % tps, 0)),
            pl.BlockSpec((tm, LANES), lambda i: (i % tps, 0)),
        ],
        out_specs=[pl.BlockSpec((tm, 3 * MIX), lambda i: (i, 0)),
                   pl.BlockSpec((None, RESIDUES, per_res, 3 * MIX), lambda i: (i // tps, 0, i % tps, 0))],
        out_shape=[jax.ShapeDtypeStruct((n, 3 * MIX), BF16),
                   jax.ShapeDtypeStruct((b, RESIDUES, seq // RESIDUES, 3 * MIX), BF16)],
        scratch_shapes=[pltpu.VMEM((2, MIX // LANES, per_res * SLAB_PITCH, LANES), F32)],
        compiler_params=_params(("parallel",)),
        name="even_proj",
    )(x2d, g, w, cos, sin)


def _na_kernel(q_ref, k_ref, v_ref, b_ref, o_ref, s0_ref, s1_ref, p0_ref, p1_ref, r0_ref, r1_ref, *, rows):
    lane = lax.broadcasted_iota(jnp.int32, (GRID_W, LANES), 1)
    is_a = lane < HEAD_DIM
    win = NA_WIN_R * GRID_W
    s_bufs, p_bufs, r_bufs = (s0_ref, s1_ref), (p0_ref, p1_ref), (r0_ref, r1_ref)

    def window_start(r):
        rs = jnp.clip(r - NA_WIN_R // 2, 0, rows - NA_WIN_R)
        return pl.multiple_of(rs * GRID_W, GRID_W)

    def scores(g, slot):
        for u in range(NA_GROUP):
            r = jnp.int32(g * NA_GROUP + u)
            pat = jnp.where(r < NA_WIN_R // 2, r,
                            jnp.where(r > rows - NA_WIN_R // 2, r - (rows - NA_WIN_R), NA_WIN_R // 2))
            q = q_ref[pl.ds(pl.multiple_of(r * GRID_W, GRID_W), GRID_W), :]
            zero = jnp.zeros_like(q)
            q2 = jnp.concatenate([jnp.where(is_a, q, zero), jnp.where(is_a, zero, q)], axis=0)
            kw = k_ref[pl.ds(window_start(r), win), :]
            s = lax.dot_general(q2, kw, NT_DIMS, preferred_element_type=F32)
            s_bufs[slot][u] = s + b_ref[pat]

    def softmax(g, slot):
        for u in range(NA_GROUP):
            s = s_bufs[slot][u]
            m = jnp.max(s, axis=1, keepdims=True)
            e = jnp.exp2(s - m)
            l = jnp.sum(e, axis=1, keepdims=True)
            p_bufs[slot][u] = e.astype(BF16)
            r_bufs[slot][u] = jnp.broadcast_to(1.0 / l, (2 * GRID_W, LANES))

    def values(g, slot):
        for u in range(NA_GROUP):
            r = jnp.int32(g * NA_GROUP + u)
            vw = v_ref[pl.ds(window_start(r), win), :]
            pv = jnp.dot(p_bufs[slot][u], vw, preferred_element_type=F32) * r_bufs[slot][u]
            o_ref[pl.ds(pl.multiple_of(r * GRID_W, GRID_W), GRID_W), :] = (
                jnp.where(is_a, pv[:GRID_W], pv[GRID_W:]).astype(BF16))

    _software_pipeline(rows // NA_GROUP, scores, softmax, values)


def _na_attention(qkv, bias, seq):
    b = qkv.shape[0]
    rows = seq // GRID_W
    npair = NA_HEADS // 2
    win = NA_WIN_R * GRID_W
    blk = lambda off: pl.BlockSpec((None, seq, LANES), lambda p, i: (i, 0, off + p))
    s_buf = pltpu.VMEM((NA_GROUP, 2 * GRID_W, win), F32)
    p_buf = pltpu.VMEM((NA_GROUP, 2 * GRID_W, win), BF16)
    r_buf = pltpu.VMEM((NA_GROUP, 2 * GRID_W, LANES), F32)
    return pl.pallas_call(
        functools.partial(_na_kernel, rows=rows),
        grid=(npair, b),
        in_specs=[blk(0), blk(npair), blk(2 * npair),
                  pl.BlockSpec((NA_WIN_R, None, 2 * GRID_W, win), lambda p, i: (0, p, 0, 0))],
        out_specs=pl.BlockSpec((None, seq, LANES), lambda p, i: (i, 0, p)),
        out_shape=jax.ShapeDtypeStruct((b, seq, MIX), BF16),
        scratch_shapes=[s_buf, s_buf, p_buf, p_buf, r_buf, r_buf],
        compiler_params=_params(("parallel", "parallel")),
        name="na_attention",
    )(qkv, qkv, qkv, bias)


def _na_bias_table(rpb, rows):
    c = jnp.arange(GRID_W)
    cs = jnp.clip(c - NA_WIN_C // 2, 0, GRID_W - NA_WIN_C)
    j = jnp.arange(GRID_W)
    valid = (j[None, :] >= cs[:, None]) & (j[None, :] < cs[:, None] + NA_WIN_C)
    k = jnp.arange(2 * NA_WIN_C - 1)
    pick = (k[None, None, :] == (j[None, :, None] - c[:, None, None] + NA_WIN_C - 1)).astype(F32)
    full = jnp.einsum("hrk,cjk->hcrj", rpb.astype(F32), pick, precision=lax.Precision.HIGHEST)
    half = NA_WIN_R // 2
    reps = list(range(half)) + [half] + list(range(rows - half + 1, rows))
    tabs = []
    for r in reps:
        rs = min(max(r - half, 0), rows - NA_WIN_R)
        first = rs - r + NA_WIN_R - 1
        t = jnp.where(valid[None, :, None, :], full[:, :, first:first + NA_WIN_R, :] * LOG2_E, NEG_INF)
        tabs.append(t.reshape(NA_HEADS // 2, 2 * GRID_W, NA_WIN_R * GRID_W))
    return jnp.stack(tabs, axis=0)


def _dil_kernel(q_ref, k_ref, v_ref, o_ref, qf_ref, kf_ref, vf_ref, acc_ref, m_ref, l_ref, nat_ref, mask_ref,
                s0_ref, s1_ref, p0_ref, p1_ref, m0_ref, m1_ref, l0_ref, l1_ref, *, slab_len):
    qb, kw_rows = DIL_QBLOCK, 2 * DIL_QBLOCK
    s_bufs, p_bufs = (s0_ref, s1_ref), (p0_ref, p1_ref)
    m_bufs, l_bufs = (m0_ref, m1_ref), (l0_ref, l1_ref)
    lane = lax.broadcasted_iota(jnp.int32, (qb, LANES), 1)
    is_a_out = lane < HEAD_DIM
    is_a_rot = (lane & (HEAD_DIM - 1)) < HEAD_DIM // 2
    row = lax.broadcasted_iota(jnp.int32, (2 * qb, kw_rows), 0) & (qb - 1)
    col = lax.broadcasted_iota(jnp.int32, (2 * qb, kw_rows), 1)

    qf_ref[...] = q_ref[...].astype(F32)
    kf_ref[...] = k_ref[...].astype(F32)
    vf_ref[...] = v_ref[...].astype(F32)

    branches = sorted(DIL_BRANCHES, key=lambda wd: -wd[1])
    for bi, (window, d) in enumerate(branches):
        g_cnt = RESIDUES // d
        rq, rk = qb // g_cnt, kw_rows // g_cnt
        base = g_cnt * ((col % rk) - (row % rq)) + (col // rk) - (row // rq)
        for pat, off in enumerate((0, (rq - rk) // 2, rq - rk)):
            on_band = jnp.abs(base + g_cnt * off) <= (window // 2) // d
            mask_ref[3 * bi + pat] = jnp.where(on_band, jnp.inf, NEG_INF)

    for bi, (window, d) in enumerate(branches):
        g_cnt = RESIDUES // d
        rq, rk = qb // g_cnt, kw_rows // g_cnt
        lead = (rk - rq) // 2
        nblk = slab_len // rq

        def tile(g, u, d=d, g_cnt=g_cnt, rq=rq, rk=rk, lead=lead, nblk=nblk):
            it = jnp.int32(g * ATTN_GROUP + u)
            i = it % nblk
            q0 = i * rq
            ws = jnp.clip(q0 - lead, 0, slab_len - rk)
            pat = jnp.where(i == 0, 0, jnp.where(i == nblk - 1, 2, 1))
            slabs = [it // nblk + d * g_ for g_ in range(g_cnt)]
            return slabs, q0, pl.multiple_of(q0, rq), ws, pat

        def window(ref, ref32, slabs, ws, rq=rq, rk=rk):
            if rq % 16 == 0:
                wsa = pl.multiple_of(ws, 16)
                return jnp.concatenate([ref[s_, pl.ds(wsa, rk), :] for s_ in slabs], axis=0)
            return jnp.concatenate([ref32[s_, pl.ds(ws, rk), :] for s_ in slabs], axis=0).astype(BF16)

        def scores(g, slot, bi=bi, rq=rq, tile=tile, window=window):
            for u in range(ATTN_GROUP):
                slabs, _, q0a, ws, pat = tile(g, u)
                if rq % 16 == 0:
                    q = jnp.concatenate([q_ref[s_, pl.ds(q0a, rq), :] for s_ in slabs], axis=0)
                else:
                    q = jnp.concatenate([qf_ref[s_, pl.ds(q0a, rq), :] for s_ in slabs], axis=0).astype(BF16)
                kw = window(k_ref, kf_ref, slabs, ws)
                zero = jnp.zeros_like(q)
                q2 = jnp.concatenate([jnp.where(is_a_rot, q, zero), jnp.where(is_a_rot, zero, q)], axis=0)
                s = lax.dot_general(q2, kw, NT_DIMS, preferred_element_type=F32)
                s_bufs[slot][u] = jnp.minimum(s, mask_ref[3 * bi + pat])

        def softmax(g, slot):
            for u in range(ATTN_GROUP):
                s = s_bufs[slot][u]
                m = jnp.max(s, axis=1, keepdims=True)
                e = jnp.exp2(s - m)
                l = jnp.sum(e, axis=1, keepdims=True)
                p_bufs[slot][u] = e.astype(BF16)
                m_bufs[slot][u] = jnp.where(is_a_out, m[:qb], m[qb:])
                l_bufs[slot][u] = jnp.where(is_a_out, l[:qb], l[qb:])

        def values(g, slot, bi=bi, rq=rq, tile=tile, window=window):
            for u in range(ATTN_GROUP):
                slabs, q0, q0a, ws, _ = tile(g, u)
                vw = window(v_ref, vf_ref, slabs, ws)
                pv = jnp.dot(p_bufs[slot][u], vw, preferred_element_type=F32)
                pv_t = jnp.where(is_a_out, pv[:qb], pv[qb:])
                m_t = m_bufs[slot][u]
                l_t = l_bufs[slot][u]
                if bi > 0:
                    m_old = jnp.concatenate([m_ref[s_, pl.ds(q0a, rq), :] for s_ in slabs], axis=0)
                    l_old = jnp.concatenate([l_ref[s_, pl.ds(q0a, rq), :] for s_ in slabs], axis=0)
                    a_old = jnp.concatenate([acc_ref[s_, pl.ds(q0a, rq), :] for s_ in slabs], axis=0)
                    m_new = jnp.maximum(m_old, m_t)
                    w_old = jnp.exp2(m_old - m_new)
                    w_new = jnp.exp2(m_t - m_new)
                    l_t = l_old * w_old + l_t * w_new
                    pv_t = a_old * w_old + pv_t * w_new
                    m_t = m_new
                if bi < len(branches) - 1:
                    for g_, s_ in enumerate(slabs):
                        rows = slice(g_ * rq, (g_ + 1) * rq)
                        m_ref[s_, pl.ds(q0a, rq), :] = m_t[rows]
                        l_ref[s_, pl.ds(q0a, rq), :] = l_t[rows]
                        acc_ref[s_, pl.ds(q0a, rq), :] = pv_t[rows]
                else:
                    out = pv_t / l_t
                    t0 = q0 * RESIDUES
                    for g_, s_ in enumerate(slabs):
                        nat_ref[pl.ds(t0 + s_, rq, stride=RESIDUES), :] = out[g_ * rq:(g_ + 1) * rq]
                    t0a = pl.multiple_of(t0, qb)
                    o_ref[pl.ds(t0a, qb), :] = nat_ref[pl.ds(t0a, qb), :].astype(BF16)

        _software_pipeline(d * nblk // ATTN_GROUP, scores, softmax, values)


def _dil_attention(perm, seq):
    b = perm.shape[0]
    slab_len = seq // RESIDUES
    npair = DIL_HEADS // 2
    blk = lambda off: pl.BlockSpec((None, RESIDUES, slab_len, LANES), lambda i, p: (i, 0, 0, off + p))
    slab = pltpu.VMEM((RESIDUES, slab_len, LANES), F32)
    s_buf = pltpu.VMEM((ATTN_GROUP, 2 * DIL_QBLOCK, 2 * DIL_QBLOCK), F32)
    p_buf = pltpu.VMEM((ATTN_GROUP, 2 * DIL_QBLOCK, 2 * DIL_QBLOCK), BF16)
    stat = pltpu.VMEM((ATTN_GROUP, DIL_QBLOCK, LANES), F32)
    return pl.pallas_call(
        functools.partial(_dil_kernel, slab_len=slab_len),
        grid=(b, npair),
        in_specs=[blk(0), blk(npair), blk(2 * npair)],
        out_specs=pl.BlockSpec((None, seq, LANES), lambda i, p: (i, 0, p)),
        out_shape=jax.ShapeDtypeStruct((b, seq, MIX), BF16),
        scratch_shapes=[slab, slab, slab, slab, slab, slab,
                        pltpu.VMEM((seq, LANES), F32),
                        pltpu.VMEM((3 * len(DIL_BRANCHES), 2 * DIL_QBLOCK, 2 * DIL_QBLOCK), F32),
                        s_buf, s_buf, p_buf, p_buf, stat, stat, stat, stat],
        compiler_params=_params(("parallel", "parallel")),
        name="dilated_attention",
    )(perm, perm, perm)


def _even_out_kernel(x_ref, oa_ref, ob_ref, w_ref, y_ref):
    o = jnp.concatenate([oa_ref[...], ob_ref[...]], axis=1)
    y_ref[...] = x_ref[...] + jnp.dot(o, w_ref[...], preferred_element_type=F32)


def _even_out(x2d, oa, ob, w):
    n = x2d.shape[0]
    tm = OUT_TILE
    row = lambda width: pl.BlockSpec((tm, width), lambda i: (i, 0))
    return pl.pallas_call(
        _even_out_kernel,
        grid=(n // tm,),
        in_specs=[row(D_MODEL), row(MIX), row(MIX), _resident((2 * MIX, D_MODEL))],
        out_specs=row(D_MODEL),
        out_shape=jax.ShapeDtypeStruct((n, D_MODEL), F32),
        compiler_params=_params(("parallel",)),
        name="even_out",
    )(x2d, oa, ob, w)


def _ffn_kernel(x_ref, xn_ref, xp_ref, g_ref, wu_ref, cp_ref, wd_ref, fg_ref, o_ref, h_ref, a_ref, act_ref, *,
                tm, tiles_per_seq, final):
    pos = pl.program_id(0) % tiles_per_seq
    g = g_ref[...]
    keep_prev = jnp.where(pos == 0, 0.0, 1.0)
    keep_next = jnp.where(pos == tiles_per_seq - 1, 0.0, 1.0)
    h_ref[0:tm, :] = _rms(x_ref[...], g).astype(BF16)
    halo = jnp.concatenate([_rms(xn_ref[...], g) * keep_next, _rms(xp_ref[...], g) * keep_prev], axis=0)
    h_ref[tm:tm + 2 * HALO, :] = halo.astype(BF16)
    tiles = FFN_CHUNK // LANES
    for c in range(N_FFN_CHUNKS):
        buf = a_ref.at[c % 2]
        ys = []
        for part in range(2):
            first = part * FFN_DIM + c * FFN_CHUNK
            a = jnp.dot(h_ref[...], wu_ref[:, first:first + FFN_CHUNK], preferred_element_type=F32)
            for j in range(tiles):
                cols = slice(j * LANES, (j + 1) * LANES)
                buf[part * tiles + j, HALO:tm + 2 * HALO, :] = a[:tm + HALO, cols]
                buf[part * tiles + j, 0:HALO, :] = a[tm + HALO:, cols]
        for part in range(2):
            for j in range(tiles):
                slab = buf.at[part * tiles + j]
                first = part * FFN_DIM + c * FFN_CHUNK + j * LANES
                cols = slice(first, first + LANES)
                y = cp_ref[3:4, cols] + slab[HALO - 1:HALO - 1 + tm, :] * cp_ref[0:1, cols]
                y = y + slab[HALO:HALO + tm, :] * cp_ref[1:2, cols]
                y = y + slab[HALO + 1:HALO + 1 + tm, :] * cp_ref[2:3, cols]
                ys.append(y)
        half_u = jnp.concatenate(ys[:tiles], axis=1)
        gate = jnp.concatenate(ys[tiles:], axis=1)
        t = jnp.tanh(gate * (GELU_C0 + GELU_C1 * (gate * gate)))
        w = half_u * gate
        act_ref[:, c * FFN_CHUNK:(c + 1) * FFN_CHUNK] = (w + w * t).astype(BF16)
    out = x_ref[...] + jnp.dot(act_ref[...], wd_ref[...], preferred_element_type=F32)
    if final:
        out = _rms(out, fg_ref[...])
    o_ref[...] = out


def _ffn(x2d, g, wu, cp, wd, fg, seq, final):
    n = x2d.shape[0]
    tm = FFN_TILE
    tps = seq // tm
    hb = tm // HALO
    last = n // HALO - 1
    return pl.pallas_call(
        functools.partial(_ffn_kernel, tm=tm, tiles_per_seq=tps, final=final),
        grid=(n // tm,),
        in_specs=[
            pl.BlockSpec((tm, D_MODEL), lambda i: (i, 0)),
            pl.BlockSpec((HALO, D_MODEL), lambda i: (jnp.minimum((i + 1) * hb, last), 0)),
            pl.BlockSpec((HALO, D_MODEL), lambda i: (jnp.maximum(i * hb - 1, 0), 0)),
            _resident((1, D_MODEL)),
            _resident((D_MODEL, 2 * FFN_DIM)),
            _resident((SUBLANES, 2 * FFN_DIM)),
            _resident((FFN_DIM, D_MODEL)),
            _resident((1, D_MODEL)),
        ],
        out_specs=pl.BlockSpec((tm, D_MODEL), lambda i: (i, 0)),
        out_shape=jax.ShapeDtypeStruct((n, D_MODEL), F32),
        scratch_shapes=[pltpu.VMEM((tm + 2 * HALO, D_MODEL), BF16),
                        pltpu.VMEM((2, 2 * FFN_CHUNK // LANES, tm + 2 * HALO, LANES), F32),
                        pltpu.VMEM((tm, FFN_DIM), BF16)],
        compiler_params=_params(("parallel",)),
        name="conv_ffn",
    )(x2d, x2d, x2d, g, wu, cp, wd, fg)


def _ret_proj_kernel(x_ref, g_ref, w_ref, cos_ref, sin_ref, o_ref):
    h = _rms(x_ref[...], g_ref[...]).astype(BF16)
    for c in range(RET_IN // MIX):
        a = jnp.dot(h, w_ref[:, c * MIX:(c + 1) * MIX], preferred_element_type=F32)
        if c < 2 * RET_Q // MIX:
            cos, sin = cos_ref[...], sin_ref[...]
            parts = []
            for j in range(MIX // RET_QK_DIM):
                x1 = a[:, j * RET_QK_DIM:j * RET_QK_DIM + LANES]
                x2 = a[:, j * RET_QK_DIM + LANES:(j + 1) * RET_QK_DIM]
                parts += [x1 * cos - x2 * sin, x1 * sin + x2 * cos]
            a = jnp.concatenate(parts, axis=1)
        o_ref[:, c * MIX:(c + 1) * MIX] = a.astype(BF16)


def _ret_proj(x2d, g, w, cos, sin, seq):
    n = x2d.shape[0]
    tm = TOKEN_TILE
    tps = seq // tm
    return pl.pallas_call(
        _ret_proj_kernel,
        grid=(n // tm,),
        in_specs=[
            pl.BlockSpec((tm, D_MODEL), lambda i: (i, 0)),
            _resident((1, D_MODEL)),
            _resident((D_MODEL, RET_IN)),
            pl.BlockSpec((tm, LANES), lambda i: (i % tps, 0)),
            pl.BlockSpec((tm, LANES), lambda i: (i % tps, 0)),
        ],
        out_specs=pl.BlockSpec((tm, RET_IN), lambda i: (i, 0)),
        out_shape=jax.ShapeDtypeStruct((n, RET_IN), BF16),
        compiler_params=_params(("parallel",)),
        name="ret_proj",
    )(x2d, g, w, cos, sin)


def _ret_kernel(lg_ref, q_ref, k_ref, v_ref, out_ref, o_ref, sf_ref, sb_ref, d_ref, *, seq):
    c = RET_BLOCK
    n = seq // c
    hd = pl.program_id(1)
    lgf = lg_ref[0, hd]
    lgb = lg_ref[1, hd]
    pr = lax.broadcasted_iota(jnp.int32, (c, c), 0).astype(F32)
    pc = lax.broadcasted_iota(jnp.int32, (c, c), 1).astype(F32)
    diff = pr - pc
    lower = diff >= 0
    d_f = jnp.where(lower, jnp.exp(lgf * jnp.where(lower, diff, 0.0)), 0.0)
    d_b = jnp.where(lower, 0.0, jnp.exp(lgb * jnp.where(lower, 0.0, -diff)))
    d_ref[...] = d_f + d_b
    pos = lax.broadcasted_iota(jnp.int32, (c, 1), 0).astype(F32)
    qd_f = jnp.exp(lgf * (pos + 1.0))
    kd_f = jnp.exp(lgf * (c - 1.0 - pos))
    qd_b = jnp.exp(lgb * (c - pos))
    kd_b = jnp.exp(lgb * pos)
    ones = jnp.ones((1, LANES), F32)
    sd_f = jnp.exp(lgf * c * ones)[:, :1]
    sd_b = jnp.exp(lgb * c * ones)[:, :1]

    def emit(r0, o, accumulate):
        if not accumulate:
            o_ref[pl.ds(r0, c), :] = o
            return
        out_ref[pl.ds(r0, c), :] = (o_ref[pl.ds(r0, c), :] + o).astype(BF16)

    def fwd_part(i, accumulate):
        r0 = pl.multiple_of(i * c, c)
        q = q_ref[pl.ds(r0, c), :]
        k = k_ref[pl.ds(r0, c), :]
        v = v_ref[pl.ds(r0, c), :]
        s = lax.dot_general(q, k, NT_DIMS, preferred_element_type=F32)
        inner = (s * d_ref[...]).astype(BF16)
        qd = (q.astype(F32) * qd_f).astype(BF16)
        o = (jnp.dot(inner, v, preferred_element_type=F32)
             + jnp.dot(qd, sf_ref[...].astype(BF16), preferred_element_type=F32))
        emit(r0, o, accumulate)
        kd = (k.astype(F32) * kd_f).astype(BF16)
        sf_ref[...] = sf_ref[...] * sd_f + lax.dot_general(kd, v, TN_DIMS, preferred_element_type=F32)

    def bwd_part(i, accumulate):
        r0 = pl.multiple_of(i * c, c)
        q = q_ref[pl.ds(r0, c), :]
        k = k_ref[pl.ds(r0, c), :]
        v = v_ref[pl.ds(r0, c), :]
        qd = (q.astype(F32) * qd_b).astype(BF16)
        o = jnp.dot(qd, sb_ref[...].astype(BF16), preferred_element_type=F32)
        emit(r0, o, accumulate)
        kd = (k.astype(F32) * kd_b).astype(BF16)
        sb_ref[...] = sb_ref[...] * sd_b + lax.dot_general(kd, v, TN_DIMS, preferred_element_type=F32)

    sf_ref[...] = jnp.zeros_like(sf_ref)
    sb_ref[...] = jnp.zeros_like(sb_ref)

    def first_half(j, carry):
        fwd_part(j, False)
        bwd_part(n - 1 - j, False)
        return carry

    def second_half(j, carry):
        fwd_part(j, True)
        bwd_part(n - 1 - j, True)
        return carry

    lax.fori_loop(0, n // 2, first_half, 0, unroll=4)
    lax.fori_loop(n // 2, n, second_half, 0, unroll=4)


def _retention(qkv, log_gamma, seq):
    b = qkv.shape[0]
    kblk = RET_Q // RET_QK_DIM
    vblk = 2 * RET_Q // RET_V_DIM
    state = pltpu.VMEM((RET_QK_DIM, RET_V_DIM), F32)
    return pl.pallas_call(
        functools.partial(_ret_kernel, seq=seq),
        grid=(b, RET_HEADS),
        in_specs=[
            pl.BlockSpec(memory_space=pltpu.SMEM),
            pl.BlockSpec((None, seq, RET_QK_DIM), lambda i, h: (i, 0, h)),
            pl.BlockSpec((None, seq, RET_QK_DIM), lambda i, h: (i, 0, kblk + h)),
            pl.BlockSpec((None, seq, RET_V_DIM), lambda i, h: (i, 0, vblk + h)),
        ],
        out_specs=pl.BlockSpec((None, seq, RET_V_DIM), lambda i, h: (i, 0, h)),
        out_shape=jax.ShapeDtypeStruct((b, seq, RET_V), BF16),
        scratch_shapes=[pltpu.VMEM((seq, RET_V_DIM), F32), state, state,
                        pltpu.VMEM((RET_BLOCK, RET_BLOCK), F32)],
        compiler_params=_params(("parallel", "parallel")),
        name="retention",
    )(log_gamma, qkv, qkv, qkv)


def _ret_out_kernel(x_ref, r_ref, g_ref, w_ref, y_ref, *, tm):
    for r0 in range(0, tm, tm // 2):
        rows = slice(r0, r0 + tm // 2)
        parts = []
        for hd in range(RET_HEADS):
            cols = slice(hd * RET_V_DIM, (hd + 1) * RET_V_DIM)
            r = r_ref[rows, cols].astype(F32)
            mu = jnp.mean(r, axis=-1, keepdims=True)
            var = jnp.mean(jnp.square(r - mu), axis=-1, keepdims=True)
            rn = (r - mu) * lax.rsqrt(var + NORM_EPS)
            half_gate = g_ref[rows, cols].astype(F32) * 0.5
            parts.append(((half_gate + half_gate * jnp.tanh(half_gate)) * rn).astype(BF16))
        act = jnp.concatenate(parts, axis=1)
        y_ref[rows, :] = x_ref[rows, :] + jnp.dot(act, w_ref[...], preferred_element_type=F32)


def _ret_out(x2d, r, proj, w):
    n = x2d.shape[0]
    tm = OUT_TILE
    row = lambda width: pl.BlockSpec((tm, width), lambda i: (i, 0))
    gate_blk = (2 * RET_Q + RET_V) // RET_V
    return pl.pallas_call(
        functools.partial(_ret_out_kernel, tm=tm),
        grid=(n // tm,),
        in_specs=[row(D_MODEL), row(RET_V), pl.BlockSpec((tm, RET_V), lambda i: (i, gate_blk)),
                  _resident((RET_V, D_MODEL))],
        out_specs=row(D_MODEL),
        out_shape=jax.ShapeDtypeStruct((n, D_MODEL), F32),
        compiler_params=_params(("parallel",)),
        name="ret_out",
    )(x2d, r, proj, w)


def _rope_tables(seq, dh, lane_freq, sign):
    inv = 1.0 / (ROPE_THETA ** (jnp.arange(0, dh, 2, dtype=F32) / dh))
    ang = jnp.arange(seq, dtype=F32)[:, None] * inv[None, :]
    return jnp.cos(ang)[:, lane_freq], jnp.sin(ang)[:, lane_freq] * sign[None, :]


def _pair_rotary_layout(w):
    rows = w.shape[0]
    w = w.reshape(rows, DIL_HEADS // 2, 2, 2, HEAD_DIM // 2)
    return w.transpose(0, 1, 3, 2, 4).reshape(rows, MIX)


def _prepare(attn_norm, even_w_in, na_rpb, even_w_out, ret_w_in, ret_decay_fwd, ret_decay_bwd, ret_w_out,
             ffn_norm, ffn_w_up, ffn_conv_w, ffn_conv_b, ffn_w_down, final_norm, seq):
    lane = jnp.arange(LANES)
    half = HEAD_DIM // 2
    w_in = even_w_in[0]
    blocks = [w_in[:, i * MIX:(i + 1) * MIX] for i in range(6)]
    scale = HEAD_DIM ** -0.5 * LOG2_E
    blocks[0] = blocks[0] * scale
    blocks[3] = _pair_rotary_layout(blocks[3]) * scale
    blocks[4] = _pair_rotary_layout(blocks[4])
    p = {}
    p["even_w_in"] = jnp.concatenate(blocks, axis=1).astype(BF16)
    p["even_cos"], p["even_sin"] = _rope_tables(seq, HEAD_DIM, lane % half,
                                                jnp.where(lane < LANES // 2, -1.0, 1.0))
    p["na_bias"] = _na_bias_table(na_rpb[0], seq // GRID_W)
    p["even_w_out"] = even_w_out[0].astype(BF16)

    rw = ret_w_in[0]
    rscale = RET_QK_DIM ** -0.5
    p["ret_w_in"] = jnp.concatenate([rw[:, :RET_Q] * rscale, rw[:, RET_Q:]], axis=1).astype(BF16)
    p["ret_cos"], p["ret_sin"] = _rope_tables(seq, RET_QK_DIM, lane, jnp.ones((LANES,), F32))
    p["ret_log_gamma"] = jnp.stack([-jax.nn.softplus(ret_decay_fwd[0].astype(F32)),
                                    -jax.nn.softplus(ret_decay_bwd[0].astype(F32))], axis=0)
    p["ret_w_out"] = ret_w_out[0].astype(BF16)

    u_half = jnp.concatenate([jnp.full((FFN_DIM,), 0.5, F32), jnp.ones((FFN_DIM,), F32)])
    p["ffn_w_up"], p["ffn_conv"] = [], []
    for layer in range(2):
        p["ffn_w_up"].append(ffn_w_up[layer].astype(BF16))
        taps = jnp.concatenate([ffn_conv_w[layer], ffn_conv_b[layer][None, :]], axis=0) * u_half[None, :]
        p["ffn_conv"].append(
            jnp.concatenate([taps, jnp.zeros((SUBLANES - CONV_WIDTH - 1, 2 * FFN_DIM), F32)], axis=0))
    p["ffn_w_down"] = [ffn_w_down[layer].astype(BF16) for layer in range(2)]
    p["attn_norm"] = [attn_norm[layer][None, :] for layer in range(2)]
    p["ffn_norm"] = [ffn_norm[layer][None, :] for layer in range(2)]
    p["final_norm"] = final_norm[None, :]
    return p


def _trunk(x, p):
    b, seq, _ = x.shape
    x2d = x.reshape(b * seq, D_MODEL)
    na_qkv, dil_qkv = _even_proj(x2d, p["attn_norm"][0], p["even_w_in"], p["even_cos"], p["even_sin"], seq)
    oa = _na_attention(na_qkv.reshape(b, seq, 3 * MIX), p["na_bias"], seq).reshape(b * seq, MIX)
    ob = _dil_attention(dil_qkv, seq).reshape(b * seq, MIX)
    x2d = _even_out(x2d, oa, ob, p["even_w_out"])
    x2d = _ffn(x2d, p["ffn_norm"][0], p["ffn_w_up"][0], p["ffn_conv"][0], p["ffn_w_down"][0],
               p["final_norm"], seq, final=False)
    rproj = _ret_proj(x2d, p["attn_norm"][1], p["ret_w_in"], p["ret_cos"], p["ret_sin"], seq)
    r = _retention(rproj.reshape(b, seq, RET_IN), p["ret_log_gamma"], seq)
    x2d = _ret_out(x2d, r.reshape(b * seq, RET_V), rproj, p["ret_w_out"])
    x2d = _ffn(x2d, p["ffn_norm"][1], p["ffn_w_up"][1], p["ffn_conv"][1], p["ffn_w_down"][1],
               p["final_norm"], seq, final=True)
    return x2d.reshape(b, seq, D_MODEL)


def kernel(x_prompt, x_sample, attn_norm, even_w_in, na_rpb, even_w_out, ret_w_in, ret_decay_fwd, ret_decay_bwd,
           ret_w_out, ffn_norm, ffn_w_up, ffn_conv_w, ffn_conv_b, ffn_w_down, final_norm):
    assert x_prompt.shape[1] == x_sample.shape[1]
    p = _prepare(attn_norm, even_w_in, na_rpb, even_w_out, ret_w_in, ret_decay_fwd, ret_decay_bwd, ret_w_out,
                 ffn_norm, ffn_w_up, ffn_conv_w, ffn_conv_b, ffn_w_down, final_norm, x_prompt.shape[1])
    return _trunk(x_prompt, p), _trunk(x_sample, p)
```

```python
import functools

import jax
import jax.numpy as jnp
from jax import lax
from jax.experimental import pallas as pl
from jax.experimental.pallas import tpu as pltpu

D_MODEL = 1024
GRID_W = 64
HEAD_DIM = 64
NA_HEADS = 8
NA_WIN_R = 8
NA_WIN_C = 16
DIL_HEADS = 8
DIL_BRANCHES = ((128, 1), (512, 4), (2048, 16))
DIL_QBLOCK = 128
RET_HEADS = 4
RET_QK_DIM = 256
RET_V_DIM = 512
FFN_DIM = 2816
CONV_WIDTH = 3
ROPE_THETA = 10000.0
NORM_EPS = 1e-6
NEG_INF = -1e30

LANES = 128
SUBLANES = 8
MIX = NA_HEADS * HEAD_DIM
RET_Q = RET_HEADS * RET_QK_DIM
RET_V = RET_HEADS * RET_V_DIM
RET_IN = 2 * RET_Q + 2 * RET_V
RET_BLOCK = 256
FFN_CHUNK = 256
N_FFN_CHUNKS = FFN_DIM // FFN_CHUNK
TOKEN_TILE = 1024
OUT_TILE = 1024
FFN_TILE = 512
ATTN_GROUP = 4
NA_GROUP = 4
HALO = SUBLANES
RESIDUES = max(d for _, d in DIL_BRANCHES)
SLAB_PITCH = RESIDUES + 4
VMEM_LIMIT = 56 * 1024 * 1024
LOG2_E = 1.4426950408889634
GELU_C0 = 0.7978845608028654
GELU_C1 = GELU_C0 * 0.044715

F32 = jnp.float32
BF16 = jnp.bfloat16
NT_DIMS = (((1,), (1,)), ((), ()))
TN_DIMS = (((0,), (0,)), ((), ()))


def _params(sem, vmem=VMEM_LIMIT):
    return pltpu.CompilerParams(dimension_semantics=sem, vmem_limit_bytes=vmem)


def _resident(shape):
    nd = len(shape)
    return pl.BlockSpec(shape, lambda *_: (0,) * nd, pipeline_mode=pl.Buffered(1))


def _rms(x, g):
    ms = jnp.mean(x * x, axis=-1, keepdims=True)
    return x * lax.rsqrt(ms + NORM_EPS) * g


def _software_pipeline(n_groups, scores, softmax, values):
    assert n_groups % 2 == 0 and n_groups >= 4
    scores(0, 0)
    scores(1, 1)
    softmax(0, 0)

    def body(gg, carry):
        g = 2 * gg
        scores(g + 2, 0)
        softmax(g + 1, 1)
        values(g, 0)
        scores(g + 3, 1)
        softmax(g + 2, 0)
        values(g + 1, 1)
        return carry

    lax.fori_loop(0, n_groups // 2 - 1, body, 0)
    softmax(n_groups - 1, 1)
    values(n_groups - 2, 0)
    values(n_groups - 1, 1)


def _even_proj_kernel(x_ref, g_ref, w_ref, cos_ref, sin_ref, na_ref, perm_ref, slab_ref, *, tm):
    h = _rms(x_ref[...], g_ref[...]).astype(BF16)
    per_res = tm // RESIDUES
    tiles = MIX // LANES
    for c in (3, 4, 5, 0, 1, 2):
        a = jnp.dot(h, w_ref[:, c * MIX:(c + 1) * MIX], preferred_element_type=F32)
        if c in (3, 4):
            cos, sin = cos_ref[...], sin_ref[...]
            parts = []
            for j in range(tiles):
                aj = a[:, j * LANES:(j + 1) * LANES]
                parts.append(aj * cos + pltpu.roll(aj, LANES // 2, axis=1) * sin)
            a = jnp.concatenate(parts, axis=1)
        if c < 3:
            na_ref[:, c * MIX:(c + 1) * MIX] = a.astype(BF16)
        else:
            slab = slab_ref.at[c % 2]
            for j in range(tiles):
                for m in range(per_res):
                    slab[j, m * SLAB_PITCH:m * SLAB_PITCH + RESIDUES, :] = (
                        a[m * RESIDUES:(m + 1) * RESIDUES, j * LANES:(j + 1) * LANES])
            for r in range(RESIDUES):
                for j in range(tiles):
                    col = (c - 3) * MIX + j * LANES
                    rows = slab[j, pl.ds(r, per_res, stride=SLAB_PITCH), :]
                    perm_ref[r, :, col:col + LANES] = rows.astype(BF16)


def _even_proj(x2d, g, w, cos, sin, seq):
    n = x2d.shape[0]
    b = n // seq
    tm = TOKEN_TILE
    tps = seq // tm
    per_res = tm // RESIDUES
    return pl.pallas_call(
        functools.partial(_even_proj_kernel, tm=tm),
        grid=(n // tm,),
        in_specs=[
            pl.BlockSpec((tm, D_MODEL), lambda i: (i, 0)),
            _resident((1, D_MODEL)),
            _resident((D_MODEL, 6 * MIX)),
            pl.BlockSpec((tm, LANES), lambda i: (i % tps, 0)),
            pl.BlockSpec((tm, LANES), lambda i: (i % tps, 0)),
        ],
        out_specs=[pl.BlockSpec((tm, 3 * MIX), lambda i: (i, 0)),
                   pl.BlockSpec((None, RESIDUES, per_res, 3 * MIX), lambda i: (i // tps, 0, i % tps, 0))],
        out_shape=[jax.ShapeDtypeStruct((n, 3 * MIX), BF16),
                   jax.ShapeDtypeStruct((b, RESIDUES, seq // RESIDUES, 3 * MIX), BF16)],
        scratch_shapes=[pltpu.VMEM((2, MIX // LANES, per_res * SLAB_PITCH, LANES), F32)],
        compiler_params=_params(("parallel",)),
        name="even_proj",
    )(x2d, g, w, cos, sin)


def _na_kernel(q_ref, k_ref, v_ref, b_ref, o_ref, s0_ref, s1_ref, p0_ref, p1_ref, *, rows):
    lane = lax.broadcasted_iota(jnp.int32, (GRID_W, LANES), 1)
    is_a = lane < HEAD_DIM
    win = NA_WIN_R * GRID_W
    s_bufs, p_bufs = (s0_ref, s1_ref), (p0_ref, p1_ref)

    def window_start(r):
        rs = jnp.clip(r - NA_WIN_R // 2, 0, rows - NA_WIN_R)
        return pl.multiple_of(rs * GRID_W, GRID_W)

    def scores(g, slot):
        for u in range(NA_GROUP):
            r = jnp.int32(g * NA_GROUP + u)
            pat = jnp.where(r < NA_WIN_R // 2, r,
                            jnp.where(r > rows - NA_WIN_R // 2, r - (rows - NA_WIN_R), NA_WIN_R // 2))
            q = q_ref[pl.ds(pl.multiple_of(r * GRID_W, GRID_W), GRID_W), :]
            zero = jnp.zeros_like(q)
            q2 = jnp.concatenate([jnp.where(is_a, q, zero), jnp.where(is_a, zero, q)], axis=0)
            kw = k_ref[pl.ds(window_start(r), win), :]
            s = lax.dot_general(q2, kw, NT_DIMS, preferred_element_type=F32)
            s_bufs[slot][u] = s + b_ref[pat]

    def softmax(g, slot):
        for u in range(NA_GROUP):
            s = s_bufs[slot][u]
            m = jnp.max(s, axis=1, keepdims=True)
            p_bufs[slot][u] = jnp.exp2(s - m).astype(BF16)

    ones = jnp.ones((win, LANES), BF16)

    def values(g, slot):
        for u in range(NA_GROUP):
            r = jnp.int32(g * NA_GROUP + u)
            vw = jnp.concatenate([v_ref[pl.ds(window_start(r), win), :], ones], axis=1)
            pv = jnp.dot(p_bufs[slot][u], vw, preferred_element_type=F32)
            pv = pv[:, :LANES] / pv[:, LANES:]
            o_ref[pl.ds(pl.multiple_of(r * GRID_W, GRID_W), GRID_W), :] = (
                jnp.where(is_a, pv[:GRID_W], pv[GRID_W:]).astype(BF16))

    _software_pipeline(rows // NA_GROUP, scores, softmax, values)


def _na_attention(qkv, bias, seq):
    b = qkv.shape[0]
    rows = seq // GRID_W
    npair = NA_HEADS // 2
    win = NA_WIN_R * GRID_W
    blk = lambda off: pl.BlockSpec((None, seq, LANES), lambda p, i: (i, 0, off + p))
    s_buf = pltpu.VMEM((NA_GROUP, 2 * GRID_W, win), F32)
    p_buf = pltpu.VMEM((NA_GROUP, 2 * GRID_W, win), BF16)
    return pl.pallas_call(
        functools.partial(_na_kernel, rows=rows),
        grid=(npair, b),
        in_specs=[blk(0), blk(npair), blk(2 * npair),
                  pl.BlockSpec((NA_WIN_R, None, 2 * GRID_W, win), lambda p, i: (0, p, 0, 0))],
        out_specs=pl.BlockSpec((None, seq, LANES), lambda p, i: (i, 0, p)),
        out_shape=jax.ShapeDtypeStruct((b, seq, MIX), BF16),
        scratch_shapes=[s_buf, s_buf, p_buf, p_buf],
        compiler_params=_params(("parallel", "parallel")),
        name="na_attention",
    )(qkv, qkv, qkv, bias)


def _na_bias_table(rpb, rows):
    c = jnp.arange(GRID_W)
    cs = jnp.clip(c - NA_WIN_C // 2, 0, GRID_W - NA_WIN_C)
    j = jnp.arange(GRID_W)
    valid = (j[None, :] >= cs[:, None]) & (j[None, :] < cs[:, None] + NA_WIN_C)
    k = jnp.arange(2 * NA_WIN_C - 1)
    pick = (k[None, None, :] == (j[None, :, None] - c[:, None, None] + NA_WIN_C - 1)).astype(F32)
    full = jnp.einsum("hrk,cjk->hcrj", rpb.astype(F32), pick, precision=lax.Precision.HIGHEST)
    half = NA_WIN_R // 2
    reps = list(range(half)) + [half] + list(range(rows - half + 1, rows))
    tabs = []
    for r in reps:
        rs = min(max(r - half, 0), rows - NA_WIN_R)
        first = rs - r + NA_WIN_R - 1
        t = jnp.where(valid[None, :, None, :], full[:, :, first:first + NA_WIN_R, :] * LOG2_E, NEG_INF)
        tabs.append(t.reshape(NA_HEADS // 2, 2 * GRID_W, NA_WIN_R * GRID_W))
    return jnp.stack(tabs, axis=0)


def _dil_kernel(q_ref, k_ref, v_ref, o_ref, qf_ref, kf_ref, vf_ref, acc_ref, m_ref, l_ref, nat_ref, mask_ref,
                s0_ref, s1_ref, p0_ref, p1_ref, m0_ref, m1_ref, *, slab_len):
    qb, kw_rows = DIL_QBLOCK, 2 * DIL_QBLOCK
    s_bufs, p_bufs, m_bufs = (s0_ref, s1_ref), (p0_ref, p1_ref), (m0_ref, m1_ref)
    ones = jnp.ones((kw_rows, LANES), BF16)
    lane = lax.broadcasted_iota(jnp.int32, (qb, LANES), 1)
    is_a_out = lane < HEAD_DIM
    is_a_rot = (lane & (HEAD_DIM - 1)) < HEAD_DIM // 2
    row = lax.broadcasted_iota(jnp.int32, (2 * qb, kw_rows), 0) & (qb - 1)
    col = lax.broadcasted_iota(jnp.int32, (2 * qb, kw_rows), 1)

    qf_ref[...] = q_ref[...].astype(F32)
    kf_ref[...] = k_ref[...].astype(F32)
    vf_ref[...] = v_ref[...].astype(F32)

    branches = sorted(DIL_BRANCHES, key=lambda wd: -wd[1])
    for bi, (window, d) in enumerate(branches):
        g_cnt = RESIDUES // d
        rq, rk = qb // g_cnt, kw_rows // g_cnt
        base = g_cnt * ((col % rk) - (row % rq)) + (col // rk) - (row // rq)
        for pat, off in enumerate((0, (rq - rk) // 2, rq - rk)):
            on_band = jnp.abs(base + g_cnt * off) <= (window // 2) // d
            mask_ref[3 * bi + pat] = jnp.where(on_band, jnp.inf, NEG_INF)

    for bi, (window, d) in enumerate(branches):
        g_cnt = RESIDUES // d
        rq, rk = qb // g_cnt, kw_rows // g_cnt
        lead = (rk - rq) // 2
        nblk = slab_len // rq

        def tile(g, u, d=d, g_cnt=g_cnt, rq=rq, rk=rk, lead=lead, nblk=nblk):
            it = jnp.int32(g * ATTN_GROUP + u)
            i = it % nblk
            q0 = i * rq
            ws = jnp.clip(q0 - lead, 0, slab_len - rk)
            pat = jnp.where(i == 0, 0, jnp.where(i == nblk - 1, 2, 1))
            slabs = [it // nblk + d * g_ for g_ in range(g_cnt)]
            return slabs, q0, pl.multiple_of(q0, rq), ws, pat

        def window(ref, ref32, slabs, ws, rq=rq, rk=rk):
            if rq % 16 == 0:
                wsa = pl.multiple_of(ws, 16)
                return jnp.concatenate([ref[s_, pl.ds(wsa, rk), :] for s_ in slabs], axis=0)
            return jnp.concatenate([ref32[s_, pl.ds(ws, rk), :] for s_ in slabs], axis=0).astype(BF16)

        def scores(g, slot, bi=bi, rq=rq, tile=tile, window=window):
            for u in range(ATTN_GROUP):
                slabs, _, q0a, ws, pat = tile(g, u)
                if rq % 16 == 0:
                    q = jnp.concatenate([q_ref[s_, pl.ds(q0a, rq), :] for s_ in slabs], axis=0)
                else:
                    q = jnp.concatenate([qf_ref[s_, pl.ds(q0a, rq), :] for s_ in slabs], axis=0).astype(BF16)
                kw = window(k_ref, kf_ref, slabs, ws)
                zero = jnp.zeros_like(q)
                q2 = jnp.concatenate([jnp.where(is_a_rot, q, zero), jnp.where(is_a_rot, zero, q)], axis=0)
                s = lax.dot_general(q2, kw, NT_DIMS, preferred_element_type=F32)
                s_bufs[slot][u] = jnp.minimum(s, mask_ref[3 * bi + pat])

        def softmax(g, slot):
            for u in range(ATTN_GROUP):
                s = s_bufs[slot][u]
                m = jnp.max(s, axis=1, keepdims=True)
                p_bufs[slot][u] = jnp.exp2(s - m).astype(BF16)
                m_bufs[slot][u] = jnp.where(is_a_out, m[:qb], m[qb:])

        def values(g, slot, bi=bi, rq=rq, tile=tile, window=window):
            for u in range(ATTN_GROUP):
                slabs, q0, q0a, ws, _ = tile(g, u)
                vw = jnp.concatenate([window(v_ref, vf_ref, slabs, ws), ones], axis=1)
                pv = jnp.dot(p_bufs[slot][u], vw, preferred_element_type=F32)
                pv_t = jnp.where(is_a_out, pv[:qb, :LANES], pv[qb:, :LANES])
                l_t = jnp.where(is_a_out, pv[:qb, LANES:], pv[qb:, LANES:])
                m_t = m_bufs[slot][u]
                if bi > 0:
                    m_old = jnp.concatenate([m_ref[s_, pl.ds(q0a, rq), :] for s_ in slabs], axis=0)
                    l_old = jnp.concatenate([l_ref[s_, pl.ds(q0a, rq), :] for s_ in slabs], axis=0)
                    a_old = jnp.concatenate([acc_ref[s_, pl.ds(q0a, rq), :] for s_ in slabs], axis=0)
                    m_new = jnp.maximum(m_old, m_t)
                    w_old = jnp.exp2(m_old - m_new)
                    w_new = jnp.exp2(m_t - m_new)
                    l_t = l_old * w_old + l_t * w_new
                    pv_t = a_old * w_old + pv_t * w_new
                    m_t = m_new
                if bi < len(branches) - 1:
                    for g_, s_ in enumerate(slabs):
                        rows = slice(g_ * rq, (g_ + 1) * rq)
                        m_ref[s_, pl.ds(q0a, rq), :] = m_t[rows]
                        l_ref[s_, pl.ds(q0a, rq), :] = l_t[rows]
                        acc_ref[s_, pl.ds(q0a, rq), :] = pv_t[rows]
                else:
                    out = pv_t / l_t
                    t0 = q0 * RESIDUES
                    for g_, s_ in enumerate(slabs):
                        nat_ref[pl.ds(t0 + s_, rq, stride=RESIDUES), :] = out[g_ * rq:(g_ + 1) * rq]
                    t0a = pl.multiple_of(t0, qb)
                    o_ref[pl.ds(t0a, qb), :] = nat_ref[pl.ds(t0a, qb), :].astype(BF16)

        _software_pipeline(d * nblk // ATTN_GROUP, scores, softmax, values)


def _dil_attention(perm, seq):
    b = perm.shape[0]
    slab_len = seq // RESIDUES
    npair = DIL_HEADS // 2
    blk = lambda off: pl.BlockSpec((None, RESIDUES, slab_len, LANES), lambda i, p: (i, 0, 0, off + p))
    slab = pltpu.VMEM((RESIDUES, slab_len, LANES), F32)
    s_buf = pltpu.VMEM((ATTN_GROUP, 2 * DIL_QBLOCK, 2 * DIL_QBLOCK), F32)
    p_buf = pltpu.VMEM((ATTN_GROUP, 2 * DIL_QBLOCK, 2 * DIL_QBLOCK), BF16)
    stat = pltpu.VMEM((ATTN_GROUP, DIL_QBLOCK, LANES), F32)
    return pl.pallas_call(
        functools.partial(_dil_kernel, slab_len=slab_len),
        grid=(b, npair),
        in_specs=[blk(0), blk(npair), blk(2 * npair)],
        out_specs=pl.BlockSpec((None, seq, LANES), lambda i, p: (i, 0, p)),
        out_shape=jax.ShapeDtypeStruct((b, seq, MIX), BF16),
        scratch_shapes=[slab, slab, slab, slab, slab, slab,
                        pltpu.VMEM((seq, LANES), F32),
                        pltpu.VMEM((3 * len(DIL_BRANCHES), 2 * DIL_QBLOCK, 2 * DIL_QBLOCK), F32),
                        s_buf, s_buf, p_buf, p_buf, stat, stat],
        compiler_params=_params(("parallel", "parallel")),
        name="dilated_attention",
    )(perm, perm, perm)


def _even_out_kernel(x_ref, oa_ref, ob_ref, w_ref, y_ref):
    o = jnp.concatenate([oa_ref[...], ob_ref[...]], axis=1)
    y_ref[...] = x_ref[...] + jnp.dot(o, w_ref[...], preferred_element_type=F32)


def _even_out(x2d, oa, ob, w):
    n = x2d.shape[0]
    tm = OUT_TILE
    row = lambda width: pl.BlockSpec((tm, width), lambda i: (i, 0))
    return pl.pallas_call(
        _even_out_kernel,
        grid=(n // tm,),
        in_specs=[row(D_MODEL), row(MIX), row(MIX), _resident((2 * MIX, D_MODEL))],
        out_specs=row(D_MODEL),
        out_shape=jax.ShapeDtypeStruct((n, D_MODEL), F32),
        compiler_params=_params(("parallel",)),
        name="even_out",
    )(x2d, oa, ob, w)


def _ffn_kernel(x_ref, xn_ref, xp_ref, g_ref, wu_ref, cp_ref, wd_ref, fg_ref, o_ref, h_ref, a_ref, act_ref, *,
                tm, tiles_per_seq, final):
    pos = pl.program_id(0) % tiles_per_seq
    g = g_ref[...]
    keep_prev = jnp.where(pos == 0, 0.0, 1.0)
    keep_next = jnp.where(pos == tiles_per_seq - 1, 0.0, 1.0)
    h_ref[0:tm, :] = _rms(x_ref[...], g).astype(BF16)
    halo = jnp.concatenate([_rms(xn_ref[...], g) * keep_next, _rms(xp_ref[...], g) * keep_prev], axis=0)
    h_ref[tm:tm + 2 * HALO, :] = halo.astype(BF16)
    tiles = FFN_CHUNK // LANES
    for c in range(N_FFN_CHUNKS):
        buf = a_ref.at[c % 2]
        ys = []
        for part in range(2):
            first = part * FFN_DIM + c * FFN_CHUNK
            a = jnp.dot(h_ref[...], wu_ref[:, first:first + FFN_CHUNK], preferred_element_type=F32)
            for j in range(tiles):
                cols = slice(j * LANES, (j + 1) * LANES)
                buf[part * tiles + j, HALO:tm + 2 * HALO, :] = a[:tm + HALO, cols]
                buf[part * tiles + j, 0:HALO, :] = a[tm + HALO:, cols]
        for part in range(2):
            for j in range(tiles):
                slab = buf.at[part * tiles + j]
                first = part * FFN_DIM + c * FFN_CHUNK + j * LANES
                cols = slice(first, first + LANES)
                y = cp_ref[3:4, cols] + slab[HALO - 1:HALO - 1 + tm, :] * cp_ref[0:1, cols]
                y = y + slab[HALO:HALO + tm, :] * cp_ref[1:2, cols]
                y = y + slab[HALO + 1:HALO + 1 + tm, :] * cp_ref[2:3, cols]
                ys.append(y)
        half_u = jnp.concatenate(ys[:tiles], axis=1)
        gate = jnp.concatenate(ys[tiles:], axis=1)
        t = jnp.tanh(gate * (GELU_C0 + GELU_C1 * (gate * gate)))
        w = half_u * gate
        act_ref[:, c * FFN_CHUNK:(c + 1) * FFN_CHUNK] = (w + w * t).astype(BF16)
    out = x_ref[...] + jnp.dot(act_ref[...], wd_ref[...], preferred_element_type=F32)
    if final:
        out = _rms(out, fg_ref[...])
    o_ref[...] = out


def _ffn(x2d, g, wu, cp, wd, fg, seq, final):
    n = x2d.shape[0]
    tm = FFN_TILE
    tps = seq // tm
    hb = tm // HALO
    last = n // HALO - 1
    return pl.pallas_call(
        functools.partial(_ffn_kernel, tm=tm, tiles_per_seq=tps, final=final),
        grid=(n // tm,),
        in_specs=[
            pl.BlockSpec((tm, D_MODEL), lambda i: (i, 0)),
            pl.BlockSpec((HALO, D_MODEL), lambda i: (jnp.minimum((i + 1) * hb, last), 0)),
            pl.BlockSpec((HALO, D_MODEL), lambda i: (jnp.maximum(i * hb - 1, 0), 0)),
            _resident((1, D_MODEL)),
            _resident((D_MODEL, 2 * FFN_DIM)),
            _resident((SUBLANES, 2 * FFN_DIM)),
            _resident((FFN_DIM, D_MODEL)),
            _resident((1, D_MODEL)),
        ],
        out_specs=pl.BlockSpec((tm, D_MODEL), lambda i: (i, 0)),
        out_shape=jax.ShapeDtypeStruct((n, D_MODEL), F32),
        scratch_shapes=[pltpu.VMEM((tm + 2 * HALO, D_MODEL), BF16),
                        pltpu.VMEM((2, 2 * FFN_CHUNK // LANES, tm + 2 * HALO, LANES), F32),
                        pltpu.VMEM((tm, FFN_DIM), BF16)],
        compiler_params=_params(("parallel",)),
        name="conv_ffn",
    )(x2d, x2d, x2d, g, wu, cp, wd, fg)


def _ret_proj_kernel(x_ref, g_ref, w_ref, cos_ref, sin_ref, o_ref):
    h = _rms(x_ref[...], g_ref[...]).astype(BF16)
    for c in range(RET_IN // MIX):
        a = jnp.dot(h, w_ref[:, c * MIX:(c + 1) * MIX], preferred_element_type=F32)
        if c < 2 * RET_Q // MIX:
            cos, sin = cos_ref[...], sin_ref[...]
            parts = []
            for j in range(MIX // RET_QK_DIM):
                x1 = a[:, j * RET_QK_DIM:j * RET_QK_DIM + LANES]
                x2 = a[:, j * RET_QK_DIM + LANES:(j + 1) * RET_QK_DIM]
                parts += [x1 * cos - x2 * sin, x1 * sin + x2 * cos]
            a = jnp.concatenate(parts, axis=1)
        o_ref[:, c * MIX:(c + 1) * MIX] = a.astype(BF16)


def _ret_proj(x2d, g, w, cos, sin, seq):
    n = x2d.shape[0]
    tm = TOKEN_TILE
    tps = seq // tm
    return pl.pallas_call(
        _ret_proj_kernel,
        grid=(n // tm,),
        in_specs=[
            pl.BlockSpec((tm, D_MODEL), lambda i: (i, 0)),
            _resident((1, D_MODEL)),
            _resident((D_MODEL, RET_IN)),
            pl.BlockSpec((tm, LANES), lambda i: (i % tps, 0)),
            pl.BlockSpec((tm, LANES), lambda i: (i % tps, 0)),
        ],
        out_specs=pl.BlockSpec((tm, RET_IN), lambda i: (i, 0)),
        out_shape=jax.ShapeDtypeStruct((n, RET_IN), BF16),
        compiler_params=_params(("parallel",)),
        name="ret_proj",
    )(x2d, g, w, cos, sin)


def _ret_kernel(lg_ref, q_ref, k_ref, v_ref, out_ref, o_ref, sf_ref, sb_ref, d_ref, *, seq):
    c = RET_BLOCK
    n = seq // c
    hd = pl.program_id(1)
    lgf = lg_ref[0, hd]
    lgb = lg_ref[1, hd]
    pr = lax.broadcasted_iota(jnp.int32, (c, c), 0).astype(F32)
    pc = lax.broadcasted_iota(jnp.int32, (c, c), 1).astype(F32)
    diff = pr - pc
    lower = diff >= 0
    d_f = jnp.where(lower, jnp.exp(lgf * jnp.where(lower, diff, 0.0)), 0.0)
    d_b = jnp.where(lower, 0.0, jnp.exp(lgb * jnp.where(lower, 0.0, -diff)))
    d_ref[...] = d_f + d_b
    pos = lax.broadcasted_iota(jnp.int32, (c, 1), 0).astype(F32)
    qd_f = jnp.exp(lgf * (pos + 1.0))
    kd_f = jnp.exp(lgf * (c - 1.0 - pos))
    qd_b = jnp.exp(lgb * (c - pos))
    kd_b = jnp.exp(lgb * pos)
    ones = jnp.ones((1, LANES), F32)
    sd_f = jnp.exp(lgf * c * ones)[:, :1]
    sd_b = jnp.exp(lgb * c * ones)[:, :1]

    def emit(r0, o, accumulate):
        if not accumulate:
            o_ref[pl.ds(r0, c), :] = o
            return
        out_ref[pl.ds(r0, c), :] = (o_ref[pl.ds(r0, c), :] + o).astype(BF16)

    def fwd_part(i, accumulate):
        r0 = pl.multiple_of(i * c, c)
        q = q_ref[pl.ds(r0, c), :]
        k = k_ref[pl.ds(r0, c), :]
        v = v_ref[pl.ds(r0, c), :]
        s = lax.dot_general(q, k, NT_DIMS, preferred_element_type=F32)
        inner = (s * d_ref[...]).astype(BF16)
        qd = (q.astype(F32) * qd_f).astype(BF16)
        o = (jnp.dot(inner, v, preferred_element_type=F32)
             + jnp.dot(qd, sf_ref[...].astype(BF16), preferred_element_type=F32))
        emit(r0, o, accumulate)
        kd = (k.astype(F32) * kd_f).astype(BF16)
        sf_ref[...] = sf_ref[...] * sd_f + lax.dot_general(kd, v, TN_DIMS, preferred_element_type=F32)

    def bwd_part(i, accumulate):
        r0 = pl.multiple_of(i * c, c)
        q = q_ref[pl.ds(r0, c), :]
        k = k_ref[pl.ds(r0, c), :]
        v = v_ref[pl.ds(r0, c), :]
        qd = (q.astype(F32) * qd_b).astype(BF16)
        o = jnp.dot(qd, sb_ref[...].astype(BF16), preferred_element_type=F32)
        emit(r0, o, accumulate)
        kd = (k.astype(F32) * kd_b).astype(BF16)
        sb_ref[...] = sb_ref[...] * sd_b + lax.dot_general(kd, v, TN_DIMS, preferred_element_type=F32)

    sf_ref[...] = jnp.zeros_like(sf_ref)
    sb_ref[...] = jnp.zeros_like(sb_ref)

    def first_half(j, carry):
        fwd_part(j, False)
        bwd_part(n - 1 - j, False)
        return carry

    def second_half(j, carry):
        fwd_part(j, True)
        bwd_part(n - 1 - j, True)
        return carry

    lax.fori_loop(0, n // 2, first_half, 0, unroll=4)
    lax.fori_loop(n // 2, n, second_half, 0, unroll=4)


def _retention(qkv, log_gamma, seq):
    b = qkv.shape[0]
    kblk = RET_Q // RET_QK_DIM
    vblk = 2 * RET_Q // RET_V_DIM
    state = pltpu.VMEM((RET_QK_DIM, RET_V_DIM), F32)
    return pl.pallas_call(
        functools.partial(_ret_kernel, seq=seq),
        grid=(b, RET_HEADS),
        in_specs=[
            pl.BlockSpec(memory_space=pltpu.SMEM),
            pl.BlockSpec((None, seq, RET_QK_DIM), lambda i, h: (i, 0, h)),
            pl.BlockSpec((None, seq, RET_QK_DIM), lambda i, h: (i, 0, kblk + h)),
            pl.BlockSpec((None, seq, RET_V_DIM), lambda i, h: (i, 0, vblk + h)),
        ],
        out_specs=pl.BlockSpec((None, seq, RET_V_DIM), lambda i, h: (i, 0, h)),
        out_shape=jax.ShapeDtypeStruct((b, seq, RET_V), BF16),
        scratch_shapes=[pltpu.VMEM((seq, RET_V_DIM), F32), state, state,
                        pltpu.VMEM((RET_BLOCK, RET_BLOCK), F32)],
        compiler_params=_params(("parallel", "parallel")),
        name="retention",
    )(log_gamma, qkv, qkv, qkv)


def _ret_out_kernel(x_ref, r_ref, g_ref, w_ref, y_ref, *, tm):
    for r0 in range(0, tm, tm // 2):
        rows = slice(r0, r0 + tm // 2)
        parts = []
        for hd in range(RET_HEADS):
            cols = slice(hd * RET_V_DIM, (hd + 1) * RET_V_DIM)
            r = r_ref[rows, cols].astype(F32)
            mu = jnp.mean(r, axis=-1, keepdims=True)
            var = jnp.mean(jnp.square(r - mu), axis=-1, keepdims=True)
            rn = (r - mu) * lax.rsqrt(var + NORM_EPS)
            half_gate = g_ref[rows, cols].astype(F32) * 0.5
            parts.append(((half_gate + half_gate * jnp.tanh(half_gate)) * rn).astype(BF16))
        act = jnp.concatenate(parts, axis=1)
        y_ref[rows, :] = x_ref[rows, :] + jnp.dot(act, w_ref[...], preferred_element_type=F32)


def _ret_out(x2d, r, proj, w):
    n = x2d.shape[0]
    tm = OUT_TILE
    row = lambda width: pl.BlockSpec((tm, width), lambda i: (i, 0))
    gate_blk = (2 * RET_Q + RET_V) // RET_V
    return pl.pallas_call(
        functools.partial(_ret_out_kernel, tm=tm),
        grid=(n // tm,),
        in_specs=[row(D_MODEL), row(RET_V), pl.BlockSpec((tm, RET_V), lambda i: (i, gate_blk)),
                  _resident((RET_V, D_MODEL))],
        out_specs=row(D_MODEL),
        out_shape=jax.ShapeDtypeStruct((n, D_MODEL), F32),
        compiler_params=_params(("parallel",)),
        name="ret_out",
    )(x2d, r, proj, w)


def _rope_tables(seq, dh, lane_freq, sign):
    inv = 1.0 / (ROPE_THETA ** (jnp.arange(0, dh, 2, dtype=F32) / dh))
    ang = jnp.arange(seq, dtype=F32)[:, None] * inv[None, :]
    return jnp.cos(ang)[:, lane_freq], jnp.sin(ang)[:, lane_freq] * sign[None, :]


def _pair_rotary_layout(w):
    rows = w.shape[0]
    w = w.reshape(rows, DIL_HEADS // 2, 2, 2, HEAD_DIM // 2)
    return w.transpose(0, 1, 3, 2, 4).reshape(rows, MIX)


def _prepare(attn_norm, even_w_in, na_rpb, even_w_out, ret_w_in, ret_decay_fwd, ret_decay_bwd, ret_w_out,
             ffn_norm, ffn_w_up, ffn_conv_w, ffn_conv_b, ffn_w_down, final_norm, seq):
    lane = jnp.arange(LANES)
    half = HEAD_DIM // 2
    w_in = even_w_in[0]
    blocks = [w_in[:, i * MIX:(i + 1) * MIX] for i in range(6)]
    scale = HEAD_DIM ** -0.5 * LOG2_E
    blocks[0] = blocks[0] * scale
    blocks[3] = _pair_rotary_layout(blocks[3]) * scale
    blocks[4] = _pair_rotary_layout(blocks[4])
    p = {}
    p["even_w_in"] = jnp.concatenate(blocks, axis=1).astype(BF16)
    p["even_cos"], p["even_sin"] = _rope_tables(seq, HEAD_DIM, lane % half,
                                                jnp.where(lane < LANES // 2, -1.0, 1.0))
    p["na_bias"] = _na_bias_table(na_rpb[0], seq // GRID_W)
    p["even_w_out"] = even_w_out[0].astype(BF16)

    rw = ret_w_in[0]
    rscale = RET_QK_DIM ** -0.5
    p["ret_w_in"] = jnp.concatenate([rw[:, :RET_Q] * rscale, rw[:, RET_Q:]], axis=1).astype(BF16)
    p["ret_cos"], p["ret_sin"] = _rope_tables(seq, RET_QK_DIM, lane, jnp.ones((LANES,), F32))
    p["ret_log_gamma"] = jnp.stack([-jax.nn.softplus(ret_decay_fwd[0].astype(F32)),
                                    -jax.nn.softplus(ret_decay_bwd[0].astype(F32))], axis=0)
    p["ret_w_out"] = ret_w_out[0].astype(BF16)

    u_half = jnp.concatenate([jnp.full((FFN_DIM,), 0.5, F32), jnp.ones((FFN_DIM,), F32)])
    p["ffn_w_up"], p["ffn_conv"] = [], []
    for layer in range(2):
        p["ffn_w_up"].append(ffn_w_up[layer].astype(BF16))
        taps = jnp.concatenate([ffn_conv_w[layer], ffn_conv_b[layer][None, :]], axis=0) * u_half[None, :]
        p["ffn_conv"].append(
            jnp.concatenate([taps, jnp.zeros((SUBLANES - CONV_WIDTH - 1, 2 * FFN_DIM), F32)], axis=0))
    p["ffn_w_down"] = [ffn_w_down[layer].astype(BF16) for layer in range(2)]
    p["attn_norm"] = [attn_norm[layer][None, :] for layer in range(2)]
    p["ffn_norm"] = [ffn_norm[layer][None, :] for layer in range(2)]
    p["final_norm"] = final_norm[None, :]
    return p


def _trunk(x, p):
    b, seq, _ = x.shape
    x2d = x.reshape(b * seq, D_MODEL)
    na_qkv, dil_qkv = _even_proj(x2d, p["attn_norm"][0], p["even_w_in"], p["even_cos"], p["even_sin"], seq)
    oa = _na_attention(na_qkv.reshape(b, seq, 3 * MIX), p["na_bias"], seq).reshape(b * seq, MIX)
    ob = _dil_attention(dil_qkv, seq).reshape(b * seq, MIX)
    x2d = _even_out(x2d, oa, ob, p["even_w_out"])
    x2d = _ffn(x2d, p["ffn_norm"][0], p["ffn_w_up"][0], p["ffn_conv"][0], p["ffn_w_down"][0],
               p["final_norm"], seq, final=False)
    rproj = _ret_proj(x2d, p["attn_norm"][1], p["ret_w_in"], p["ret_cos"], p["ret_sin"], seq)
    r = _retention(rproj.reshape(b, seq, RET_IN), p["ret_log_gamma"], seq)
    x2d = _ret_out(x2d, r.reshape(b * seq, RET_V), rproj, p["ret_w_out"])
    x2d = _ffn(x2d, p["ffn_norm"][1], p["ffn_w_up"][1], p["ffn_conv"][1], p["ffn_w_down"][1],
               p["final_norm"], seq, final=True)
    return x2d.reshape(b, seq, D_MODEL)


def kernel(x_prompt, x_sample, attn_norm, even_w_in, na_rpb, even_w_out, ret_w_in, ret_decay_fwd, ret_decay_bwd,
           ret_w_out, ffn_norm, ffn_w_up, ffn_conv_w, ffn_conv_b, ffn_w_down, final_norm):
    assert x_prompt.shape[1] == x_sample.shape[1]
    p = _prepare(attn_norm, even_w_in, na_rpb, even_w_out, ret_w_in, ret_decay_fwd, ret_decay_bwd, ret_w_out,
                 ffn_norm, ffn_w_up, ffn_conv_w, ffn_conv_b, ffn_w_down, final_norm, x_prompt.shape[1])
    return _trunk(x_prompt, p), _trunk(x_sample, p)
```

```python
import functools

import jax
import jax.numpy as jnp
from jax import lax
from jax.experimental import pallas as pl
from jax.experimental.pallas import tpu as pltpu

D_MODEL = 1024
GRID_W = 64
HEAD_DIM = 64
NA_HEADS = 8
NA_WIN_R = 8
NA_WIN_C = 16
DIL_HEADS = 8
DIL_BRANCHES = ((128, 1), (512, 4), (2048, 16))
DIL_QBLOCK = 128
RET_HEADS = 4
RET_QK_DIM = 256
RET_V_DIM = 512
FFN_DIM = 2816
CONV_WIDTH = 3
ROPE_THETA = 10000.0
NORM_EPS = 1e-6
NEG_INF = -1e30

LANES = 128
SUBLANES = 8
MIX = NA_HEADS * HEAD_DIM
RET_Q = RET_HEADS * RET_QK_DIM
RET_V = RET_HEADS * RET_V_DIM
RET_IN = 2 * RET_Q + 2 * RET_V
RET_BLOCK = 256
FFN_CHUNK = 256
N_FFN_CHUNKS = FFN_DIM // FFN_CHUNK
TOKEN_TILE = 1024
OUT_TILE = 1024
FFN_TILE = 512
ATTN_GROUP = 4
NA_GROUP = 4
HALO = SUBLANES
RESIDUES = max(d for _, d in DIL_BRANCHES)
SLAB_PITCH = RESIDUES + 4
VMEM_LIMIT = 56 * 1024 * 1024
LOG2_E = 1.4426950408889634
GELU_C0 = 0.7978845608028654
GELU_C1 = GELU_C0 * 0.044715

F32 = jnp.float32
BF16 = jnp.bfloat16
NT_DIMS = (((1,), (1,)), ((), ()))
TN_DIMS = (((0,), (0,)), ((), ()))


def _params(sem, vmem=VMEM_LIMIT):
    return pltpu.CompilerParams(dimension_semantics=sem, vmem_limit_bytes=vmem)


def _resident(shape):
    nd = len(shape)
    return pl.BlockSpec(shape, lambda *_: (0,) * nd, pipeline_mode=pl.Buffered(1))


def _rms(x, g):
    ms = jnp.mean(x * x, axis=-1, keepdims=True)
    return x * lax.rsqrt(ms + NORM_EPS) * g


def _software_pipeline(n_groups, scores, softmax, values):
    assert n_groups % 2 == 0 and n_groups >= 4
    scores(0, 0)
    scores(1, 1)
    softmax(0, 0)

    def body(gg, carry):
        g = 2 * gg
        scores(g + 2, 0)
        softmax(g + 1, 1)
        values(g, 0)
        scores(g + 3, 1)
        softmax(g + 2, 0)
        values(g + 1, 1)
        return carry

    lax.fori_loop(0, n_groups // 2 - 1, body, 0)
    softmax(n_groups - 1, 1)
    values(n_groups - 2, 0)
    values(n_groups - 1, 1)


def _even_proj_kernel(x_ref, g_ref, w_ref, cos_ref, sin_ref, na_ref, perm_ref, slab_ref, *, tm):
    h = _rms(x_ref[...], g_ref[...]).astype(BF16)
    per_res = tm // RESIDUES
    tiles = MIX // LANES
    for c in (3, 4, 5, 0, 1, 2):
        a = jnp.dot(h, w_ref[:, c * MIX:(c + 1) * MIX], preferred_element_type=F32)
        if c in (3, 4):
            cos, sin = cos_ref[...], sin_ref[...]
            parts = []
            for j in range(tiles):
                aj = a[:, j * LANES:(j + 1) * LANES]
                parts.append(aj * cos + pltpu.roll(aj, LANES // 2, axis=1) * sin)
            a = jnp.concatenate(parts, axis=1)
        if c < 3:
            na_ref[:, c * MIX:(c + 1) * MIX] = a.astype(BF16)
        else:
            slab = slab_ref.at[c % 2]
            for j in range(tiles):
                for m in range(per_res):
                    slab[j, m * SLAB_PITCH:m * SLAB_PITCH + RESIDUES, :] = (
                        a[m * RESIDUES:(m + 1) * RESIDUES, j * LANES:(j + 1) * LANES])
            for r in range(RESIDUES):
                for j in range(tiles):
                    col = (c - 3) * MIX + j * LANES
                    rows = slab[j, pl.ds(r, per_res, stride=SLAB_PITCH), :]
                    perm_ref[r, :, col:col + LANES] = rows.astype(BF16)


def _even_proj(x2d, g, w, cos, sin, seq):
    n = x2d.shape[0]
    b = n // seq
    tm = TOKEN_TILE
    tps = seq // tm
    per_res = tm // RESIDUES
    return pl.pallas_call(
        functools.partial(_even_proj_kernel, tm=tm),
        grid=(n // tm,),
        in_specs=[
            pl.BlockSpec((tm, D_MODEL), lambda i: (i, 0)),
            _resident((1, D_MODEL)),
            _resident((D_MODEL, 6 * MIX)),
            pl.BlockSpec((tm, LANES), lambda i: (i % tps, 0)),
            pl.BlockSpec((tm, LANES), lambda i: (i % tps, 0)),
        ],
        out_specs=[pl.BlockSpec((tm, 3 * MIX), lambda i: (i, 0)),
                   pl.BlockSpec((None, RESIDUES, per_res, 3 * MIX), lambda i: (i // tps, 0, i % tps, 0))],
        out_shape=[jax.ShapeDtypeStruct((n, 3 * MIX), BF16),
                   jax.ShapeDtypeStruct((b, RESIDUES, seq // RESIDUES, 3 * MIX), BF16)],
        scratch_shapes=[pltpu.VMEM((2, MIX // LANES, per_res * SLAB_PITCH, LANES), F32)],
        compiler_params=_params(("parallel",)),
        name="even_proj",
    )(x2d, g, w, cos, sin)


def _na_kernel(q_ref, k_ref, v_ref, b_ref, o_ref, s0_ref, s1_ref, p0_ref, p1_ref, *, rows):
    lane = lax.broadcasted_iota(jnp.int32, (GRID_W, LANES), 1)
    is_a = lane < HEAD_DIM
    win = NA_WIN_R * GRID_W
    s_bufs, p_bufs = (s0_ref, s1_ref), (p0_ref, p1_ref)

    def window_start(r):
        rs = jnp.clip(r - NA_WIN_R // 2, 0, rows - NA_WIN_R)
        return pl.multiple_of(rs * GRID_W, GRID_W)

    def scores(g, slot):
        for u in range(NA_GROUP):
            r = jnp.int32(g * NA_GROUP + u)
            pat = jnp.where(r < NA_WIN_R // 2, r,
                            jnp.where(r > rows - NA_WIN_R // 2, r - (rows - NA_WIN_R), NA_WIN_R // 2))
            q = q_ref[pl.ds(pl.multiple_of(r * GRID_W, GRID_W), GRID_W), :]
            zero = jnp.zeros_like(q)
            q2 = jnp.concatenate([jnp.where(is_a, q, zero), jnp.where(is_a, zero, q)], axis=0)
            kw = k_ref[pl.ds(window_start(r), win), :]
            s = lax.dot_general(q2, kw, NT_DIMS, preferred_element_type=F32)
            s_bufs[slot][u] = s + b_ref[pat]

    def softmax(g, slot):
        for u in range(NA_GROUP):
            s = s_bufs[slot][u]
            m = jnp.max(s, axis=1, keepdims=True)
            p_bufs[slot][u] = jnp.exp2(s - m).astype(BF16)

    ones = jnp.ones((win, LANES), BF16)

    def values(g, slot):
        for u in range(NA_GROUP):
            r = jnp.int32(g * NA_GROUP + u)
            vw = jnp.concatenate([v_ref[pl.ds(window_start(r), win), :], ones], axis=1)
            pv = jnp.dot(p_bufs[slot][u], vw, preferred_element_type=F32)
            pv = pv[:, :LANES] / pv[:, LANES:]
            o_ref[pl.ds(pl.multiple_of(r * GRID_W, GRID_W), GRID_W), :] = (
                jnp.where(is_a, pv[:GRID_W], pv[GRID_W:]).astype(BF16))

    _software_pipeline(rows // NA_GROUP, scores, softmax, values)


def _na_attention(qkv, bias, seq):
    b = qkv.shape[0]
    rows = seq // GRID_W
    npair = NA_HEADS // 2
    win = NA_WIN_R * GRID_W
    blk = lambda off: pl.BlockSpec((None, seq, LANES), lambda p, i: (i, 0, off + p))
    s_buf = pltpu.VMEM((NA_GROUP, 2 * GRID_W, win), F32)
    p_buf = pltpu.VMEM((NA_GROUP, 2 * GRID_W, win), BF16)
    return pl.pallas_call(
        functools.partial(_na_kernel, rows=rows),
        grid=(npair, b),
        in_specs=[blk(0), blk(npair), blk(2 * npair),
                  pl.BlockSpec((NA_WIN_R, None, 2 * GRID_W, win), lambda p, i: (0, p, 0, 0))],
        out_specs=pl.BlockSpec((None, seq, LANES), lambda p, i: (i, 0, p)),
        out_shape=jax.ShapeDtypeStruct((b, seq, MIX), BF16),
        scratch_shapes=[s_buf, s_buf, p_buf, p_buf],
        compiler_params=_params(("parallel", "parallel")),
        name="na_attention",
    )(qkv, qkv, qkv, bias)


def _na_bias_table(rpb, rows):
    c = jnp.arange(GRID_W)
    cs = jnp.clip(c - NA_WIN_C // 2, 0, GRID_W - NA_WIN_C)
    j = jnp.arange(GRID_W)
    valid = (j[None, :] >= cs[:, None]) & (j[None, :] < cs[:, None] + NA_WIN_C)
    k = jnp.arange(2 * NA_WIN_C - 1)
    pick = (k[None, None, :] == (j[None, :, None] - c[:, None, None] + NA_WIN_C - 1)).astype(F32)
    full = jnp.einsum("hrk,cjk->hcrj", rpb.astype(F32), pick, precision=lax.Precision.HIGHEST)
    half = NA_WIN_R // 2
    reps = list(range(half)) + [half] + list(range(rows - half + 1, rows))
    tabs = []
    for r in reps:
        rs = min(max(r - half, 0), rows - NA_WIN_R)
        first = rs - r + NA_WIN_R - 1
        t = jnp.where(valid[None, :, None, :], full[:, :, first:first + NA_WIN_R, :] * LOG2_E, NEG_INF)
        tabs.append(t.reshape(NA_HEADS // 2, 2 * GRID_W, NA_WIN_R * GRID_W))
    return jnp.stack(tabs, axis=0)


def _dil_kernel(q_ref, k_ref, v_ref, o_ref, qf_ref, kf_ref, vf_ref, acc_ref, m_ref, l_ref, nat_ref, mask_ref,
                s0_ref, s1_ref, p0_ref, p1_ref, m0_ref, m1_ref, *, slab_len):
    qb, kw_rows = DIL_QBLOCK, 2 * DIL_QBLOCK
    s_bufs, p_bufs, m_bufs = (s0_ref, s1_ref), (p0_ref, p1_ref), (m0_ref, m1_ref)
    ones = jnp.ones((kw_rows, LANES), BF16)
    lane = lax.broadcasted_iota(jnp.int32, (qb, LANES), 1)
    is_a_out = lane < HEAD_DIM
    is_a_rot = (lane & (HEAD_DIM - 1)) < HEAD_DIM // 2
    row = lax.broadcasted_iota(jnp.int32, (2 * qb, kw_rows), 0) & (qb - 1)
    col = lax.broadcasted_iota(jnp.int32, (2 * qb, kw_rows), 1)

    qf_ref[...] = q_ref[...].astype(F32)
    kf_ref[...] = k_ref[...].astype(F32)
    vf_ref[...] = v_ref[...].astype(F32)

    branches = sorted(DIL_BRANCHES, key=lambda wd: -wd[1])

    @pl.when((pl.program_id(0) == 0) & (pl.program_id(1) == 0))
    def _():
        for bi, (window, d) in enumerate(branches):
            g_cnt = RESIDUES // d
            rq, rk = qb // g_cnt, kw_rows // g_cnt
            base = g_cnt * ((col % rk) - (row % rq)) + (col // rk) - (row // rq)
            for pat, off in enumerate((0, (rq - rk) // 2, rq - rk)):
                on_band = jnp.abs(base + g_cnt * off) <= (window // 2) // d
                mask_ref[3 * bi + pat] = jnp.where(on_band, jnp.inf, NEG_INF)

    for bi, (window, d) in enumerate(branches):
        g_cnt = RESIDUES // d
        rq, rk = qb // g_cnt, kw_rows // g_cnt
        lead = (rk - rq) // 2
        nblk = slab_len // rq

        def tile(g, u, d=d, g_cnt=g_cnt, rq=rq, rk=rk, lead=lead, nblk=nblk):
            it = jnp.int32(g * ATTN_GROUP + u)
            i = it % nblk
            q0 = i * rq
            ws = jnp.clip(q0 - lead, 0, slab_len - rk)
            pat = jnp.where(i == 0, 0, jnp.where(i == nblk - 1, 2, 1))
            slabs = [it // nblk + d * g_ for g_ in range(g_cnt)]
            return slabs, q0, pl.multiple_of(q0, rq), ws, pat

        def window(ref, ref32, slabs, ws, rq=rq, rk=rk):
            if rq % 16 == 0:
                wsa = pl.multiple_of(ws, 16)
                return jnp.concatenate([ref[s_, pl.ds(wsa, rk), :] for s_ in slabs], axis=0)
            return jnp.concatenate([ref32[s_, pl.ds(ws, rk), :] for s_ in slabs], axis=0).astype(BF16)

        def scores(g, slot, bi=bi, rq=rq, tile=tile, window=window):
            for u in range(ATTN_GROUP):
                slabs, _, q0a, ws, pat = tile(g, u)
                if rq % 16 == 0:
                    q = jnp.concatenate([q_ref[s_, pl.ds(q0a, rq), :] for s_ in slabs], axis=0)
                else:
                    q = jnp.concatenate([qf_ref[s_, pl.ds(q0a, rq), :] for s_ in slabs], axis=0).astype(BF16)
                kw = window(k_ref, kf_ref, slabs, ws)
                zero = jnp.zeros_like(q)
                q2 = jnp.concatenate([jnp.where(is_a_rot, q, zero), jnp.where(is_a_rot, zero, q)], axis=0)
                s = lax.dot_general(q2, kw, NT_DIMS, preferred_element_type=F32)
                s_bufs[slot][u] = jnp.minimum(s, mask_ref[3 * bi + pat])

        def softmax(g, slot):
            for u in range(ATTN_GROUP):
                s = s_bufs[slot][u]
                m = jnp.max(s, axis=1, keepdims=True)
                p_bufs[slot][u] = jnp.exp2(s - m).astype(BF16)
                m_bufs[slot][u] = jnp.where(is_a_out, m[:qb], m[qb:])

        def values(g, slot, bi=bi, rq=rq, tile=tile, window=window):
            for u in range(ATTN_GROUP):
                slabs, q0, q0a, ws, _ = tile(g, u)
                vw = jnp.concatenate([window(v_ref, vf_ref, slabs, ws), ones], axis=1)
                pv = jnp.dot(p_bufs[slot][u], vw, preferred_element_type=F32)
                pv_t = jnp.where(is_a_out, pv[:qb, :LANES], pv[qb:, :LANES])
                l_t = jnp.where(is_a_out, pv[:qb, LANES:], pv[qb:, LANES:])
                m_t = m_bufs[slot][u]
                if bi > 0:
                    m_old = jnp.concatenate([m_ref[s_, pl.ds(q0a, rq), :] for s_ in slabs], axis=0)
                    l_old = jnp.concatenate([l_ref[s_, pl.ds(q0a, rq), :] for s_ in slabs], axis=0)
                    a_old = jnp.concatenate([acc_ref[s_, pl.ds(q0a, rq), :] for s_ in slabs], axis=0)
                    m_new = jnp.maximum(m_old, m_t)
                    w_old = jnp.exp2(m_old - m_new)
                    w_new = jnp.exp2(m_t - m_new)
                    l_t = l_old * w_old + l_t * w_new
                    pv_t = a_old * w_old + pv_t * w_new
                    m_t = m_new
                if bi < len(branches) - 1:
                    for g_, s_ in enumerate(slabs):
                        rows = slice(g_ * rq, (g_ + 1) * rq)
                        m_ref[s_, pl.ds(q0a, rq), :] = m_t[rows]
                        l_ref[s_, pl.ds(q0a, rq), :] = l_t[rows]
                        acc_ref[s_, pl.ds(q0a, rq), :] = pv_t[rows]
                else:
                    out = pv_t / l_t
                    t0 = q0 * RESIDUES
                    for g_, s_ in enumerate(slabs):
                        nat_ref[pl.ds(t0 + s_, rq, stride=RESIDUES), :] = out[g_ * rq:(g_ + 1) * rq]
                    t0a = pl.multiple_of(t0, qb)
                    o_ref[pl.ds(t0a, qb), :] = nat_ref[pl.ds(t0a, qb), :].astype(BF16)

        _software_pipeline(d * nblk // ATTN_GROUP, scores, softmax, values)


def _dil_attention(perm, seq):
    b = perm.shape[0]
    slab_len = seq // RESIDUES
    npair = DIL_HEADS // 2
    blk = lambda off: pl.BlockSpec((None, RESIDUES, slab_len, LANES), lambda i, p: (i, 0, 0, off + p))
    slab = pltpu.VMEM((RESIDUES, slab_len, LANES), F32)
    s_buf = pltpu.VMEM((ATTN_GROUP, 2 * DIL_QBLOCK, 2 * DIL_QBLOCK), F32)
    p_buf = pltpu.VMEM((ATTN_GROUP, 2 * DIL_QBLOCK, 2 * DIL_QBLOCK), BF16)
    stat = pltpu.VMEM((ATTN_GROUP, DIL_QBLOCK, LANES), F32)
    return pl.pallas_call(
        functools.partial(_dil_kernel, slab_len=slab_len),
        grid=(b, npair),
        in_specs=[blk(0), blk(npair), blk(2 * npair)],
        out_specs=pl.BlockSpec((None, seq, LANES), lambda i, p: (i, 0, p)),
        out_shape=jax.ShapeDtypeStruct((b, seq, MIX), BF16),
        scratch_shapes=[slab, slab, slab, slab, slab, slab,
                        pltpu.VMEM((seq, LANES), F32),
                        pltpu.VMEM((3 * len(DIL_BRANCHES), 2 * DIL_QBLOCK, 2 * DIL_QBLOCK), F32),
                        s_buf, s_buf, p_buf, p_buf, stat, stat],
        compiler_params=_params(("arbitrary", "arbitrary")),
        name="dilated_attention",
    )(perm, perm, perm)


def _even_out_kernel(x_ref, oa_ref, ob_ref, w_ref, y_ref):
    o = jnp.concatenate([oa_ref[...], ob_ref[...]], axis=1)
    y_ref[...] = x_ref[...] + jnp.dot(o, w_ref[...], preferred_element_type=F32)


def _even_out(x2d, oa, ob, w):
    n = x2d.shape[0]
    tm = OUT_TILE
    row = lambda width: pl.BlockSpec((tm, width), lambda i: (i, 0))
    return pl.pallas_call(
        _even_out_kernel,
        grid=(n // tm,),
        in_specs=[row(D_MODEL), row(MIX), row(MIX), _resident((2 * MIX, D_MODEL))],
        out_specs=row(D_MODEL),
        out_shape=jax.ShapeDtypeStruct((n, D_MODEL), F32),
        compiler_params=_params(("parallel",)),
        name="even_out",
    )(x2d, oa, ob, w)


def _ffn_kernel(x_ref, xn_ref, xp_ref, g_ref, wu_ref, cp_ref, wd_ref, fg_ref, o_ref, h_ref, a_ref, act_ref, *,
                tm, tiles_per_seq, final):
    pos = pl.program_id(0) % tiles_per_seq
    g = g_ref[...]
    keep_prev = jnp.where(pos == 0, 0.0, 1.0)
    keep_next = jnp.where(pos == tiles_per_seq - 1, 0.0, 1.0)
    h_ref[0:tm, :] = _rms(x_ref[...], g).astype(BF16)
    halo = jnp.concatenate([_rms(xn_ref[...], g) * keep_next, _rms(xp_ref[...], g) * keep_prev], axis=0)
    h_ref[tm:tm + 2 * HALO, :] = halo.astype(BF16)
    tiles = FFN_CHUNK // LANES
    for c in range(N_FFN_CHUNKS):
        buf = a_ref.at[c % 2]
        ys = []
        for part in range(2):
            first = part * FFN_DIM + c * FFN_CHUNK
            a = jnp.dot(h_ref[...], wu_ref[:, first:first + FFN_CHUNK], preferred_element_type=F32)
            for j in range(tiles):
                cols = slice(j * LANES, (j + 1) * LANES)
                buf[part * tiles + j, HALO:tm + 2 * HALO, :] = a[:tm + HALO, cols]
                buf[part * tiles + j, 0:HALO, :] = a[tm + HALO:, cols]
        for part in range(2):
            for j in range(tiles):
                slab = buf.at[part * tiles + j]
                first = part * FFN_DIM + c * FFN_CHUNK + j * LANES
                cols = slice(first, first + LANES)
                y = cp_ref[3:4, cols] + slab[HALO - 1:HALO - 1 + tm, :] * cp_ref[0:1, cols]
                y = y + slab[HALO:HALO + tm, :] * cp_ref[1:2, cols]
                y = y + slab[HALO + 1:HALO + 1 + tm, :] * cp_ref[2:3, cols]
                ys.append(y)
        half_u = jnp.concatenate(ys[:tiles], axis=1)
        gate = jnp.concatenate(ys[tiles:], axis=1)
        t = jnp.tanh(gate * (GELU_C0 + GELU_C1 * (gate * gate)))
        w = half_u * gate
        act_ref[:, c * FFN_CHUNK:(c + 1) * FFN_CHUNK] = (w + w * t).astype(BF16)
    out = x_ref[...] + jnp.dot(act_ref[...], wd_ref[...], preferred_element_type=F32)
    if final:
        out = _rms(out, fg_ref[...])
    o_ref[...] = out


def _ffn(x2d, g, wu, cp, wd, fg, seq, final):
    n = x2d.shape[0]
    tm = FFN_TILE
    tps = seq // tm
    hb = tm // HALO
    last = n // HALO - 1
    return pl.pallas_call(
        functools.partial(_ffn_kernel, tm=tm, tiles_per_seq=tps, final=final),
        grid=(n // tm,),
        in_specs=[
            pl.BlockSpec((tm, D_MODEL), lambda i: (i, 0)),
            pl.BlockSpec((HALO, D_MODEL), lambda i: (jnp.minimum((i + 1) * hb, last), 0)),
            pl.BlockSpec((HALO, D_MODEL), lambda i: (jnp.maximum(i * hb - 1, 0), 0)),
            _resident((1, D_MODEL)),
            _resident((D_MODEL, 2 * FFN_DIM)),
            _resident((SUBLANES, 2 * FFN_DIM)),
            _resident((FFN_DIM, D_MODEL)),
            _resident((1, D_MODEL)),
        ],
        out_specs=pl.BlockSpec((tm, D_MODEL), lambda i: (i, 0)),
        out_shape=jax.ShapeDtypeStruct((n, D_MODEL), F32),
        scratch_shapes=[pltpu.VMEM((tm + 2 * HALO, D_MODEL), BF16),
                        pltpu.VMEM((2, 2 * FFN_CHUNK // LANES, tm + 2 * HALO, LANES), F32),
                        pltpu.VMEM((tm, FFN_DIM), BF16)],
        compiler_params=_params(("parallel",)),
        name="conv_ffn",
    )(x2d, x2d, x2d, g, wu, cp, wd, fg)


def _ret_proj_kernel(x_ref, g_ref, w_ref, cos_ref, sin_ref, o_ref):
    h = _rms(x_ref[...], g_ref[...]).astype(BF16)
    for c in range(RET_IN // MIX):
        a = jnp.dot(h, w_ref[:, c * MIX:(c + 1) * MIX], preferred_element_type=F32)
        if c < 2 * RET_Q // MIX:
            cos, sin = cos_ref[...], sin_ref[...]
            parts = []
            for j in range(MIX // RET_QK_DIM):
                x1 = a[:, j * RET_QK_DIM:j * RET_QK_DIM + LANES]
                x2 = a[:, j * RET_QK_DIM + LANES:(j + 1) * RET_QK_DIM]
                parts += [x1 * cos - x2 * sin, x1 * sin + x2 * cos]
            a = jnp.concatenate(parts, axis=1)
        elif c >= (2 * RET_Q + RET_V) // MIX:
            half_gate = a * 0.5
            a = half_gate + half_gate * jnp.tanh(half_gate)
        o_ref[:, c * MIX:(c + 1) * MIX] = a.astype(BF16)


def _ret_proj(x2d, g, w, cos, sin, seq):
    n = x2d.shape[0]
    tm = TOKEN_TILE
    tps = seq // tm
    return pl.pallas_call(
        _ret_proj_kernel,
        grid=(n // tm,),
        in_specs=[
            pl.BlockSpec((tm, D_MODEL), lambda i: (i, 0)),
            _resident((1, D_MODEL)),
            _resident((D_MODEL, RET_IN)),
            pl.BlockSpec((tm, LANES), lambda i: (i % tps, 0)),
            pl.BlockSpec((tm, LANES), lambda i: (i % tps, 0)),
        ],
        out_specs=pl.BlockSpec((tm, RET_IN), lambda i: (i, 0)),
        out_shape=jax.ShapeDtypeStruct((n, RET_IN), BF16),
        compiler_params=_params(("parallel",)),
        name="ret_proj",
    )(x2d, g, w, cos, sin)


def _ret_kernel(lg_ref, q_ref, k_ref, v_ref, out_ref, o_ref, sf_ref, sb_ref, d_ref, *, seq):
    c = RET_BLOCK
    n = seq // c
    hd = pl.program_id(1)
    lgf = lg_ref[0, hd]
    lgb = lg_ref[1, hd]
    pr = lax.broadcasted_iota(jnp.int32, (c, c), 0).astype(F32)
    pc = lax.broadcasted_iota(jnp.int32, (c, c), 1).astype(F32)
    diff = pr - pc
    lower = diff >= 0
    d_f = jnp.where(lower, jnp.exp(lgf * jnp.where(lower, diff, 0.0)), 0.0)
    d_b = jnp.where(lower, 0.0, jnp.exp(lgb * jnp.where(lower, 0.0, -diff)))
    d_ref[...] = d_f + d_b
    pos = lax.broadcasted_iota(jnp.int32, (c, 1), 0).astype(F32)
    qd_f = jnp.exp(lgf * (pos + 1.0))
    kd_f = jnp.exp(lgf * (c - 1.0 - pos))
    qd_b = jnp.exp(lgb * (c - pos))
    kd_b = jnp.exp(lgb * pos)
    ones = jnp.ones((1, LANES), F32)
    sd_f = jnp.exp(lgf * c * ones)[:, :1]
    sd_b = jnp.exp(lgb * c * ones)[:, :1]

    def emit(r0, o, accumulate):
        if not accumulate:
            o_ref[pl.ds(r0, c), :] = o
            return
        out_ref[pl.ds(r0, c), :] = (o_ref[pl.ds(r0, c), :] + o).astype(BF16)

    def fwd_part(i, accumulate):
        r0 = pl.multiple_of(i * c, c)
        q = q_ref[pl.ds(r0, c), :]
        k = k_ref[pl.ds(r0, c), :]
        v = v_ref[pl.ds(r0, c), :]
        s = lax.dot_general(q, k, NT_DIMS, preferred_element_type=F32)
        inner = (s * d_ref[...]).astype(BF16)
        qd = (q.astype(F32) * qd_f).astype(BF16)
        o = (jnp.dot(inner, v, preferred_element_type=F32)
             + jnp.dot(qd, sf_ref[...].astype(BF16), preferred_element_type=F32))
        emit(r0, o, accumulate)
        kd = (k.astype(F32) * kd_f).astype(BF16)
        sf_ref[...] = sf_ref[...] * sd_f + lax.dot_general(kd, v, TN_DIMS, preferred_element_type=F32)

    def bwd_part(i, accumulate):
        r0 = pl.multiple_of(i * c, c)
        q = q_ref[pl.ds(r0, c), :]
        k = k_ref[pl.ds(r0, c), :]
        v = v_ref[pl.ds(r0, c), :]
        qd = (q.astype(F32) * qd_b).astype(BF16)
        o = jnp.dot(qd, sb_ref[...].astype(BF16), preferred_element_type=F32)
        emit(r0, o, accumulate)
        kd = (k.astype(F32) * kd_b).astype(BF16)
        sb_ref[...] = sb_ref[...] * sd_b + lax.dot_general(kd, v, TN_DIMS, preferred_element_type=F32)

    sf_ref[...] = jnp.zeros_like(sf_ref)
    sb_ref[...] = jnp.zeros_like(sb_ref)

    def first_half(j, carry):
        fwd_part(j, False)
        bwd_part(n - 1 - j, False)
        return carry

    def second_half(j, carry):
        fwd_part(j, True)
        bwd_part(n - 1 - j, True)
        return carry

    lax.fori_loop(0, n // 2, first_half, 0, unroll=4)
    lax.fori_loop(n // 2, n, second_half, 0, unroll=4)


def _retention(qkv, log_gamma, seq):
    b = qkv.shape[0]
    kblk = RET_Q // RET_QK_DIM
    vblk = 2 * RET_Q // RET_V_DIM
    state = pltpu.VMEM((RET_QK_DIM, RET_V_DIM), F32)
    return pl.pallas_call(
        functools.partial(_ret_kernel, seq=seq),
        grid=(b, RET_HEADS),
        in_specs=[
            pl.BlockSpec(memory_space=pltpu.SMEM),
            pl.BlockSpec((None, seq, RET_QK_DIM), lambda i, h: (i, 0, h)),
            pl.BlockSpec((None, seq, RET_QK_DIM), lambda i, h: (i, 0, kblk + h)),
            pl.BlockSpec((None, seq, RET_V_DIM), lambda i, h: (i, 0, vblk + h)),
        ],
        out_specs=pl.BlockSpec((None, seq, RET_V_DIM), lambda i, h: (i, 0, h)),
        out_shape=jax.ShapeDtypeStruct((b, seq, RET_V), BF16),
        scratch_shapes=[pltpu.VMEM((seq, RET_V_DIM), F32), state, state,
                        pltpu.VMEM((RET_BLOCK, RET_BLOCK), F32)],
        compiler_params=_params(("parallel", "parallel")),
        name="retention",
    )(log_gamma, qkv, qkv, qkv)


def _ret_out_kernel(x_ref, r_ref, g_ref, w_ref, y_ref, *, tm):
    for r0 in range(0, tm, tm // 2):
        rows = slice(r0, r0 + tm // 2)
        parts = []
        for hd in range(RET_HEADS):
            cols = slice(hd * RET_V_DIM, (hd + 1) * RET_V_DIM)
            r = r_ref[rows, cols].astype(F32)
            mu = jnp.mean(r, axis=-1, keepdims=True)
            var = jnp.mean(jnp.square(r - mu), axis=-1, keepdims=True)
            rn = (r - mu) * lax.rsqrt(var + NORM_EPS)
            parts.append((g_ref[rows, cols].astype(F32) * rn).astype(BF16))
        act = jnp.concatenate(parts, axis=1)
        y_ref[rows, :] = x_ref[rows, :] + jnp.dot(act, w_ref[...], preferred_element_type=F32)


def _ret_out(x2d, r, proj, w):
    n = x2d.shape[0]
    tm = OUT_TILE
    row = lambda width: pl.BlockSpec((tm, width), lambda i: (i, 0))
    gate_blk = (2 * RET_Q + RET_V) // RET_V
    return pl.pallas_call(
        functools.partial(_ret_out_kernel, tm=tm),
        grid=(n // tm,),
        in_specs=[row(D_MODEL), row(RET_V), pl.BlockSpec((tm, RET_V), lambda i: (i, gate_blk)),
                  _resident((RET_V, D_MODEL))],
        out_specs=row(D_MODEL),
        out_shape=jax.ShapeDtypeStruct((n, D_MODEL), F32),
        compiler_params=_params(("parallel",)),
        name="ret_out",
    )(x2d, r, proj, w)


def _rope_tables(seq, dh, lane_freq, sign):
    inv = 1.0 / (ROPE_THETA ** (jnp.arange(0, dh, 2, dtype=F32) / dh))
    ang = jnp.arange(seq, dtype=F32)[:, None] * inv[None, :]
    return jnp.cos(ang)[:, lane_freq], jnp.sin(ang)[:, lane_freq] * sign[None, :]


def _pair_rotary_layout(w):
    rows = w.shape[0]
    w = w.reshape(rows, DIL_HEADS // 2, 2, 2, HEAD_DIM // 2)
    return w.transpose(0, 1, 3, 2, 4).reshape(rows, MIX)


def _prepare(attn_norm, even_w_in, na_rpb, even_w_out, ret_w_in, ret_decay_fwd, ret_decay_bwd, ret_w_out,
             ffn_norm, ffn_w_up, ffn_conv_w, ffn_conv_b, ffn_w_down, final_norm, seq):
    lane = jnp.arange(LANES)
    half = HEAD_DIM // 2
    w_in = even_w_in[0]
    blocks = [w_in[:, i * MIX:(i + 1) * MIX] for i in range(6)]
    scale = HEAD_DIM ** -0.5 * LOG2_E
    blocks[0] = blocks[0] * scale
    blocks[3] = _pair_rotary_layout(blocks[3]) * scale
    blocks[4] = _pair_rotary_layout(blocks[4])
    p = {}
    p["even_w_in"] = jnp.concatenate(blocks, axis=1).astype(BF16)
    p["even_cos"], p["even_sin"] = _rope_tables(seq, HEAD_DIM, lane % half,
                                                jnp.where(lane < LANES // 2, -1.0, 1.0))
    p["na_bias"] = _na_bias_table(na_rpb[0], seq // GRID_W)
    p["even_w_out"] = even_w_out[0].astype(BF16)

    rw = ret_w_in[0]
    rscale = RET_QK_DIM ** -0.5
    p["ret_w_in"] = jnp.concatenate([rw[:, :RET_Q] * rscale, rw[:, RET_Q:]], axis=1).astype(BF16)
    p["ret_cos"], p["ret_sin"] = _rope_tables(seq, RET_QK_DIM, lane, jnp.ones((LANES,), F32))
    p["ret_log_gamma"] = jnp.stack([-jax.nn.softplus(ret_decay_fwd[0].astype(F32)),
                                    -jax.nn.softplus(ret_decay_bwd[0].astype(F32))], axis=0)
    p["ret_w_out"] = ret_w_out[0].astype(BF16)

    u_half = jnp.concatenate([jnp.full((FFN_DIM,), 0.5, F32), jnp.ones((FFN_DIM,), F32)])
    p["ffn_w_up"], p["ffn_conv"] = [], []
    for layer in range(2):
        p["ffn_w_up"].append(ffn_w_up[layer].astype(BF16))
        taps = jnp.concatenate([ffn_conv_w[layer], ffn_conv_b[layer][None, :]], axis=0) * u_half[None, :]
        p["ffn_conv"].append(
            jnp.concatenate([taps, jnp.zeros((SUBLANES - CONV_WIDTH - 1, 2 * FFN_DIM), F32)], axis=0))
    p["ffn_w_down"] = [ffn_w_down[layer].astype(BF16) for layer in range(2)]
    p["attn_norm"] = [attn_norm[layer][None, :] for layer in range(2)]
    p["ffn_norm"] = [ffn_norm[layer][None, :] for layer in range(2)]
    p["final_norm"] = final_norm[None, :]
    return p


def _trunk(x, p):
    b, seq, _ = x.shape
    x2d = x.reshape(b * seq, D_MODEL)
    na_qkv, dil_qkv = _even_proj(x2d, p["attn_norm"][0], p["even_w_in"], p["even_cos"], p["even_sin"], seq)
    oa = _na_attention(na_qkv.reshape(b, seq, 3 * MIX), p["na_bias"], seq).reshape(b * seq, MIX)
    ob = _dil_attention(dil_qkv, seq).reshape(b * seq, MIX)
    x2d = _even_out(x2d, oa, ob, p["even_w_out"])
    x2d = _ffn(x2d, p["ffn_norm"][0], p["ffn_w_up"][0], p["ffn_conv"][0], p["ffn_w_down"][0],
               p["final_norm"], seq, final=False)
    rproj = _ret_proj(x2d, p["attn_norm"][1], p["ret_w_in"], p["ret_cos"], p["ret_sin"], seq)
    r = _retention(rproj.reshape(b, seq, RET_IN), p["ret_log_gamma"], seq)
    x2d = _ret_out(x2d, r.reshape(b * seq, RET_V), rproj, p["ret_w_out"])
    x2d = _ffn(x2d, p["ffn_norm"][1], p["ffn_w_up"][1], p["ffn_conv"][1], p["ffn_w_down"][1],
               p["final_norm"], seq, final=True)
    return x2d.reshape(b, seq, D_MODEL)


def kernel(x_prompt, x_sample, attn_norm, even_w_in, na_rpb, even_w_out, ret_w_in, ret_decay_fwd, ret_decay_bwd,
           ret_w_out, ffn_norm, ffn_w_up, ffn_conv_w, ffn_conv_b, ffn_w_down, final_norm):
    assert x_prompt.shape[1] == x_sample.shape[1]
    p = _prepare(attn_norm, even_w_in, na_rpb, even_w_out, ret_w_in, ret_decay_fwd, ret_decay_bwd, ret_w_out,
                 ffn_norm, ffn_w_up, ffn_conv_w, ffn_conv_b, ffn_w_down, final_norm, x_prompt.shape[1])
    return _trunk(x_prompt, p), _trunk(x_sample, p)
```

```python
import functools

import jax
import jax.numpy as jnp
from jax import lax
from jax.experimental import pallas as pl
from jax.experimental.pallas import tpu as pltpu

D_MODEL = 1024
GRID_W = 64
HEAD_DIM = 64
NA_HEADS = 8
NA_WIN_R = 8
NA_WIN_C = 16
DIL_HEADS = 8
DIL_BRANCHES = ((128, 1), (512, 4), (2048, 16))
DIL_QBLOCK = 128
RET_HEADS = 4
RET_QK_DIM = 256
RET_V_DIM = 512
FFN_DIM = 2816
CONV_WIDTH = 3
ROPE_THETA = 10000.0
NORM_EPS = 1e-6
NEG_INF = -1e30

LANES = 128
SUBLANES = 8
MIX = NA_HEADS * HEAD_DIM
RET_Q = RET_HEADS * RET_QK_DIM
RET_V = RET_HEADS * RET_V_DIM
RET_IN = 2 * RET_Q + 2 * RET_V
RET_BLOCK = 256
FFN_CHUNK = 256
N_FFN_CHUNKS = FFN_DIM // FFN_CHUNK
TOKEN_TILE = 1024
OUT_TILE = 1024
FFN_TILE = 512
ATTN_GROUP = 8
NA_GROUP = 4
HALO = SUBLANES
RESIDUES = max(d for _, d in DIL_BRANCHES)
SLAB_PITCH = RESIDUES + 4
VMEM_LIMIT = 56 * 1024 * 1024
LOG2_E = 1.4426950408889634
GELU_C0 = 0.7978845608028654
GELU_C1 = GELU_C0 * 0.044715

F32 = jnp.float32
BF16 = jnp.bfloat16
NT_DIMS = (((1,), (1,)), ((), ()))
TN_DIMS = (((0,), (0,)), ((), ()))


def _params(sem, vmem=VMEM_LIMIT):
    return pltpu.CompilerParams(dimension_semantics=sem, vmem_limit_bytes=vmem)


def _resident(shape):
    nd = len(shape)
    return pl.BlockSpec(shape, lambda *_: (0,) * nd, pipeline_mode=pl.Buffered(1))


def _rms(x, g):
    ms = jnp.mean(x * x, axis=-1, keepdims=True)
    return x * lax.rsqrt(ms + NORM_EPS) * g


def _software_pipeline(n_groups, scores, softmax, values):
    assert n_groups % 2 == 0 and n_groups >= 4
    scores(0, 0)
    scores(1, 1)
    softmax(0, 0)

    def body(gg, carry):
        g = 2 * gg
        scores(g + 2, 0)
        softmax(g + 1, 1)
        values(g, 0)
        scores(g + 3, 1)
        softmax(g + 2, 0)
        values(g + 1, 1)
        return carry

    lax.fori_loop(0, n_groups // 2 - 1, body, 0)
    softmax(n_groups - 1, 1)
    values(n_groups - 2, 0)
    values(n_groups - 1, 1)


def _even_proj_kernel(x_ref, g_ref, w_ref, cos_ref, sin_ref, na_ref, perm_ref, slab_ref, *, tm):
    h = _rms(x_ref[...], g_ref[...]).astype(BF16)
    per_res = tm // RESIDUES
    tiles = MIX // LANES
    for c in (3, 4, 5, 0, 1, 2):
        a = jnp.dot(h, w_ref[:, c * MIX:(c + 1) * MIX], preferred_element_type=F32)
        if c in (3, 4):
            cos, sin = cos_ref[...], sin_ref[...]
            parts = []
            for j in range(tiles):
                aj = a[:, j * LANES:(j + 1) * LANES]
                parts.append(aj * cos + pltpu.roll(aj, LANES // 2, axis=1) * sin)
            a = jnp.concatenate(parts, axis=1)
        if c < 3:
            na_ref[:, c * MIX:(c + 1) * MIX] = a.astype(BF16)
        else:
            slab = slab_ref.at[c % 2]
            for j in range(tiles):
                for m in range(per_res):
                    slab[j, m * SLAB_PITCH:m * SLAB_PITCH + RESIDUES, :] = (
                        a[m * RESIDUES:(m + 1) * RESIDUES, j * LANES:(j + 1) * LANES])
            for r in range(RESIDUES):
                for j in range(tiles):
                    col = (c - 3) * MIX + j * LANES
                    rows = slab[j, pl.ds(r, per_res, stride=SLAB_PITCH), :]
                    perm_ref[r, :, col:col + LANES] = rows.astype(BF16)


def _even_proj(x2d, g, w, cos, sin, seq):
    n = x2d.shape[0]
    b = n // seq
    tm = TOKEN_TILE
    tps = seq // tm
    per_res = tm // RESIDUES
    return pl.pallas_call(
        functools.partial(_even_proj_kernel, tm=tm),
        grid=(n // tm,),
        in_specs=[
            pl.BlockSpec((tm, D_MODEL), lambda i: (i, 0)),
            _resident((1, D_MODEL)),
            _resident((D_MODEL, 6 * MIX)),
            pl.BlockSpec((tm, LANES), lambda i: (i % tps, 0)),
            pl.BlockSpec((tm, LANES), lambda i: (i % tps, 0)),
        ],
        out_specs=[pl.BlockSpec((tm, 3 * MIX), lambda i: (i, 0)),
                   pl.BlockSpec((None, RESIDUES, per_res, 3 * MIX), lambda i: (i // tps, 0, i % tps, 0))],
        out_shape=[jax.ShapeDtypeStruct((n, 3 * MIX), BF16),
                   jax.ShapeDtypeStruct((b, RESIDUES, seq // RESIDUES, 3 * MIX), BF16)],
        scratch_shapes=[pltpu.VMEM((2, MIX // LANES, per_res * SLAB_PITCH, LANES), F32)],
        compiler_params=_params(("parallel",)),
        name="even_proj",
    )(x2d, g, w, cos, sin)


def _na_kernel(q_ref, k_ref, v_ref, b_ref, o_ref, s0_ref, s1_ref, p0_ref, p1_ref, *, rows):
    lane = lax.broadcasted_iota(jnp.int32, (GRID_W, LANES), 1)
    is_a = lane < HEAD_DIM
    win = NA_WIN_R * GRID_W
    s_bufs, p_bufs = (s0_ref, s1_ref), (p0_ref, p1_ref)

    def window_start(r):
        rs = jnp.clip(r - NA_WIN_R // 2, 0, rows - NA_WIN_R)
        return pl.multiple_of(rs * GRID_W, GRID_W)

    def scores(g, slot):
        for u in range(NA_GROUP):
            r = jnp.int32(g * NA_GROUP + u)
            pat = jnp.where(r < NA_WIN_R // 2, r,
                            jnp.where(r > rows - NA_WIN_R // 2, r - (rows - NA_WIN_R), NA_WIN_R // 2))
            q = q_ref[pl.ds(pl.multiple_of(r * GRID_W, GRID_W), GRID_W), :]
            zero = jnp.zeros_like(q)
            q2 = jnp.concatenate([jnp.where(is_a, q, zero), jnp.where(is_a, zero, q)], axis=0)
            kw = k_ref[pl.ds(window_start(r), win), :]
            s = lax.dot_general(q2, kw, NT_DIMS, preferred_element_type=F32)
            s_bufs[slot][u] = s + b_ref[pat]

    def softmax(g, slot):
        for u in range(NA_GROUP):
            s = s_bufs[slot][u]
            m = jnp.max(s, axis=1, keepdims=True)
            p_bufs[slot][u] = jnp.exp2(s - m).astype(BF16)

    ones = jnp.ones((win, LANES), BF16)

    def values(g, slot):
        for u in range(NA_GROUP):
            r = jnp.int32(g * NA_GROUP + u)
            vw = jnp.concatenate([v_ref[pl.ds(window_start(r), win), :], ones], axis=1)
            pv = jnp.dot(p_bufs[slot][u], vw, preferred_element_type=F32)
            pv = pv[:, :LANES] / pv[:, LANES:]
            o_ref[pl.ds(pl.multiple_of(r * GRID_W, GRID_W), GRID_W), :] = (
                jnp.where(is_a, pv[:GRID_W], pv[GRID_W:]).astype(BF16))

    _software_pipeline(rows // NA_GROUP, scores, softmax, values)


def _na_attention(qkv, bias, seq):
    b = qkv.shape[0]
    rows = seq // GRID_W
    npair = NA_HEADS // 2
    win = NA_WIN_R * GRID_W
    blk = lambda off: pl.BlockSpec((None, seq, LANES), lambda p, i: (i, 0, off + p))
    s_buf = pltpu.VMEM((NA_GROUP, 2 * GRID_W, win), F32)
    p_buf = pltpu.VMEM((NA_GROUP, 2 * GRID_W, win), BF16)
    return pl.pallas_call(
        functools.partial(_na_kernel, rows=rows),
        grid=(npair, b),
        in_specs=[blk(0), blk(npair), blk(2 * npair),
                  pl.BlockSpec((NA_WIN_R, None, 2 * GRID_W, win), lambda p, i: (0, p, 0, 0))],
        out_specs=pl.BlockSpec((None, seq, LANES), lambda p, i: (i, 0, p)),
        out_shape=jax.ShapeDtypeStruct((b, seq, MIX), BF16),
        scratch_shapes=[s_buf, s_buf, p_buf, p_buf],
        compiler_params=_params(("parallel", "parallel")),
        name="na_attention",
    )(qkv, qkv, qkv, bias)


def _na_bias_table(rpb, rows):
    c = jnp.arange(GRID_W)
    cs = jnp.clip(c - NA_WIN_C // 2, 0, GRID_W - NA_WIN_C)
    j = jnp.arange(GRID_W)
    valid = (j[None, :] >= cs[:, None]) & (j[None, :] < cs[:, None] + NA_WIN_C)
    k = jnp.arange(2 * NA_WIN_C - 1)
    pick = (k[None, None, :] == (j[None, :, None] - c[:, None, None] + NA_WIN_C - 1)).astype(F32)
    full = jnp.einsum("hrk,cjk->hcrj", rpb.astype(F32), pick, precision=lax.Precision.HIGHEST)
    half = NA_WIN_R // 2
    reps = list(range(half)) + [half] + list(range(rows - half + 1, rows))
    tabs = []
    for r in reps:
        rs = min(max(r - half, 0), rows - NA_WIN_R)
        first = rs - r + NA_WIN_R - 1
        t = jnp.where(valid[None, :, None, :], full[:, :, first:first + NA_WIN_R, :] * LOG2_E, NEG_INF)
        tabs.append(t.reshape(NA_HEADS // 2, 2 * GRID_W, NA_WIN_R * GRID_W))
    return jnp.stack(tabs, axis=0)


def _dil_kernel(q_ref, k_ref, v_ref, o_ref, qf_ref, kf_ref, vf_ref, acc_ref, m_ref, l_ref, nat_ref, mask_ref,
                s0_ref, s1_ref, p0_ref, p1_ref, m0_ref, m1_ref, *, slab_len):
    qb, kw_rows = DIL_QBLOCK, 2 * DIL_QBLOCK
    s_bufs, p_bufs, m_bufs = (s0_ref, s1_ref), (p0_ref, p1_ref), (m0_ref, m1_ref)
    ones = jnp.ones((kw_rows, LANES), BF16)
    lane = lax.broadcasted_iota(jnp.int32, (qb, LANES), 1)
    is_a_out = lane < HEAD_DIM
    is_a_rot = (lane & (HEAD_DIM - 1)) < HEAD_DIM // 2
    row = lax.broadcasted_iota(jnp.int32, (2 * qb, kw_rows), 0) & (qb - 1)
    col = lax.broadcasted_iota(jnp.int32, (2 * qb, kw_rows), 1)

    qf_ref[...] = q_ref[...].astype(F32)
    kf_ref[...] = k_ref[...].astype(F32)
    vf_ref[...] = v_ref[...].astype(F32)

    branches = sorted(DIL_BRANCHES, key=lambda wd: -wd[1])

    @pl.when((pl.program_id(0) == 0) & (pl.program_id(1) == 0))
    def _():
        for bi, (window, d) in enumerate(branches):
            g_cnt = RESIDUES // d
            rq, rk = qb // g_cnt, kw_rows // g_cnt
            base = g_cnt * ((col % rk) - (row % rq)) + (col // rk) - (row // rq)
            for pat, off in enumerate((0, (rq - rk) // 2, rq - rk)):
                on_band = jnp.abs(base + g_cnt * off) <= (window // 2) // d
                mask_ref[3 * bi + pat] = jnp.where(on_band, jnp.inf, NEG_INF)

    for bi, (window, d) in enumerate(branches):
        g_cnt = RESIDUES // d
        rq, rk = qb // g_cnt, kw_rows // g_cnt
        lead = (rk - rq) // 2
        nblk = slab_len // rq

        def tile(g, u, d=d, g_cnt=g_cnt, rq=rq, rk=rk, lead=lead, nblk=nblk):
            it = jnp.int32(g * ATTN_GROUP + u)
            i = it % nblk
            q0 = i * rq
            ws = jnp.clip(q0 - lead, 0, slab_len - rk)
            pat = jnp.where(i == 0, 0, jnp.where(i == nblk - 1, 2, 1))
            slabs = [it // nblk + d * g_ for g_ in range(g_cnt)]
            return slabs, q0, pl.multiple_of(q0, rq), ws, pat

        def window(ref, ref32, slabs, ws, rq=rq, rk=rk):
            if rq % 16 == 0:
                wsa = pl.multiple_of(ws, 16)
                return jnp.concatenate([ref[s_, pl.ds(wsa, rk), :] for s_ in slabs], axis=0)
            return jnp.concatenate([ref32[s_, pl.ds(ws, rk), :] for s_ in slabs], axis=0).astype(BF16)

        def scores(g, slot, bi=bi, rq=rq, tile=tile, window=window):
            for u in range(ATTN_GROUP):
                slabs, _, q0a, ws, pat = tile(g, u)
                if rq % 16 == 0:
                    q = jnp.concatenate([q_ref[s_, pl.ds(q0a, rq), :] for s_ in slabs], axis=0)
                else:
                    q = jnp.concatenate([qf_ref[s_, pl.ds(q0a, rq), :] for s_ in slabs], axis=0).astype(BF16)
                kw = window(k_ref, kf_ref, slabs, ws)
                zero = jnp.zeros_like(q)
                q2 = jnp.concatenate([jnp.where(is_a_rot, q, zero), jnp.where(is_a_rot, zero, q)], axis=0)
                s = lax.dot_general(q2, kw, NT_DIMS, preferred_element_type=F32)
                s_bufs[slot][u] = jnp.minimum(s, mask_ref[3 * bi + pat])

        def softmax(g, slot):
            for u in range(ATTN_GROUP):
                s = s_bufs[slot][u]
                m = jnp.max(s, axis=1, keepdims=True)
                p_bufs[slot][u] = jnp.exp2(s - m).astype(BF16)
                m_bufs[slot][u] = jnp.where(is_a_out, m[:qb], m[qb:])

        def values(g, slot, bi=bi, rq=rq, tile=tile, window=window):
            for u in range(ATTN_GROUP):
                slabs, q0, q0a, ws, _ = tile(g, u)
                vw = jnp.concatenate([window(v_ref, vf_ref, slabs, ws), ones], axis=1)
                pv = jnp.dot(p_bufs[slot][u], vw, preferred_element_type=F32)
                pv_t = jnp.where(is_a_out, pv[:qb, :LANES], pv[qb:, :LANES])
                l_t = jnp.where(is_a_out, pv[:qb, LANES:], pv[qb:, LANES:])
                m_t = m_bufs[slot][u]
                if bi > 0:
                    m_old = jnp.concatenate([m_ref[s_, pl.ds(q0a, rq), :] for s_ in slabs], axis=0)
                    l_old = jnp.concatenate([l_ref[s_, pl.ds(q0a, rq), :] for s_ in slabs], axis=0)
                    a_old = jnp.concatenate([acc_ref[s_, pl.ds(q0a, rq), :] for s_ in slabs], axis=0)
                    m_new = jnp.maximum(m_old, m_t)
                    w_old = jnp.exp2(m_old - m_new)
                    w_new = jnp.exp2(m_t - m_new)
                    l_t = l_old * w_old + l_t * w_new
                    pv_t = a_old * w_old + pv_t * w_new
                    m_t = m_new
                if bi < len(branches) - 1:
                    for g_, s_ in enumerate(slabs):
                        rows = slice(g_ * rq, (g_ + 1) * rq)
                        m_ref[s_, pl.ds(q0a, rq), :] = m_t[rows]
                        l_ref[s_, pl.ds(q0a, rq), :] = l_t[rows]
                        acc_ref[s_, pl.ds(q0a, rq), :] = pv_t[rows]
                else:
                    out = pv_t / l_t
                    t0 = q0 * RESIDUES
                    for g_, s_ in enumerate(slabs):
                        nat_ref[pl.ds(t0 + s_, rq, stride=RESIDUES), :] = out[g_ * rq:(g_ + 1) * rq]
                    t0a = pl.multiple_of(t0, qb)
                    o_ref[pl.ds(t0a, qb), :] = nat_ref[pl.ds(t0a, qb), :].astype(BF16)

        _software_pipeline(d * nblk // ATTN_GROUP, scores, softmax, values)


def _dil_attention(perm, seq):
    b = perm.shape[0]
    slab_len = seq // RESIDUES
    npair = DIL_HEADS // 2
    blk = lambda off: pl.BlockSpec((None, RESIDUES, slab_len, LANES), lambda i, p: (i, 0, 0, off + p))
    slab = pltpu.VMEM((RESIDUES, slab_len, LANES), F32)
    s_buf = pltpu.VMEM((ATTN_GROUP, 2 * DIL_QBLOCK, 2 * DIL_QBLOCK), F32)
    p_buf = pltpu.VMEM((ATTN_GROUP, 2 * DIL_QBLOCK, 2 * DIL_QBLOCK), BF16)
    stat = pltpu.VMEM((ATTN_GROUP, DIL_QBLOCK, LANES), F32)
    return pl.pallas_call(
        functools.partial(_dil_kernel, slab_len=slab_len),
        grid=(b, npair),
        in_specs=[blk(0), blk(npair), blk(2 * npair)],
        out_specs=pl.BlockSpec((None, seq, LANES), lambda i, p: (i, 0, p)),
        out_shape=jax.ShapeDtypeStruct((b, seq, MIX), BF16),
        scratch_shapes=[slab, slab, slab, slab, slab, slab,
                        pltpu.VMEM((seq, LANES), F32),
                        pltpu.VMEM((3 * len(DIL_BRANCHES), 2 * DIL_QBLOCK, 2 * DIL_QBLOCK), F32),
                        s_buf, s_buf, p_buf, p_buf, stat, stat],
        compiler_params=_params(("arbitrary", "arbitrary")),
        name="dilated_attention",
    )(perm, perm, perm)


def _even_out_kernel(x_ref, oa_ref, ob_ref, w_ref, y_ref):
    o = jnp.concatenate([oa_ref[...], ob_ref[...]], axis=1)
    y_ref[...] = x_ref[...] + jnp.dot(o, w_ref[...], preferred_element_type=F32)


def _even_out(x2d, oa, ob, w):
    n = x2d.shape[0]
    tm = OUT_TILE
    row = lambda width: pl.BlockSpec((tm, width), lambda i: (i, 0))
    return pl.pallas_call(
        _even_out_kernel,
        grid=(n // tm,),
        in_specs=[row(D_MODEL), row(MIX), row(MIX), _resident((2 * MIX, D_MODEL))],
        out_specs=row(D_MODEL),
        out_shape=jax.ShapeDtypeStruct((n, D_MODEL), F32),
        compiler_params=_params(("parallel",)),
        name="even_out",
    )(x2d, oa, ob, w)


def _ffn_kernel(x_ref, xn_ref, xp_ref, g_ref, wu_ref, cp_ref, wd_ref, fg_ref, o_ref, h_ref, a_ref, act_ref, *,
                tm, tiles_per_seq, final):
    pos = pl.program_id(0) % tiles_per_seq
    g = g_ref[...]
    keep_prev = jnp.where(pos == 0, 0.0, 1.0)
    keep_next = jnp.where(pos == tiles_per_seq - 1, 0.0, 1.0)
    h_ref[0:tm, :] = _rms(x_ref[...], g).astype(BF16)
    halo = jnp.concatenate([_rms(xn_ref[...], g) * keep_next, _rms(xp_ref[...], g) * keep_prev], axis=0)
    h_ref[tm:tm + 2 * HALO, :] = halo.astype(BF16)
    tiles = FFN_CHUNK // LANES
    for c in range(N_FFN_CHUNKS):
        buf = a_ref.at[c % 2]
        ys = []
        for part in range(2):
            first = part * FFN_DIM + c * FFN_CHUNK
            a = jnp.dot(h_ref[...], wu_ref[:, first:first + FFN_CHUNK], preferred_element_type=F32)
            for j in range(tiles):
                cols = slice(j * LANES, (j + 1) * LANES)
                buf[part * tiles + j, HALO:tm + 2 * HALO, :] = a[:tm + HALO, cols]
                buf[part * tiles + j, 0:HALO, :] = a[tm + HALO:, cols]
        for part in range(2):
            for j in range(tiles):
                slab = buf.at[part * tiles + j]
                first = part * FFN_DIM + c * FFN_CHUNK + j * LANES
                cols = slice(first, first + LANES)
                y = cp_ref[3:4, cols] + slab[HALO - 1:HALO - 1 + tm, :] * cp_ref[0:1, cols]
                y = y + slab[HALO:HALO + tm, :] * cp_ref[1:2, cols]
                y = y + slab[HALO + 1:HALO + 1 + tm, :] * cp_ref[2:3, cols]
                ys.append(y)
        half_u = jnp.concatenate(ys[:tiles], axis=1)
        gate = jnp.concatenate(ys[tiles:], axis=1)
        t = jnp.tanh(gate * (GELU_C0 + GELU_C1 * (gate * gate)))
        w = half_u * gate
        act_ref[:, c * FFN_CHUNK:(c + 1) * FFN_CHUNK] = (w + w * t).astype(BF16)
    out = x_ref[...] + jnp.dot(act_ref[...], wd_ref[...], preferred_element_type=F32)
    if final:
        out = _rms(out, fg_ref[...])
    o_ref[...] = out


def _ffn(x2d, g, wu, cp, wd, fg, seq, final):
    n = x2d.shape[0]
    tm = FFN_TILE
    tps = seq // tm
    hb = tm // HALO
    last = n // HALO - 1
    return pl.pallas_call(
        functools.partial(_ffn_kernel, tm=tm, tiles_per_seq=tps, final=final),
        grid=(n // tm,),
        in_specs=[
            pl.BlockSpec((tm, D_MODEL), lambda i: (i, 0)),
            pl.BlockSpec((HALO, D_MODEL), lambda i: (jnp.minimum((i + 1) * hb, last), 0)),
            pl.BlockSpec((HALO, D_MODEL), lambda i: (jnp.maximum(i * hb - 1, 0), 0)),
            _resident((1, D_MODEL)),
            _resident((D_MODEL, 2 * FFN_DIM)),
            _resident((SUBLANES, 2 * FFN_DIM)),
            _resident((FFN_DIM, D_MODEL)),
            _resident((1, D_MODEL)),
        ],
        out_specs=pl.BlockSpec((tm, D_MODEL), lambda i: (i, 0)),
        out_shape=jax.ShapeDtypeStruct((n, D_MODEL), F32),
        scratch_shapes=[pltpu.VMEM((tm + 2 * HALO, D_MODEL), BF16),
                        pltpu.VMEM((2, 2 * FFN_CHUNK // LANES, tm + 2 * HALO, LANES), F32),
                        pltpu.VMEM((tm, FFN_DIM), BF16)],
        compiler_params=_params(("parallel",)),
        name="conv_ffn",
    )(x2d, x2d, x2d, g, wu, cp, wd, fg)


def _ret_proj_kernel(x_ref, g_ref, w_ref, cos_ref, sin_ref, o_ref):
    h = _rms(x_ref[...], g_ref[...]).astype(BF16)
    for c in range(RET_IN // MIX):
        a = jnp.dot(h, w_ref[:, c * MIX:(c + 1) * MIX], preferred_element_type=F32)
        if c < 2 * RET_Q // MIX:
            cos, sin = cos_ref[...], sin_ref[...]
            parts = []
            for j in range(MIX // RET_QK_DIM):
                x1 = a[:, j * RET_QK_DIM:j * RET_QK_DIM + LANES]
                x2 = a[:, j * RET_QK_DIM + LANES:(j + 1) * RET_QK_DIM]
                parts += [x1 * cos - x2 * sin, x1 * sin + x2 * cos]
            a = jnp.concatenate(parts, axis=1)
        elif c >= (2 * RET_Q + RET_V) // MIX:
            half_gate = a * 0.5
            a = half_gate + half_gate * jnp.tanh(half_gate)
        o_ref[:, c * MIX:(c + 1) * MIX] = a.astype(BF16)


def _ret_proj(x2d, g, w, cos, sin, seq):
    n = x2d.shape[0]
    tm = TOKEN_TILE
    tps = seq // tm
    return pl.pallas_call(
        _ret_proj_kernel,
        grid=(n // tm,),
        in_specs=[
            pl.BlockSpec((tm, D_MODEL), lambda i: (i, 0)),
            _resident((1, D_MODEL)),
            _resident((D_MODEL, RET_IN)),
            pl.BlockSpec((tm, LANES), lambda i: (i % tps, 0)),
            pl.BlockSpec((tm, LANES), lambda i: (i % tps, 0)),
        ],
        out_specs=pl.BlockSpec((tm, RET_IN), lambda i: (i, 0)),
        out_shape=jax.ShapeDtypeStruct((n, RET_IN), BF16),
        compiler_params=_params(("parallel",)),
        name="ret_proj",
    )(x2d, g, w, cos, sin)


def _ret_kernel(lg_ref, q_ref, k_ref, v_ref, out_ref, o_ref, sf_ref, sb_ref, d_ref, *, seq):
    c = RET_BLOCK
    n = seq // c
    hd = pl.program_id(1)
    lgf = lg_ref[0, hd]
    lgb = lg_ref[1, hd]
    pr = lax.broadcasted_iota(jnp.int32, (c, c), 0).astype(F32)
    pc = lax.broadcasted_iota(jnp.int32, (c, c), 1).astype(F32)
    diff = pr - pc
    lower = diff >= 0
    d_f = jnp.where(lower, jnp.exp(lgf * jnp.where(lower, diff, 0.0)), 0.0)
    d_b = jnp.where(lower, 0.0, jnp.exp(lgb * jnp.where(lower, 0.0, -diff)))
    d_ref[...] = d_f + d_b
    pos = lax.broadcasted_iota(jnp.int32, (c, 1), 0).astype(F32)
    qd_f = jnp.exp(lgf * (pos + 1.0))
    kd_f = jnp.exp(lgf * (c - 1.0 - pos))
    qd_b = jnp.exp(lgb * (c - pos))
    kd_b = jnp.exp(lgb * pos)
    ones = jnp.ones((1, LANES), F32)
    sd_f = jnp.exp(lgf * c * ones)[:, :1]
    sd_b = jnp.exp(lgb * c * ones)[:, :1]

    def emit(r0, o, accumulate):
        if not accumulate:
            o_ref[pl.ds(r0, c), :] = o
            return
        out_ref[pl.ds(r0, c), :] = (o_ref[pl.ds(r0, c), :] + o).astype(BF16)

    def fwd_part(i, accumulate):
        r0 = pl.multiple_of(i * c, c)
        q = q_ref[pl.ds(r0, c), :]
        k = k_ref[pl.ds(r0, c), :]
        v = v_ref[pl.ds(r0, c), :]
        s = lax.dot_general(q, k, NT_DIMS, preferred_element_type=F32)
        inner = (s * d_ref[...]).astype(BF16)
        qd = (q.astype(F32) * qd_f).astype(BF16)
        o = (jnp.dot(inner, v, preferred_element_type=F32)
             + jnp.dot(qd, sf_ref[...].astype(BF16), preferred_element_type=F32))
        emit(r0, o, accumulate)
        kd = (k.astype(F32) * kd_f).astype(BF16)
        sf_ref[...] = sf_ref[...] * sd_f + lax.dot_general(kd, v, TN_DIMS, preferred_element_type=F32)

    def bwd_part(i, accumulate):
        r0 = pl.multiple_of(i * c, c)
        q = q_ref[pl.ds(r0, c), :]
        k = k_ref[pl.ds(r0, c), :]
        v = v_ref[pl.ds(r0, c), :]
        qd = (q.astype(F32) * qd_b).astype(BF16)
        o = jnp.dot(qd, sb_ref[...].astype(BF16), preferred_element_type=F32)
        emit(r0, o, accumulate)
        kd = (k.astype(F32) * kd_b).astype(BF16)
        sb_ref[...] = sb_ref[...] * sd_b + lax.dot_general(kd, v, TN_DIMS, preferred_element_type=F32)

    sf_ref[...] = jnp.zeros_like(sf_ref)
    sb_ref[...] = jnp.zeros_like(sb_ref)

    def first_half(j, carry):
        fwd_part(j, False)
        bwd_part(n - 1 - j, False)
        return carry

    def second_half(j, carry):
        fwd_part(j, True)
        bwd_part(n - 1 - j, True)
        return carry

    lax.fori_loop(0, n // 2, first_half, 0, unroll=4)
    lax.fori_loop(n // 2, n, second_half, 0, unroll=4)


def _retention(qkv, log_gamma, seq):
    b = qkv.shape[0]
    kblk = RET_Q // RET_QK_DIM
    vblk = 2 * RET_Q // RET_V_DIM
    state = pltpu.VMEM((RET_QK_DIM, RET_V_DIM), F32)
    return pl.pallas_call(
        functools.partial(_ret_kernel, seq=seq),
        grid=(b, RET_HEADS),
        in_specs=[
            pl.BlockSpec(memory_space=pltpu.SMEM),
            pl.BlockSpec((None, seq, RET_QK_DIM), lambda i, h: (i, 0, h)),
            pl.BlockSpec((None, seq, RET_QK_DIM), lambda i, h: (i, 0, kblk + h)),
            pl.BlockSpec((None, seq, RET_V_DIM), lambda i, h: (i, 0, vblk + h)),
        ],
        out_specs=pl.BlockSpec((None, seq, RET_V_DIM), lambda i, h: (i, 0, h)),
        out_shape=jax.ShapeDtypeStruct((b, seq, RET_V), BF16),
        scratch_shapes=[pltpu.VMEM((seq, RET_V_DIM), F32), state, state,
                        pltpu.VMEM((RET_BLOCK, RET_BLOCK), F32)],
        compiler_params=_params(("parallel", "parallel")),
        name="retention",
    )(log_gamma, qkv, qkv, qkv)


def _ret_out_kernel(x_ref, r_ref, g_ref, w_ref, y_ref, *, tm):
    for r0 in range(0, tm, tm // 2):
        rows = slice(r0, r0 + tm // 2)
        parts = []
        for hd in range(RET_HEADS):
            cols = slice(hd * RET_V_DIM, (hd + 1) * RET_V_DIM)
            r = r_ref[rows, cols].astype(F32)
            mu = jnp.mean(r, axis=-1, keepdims=True)
            var = jnp.mean(jnp.square(r - mu), axis=-1, keepdims=True)
            rn = (r - mu) * lax.rsqrt(var + NORM_EPS)
            parts.append((g_ref[rows, cols].astype(F32) * rn).astype(BF16))
        act = jnp.concatenate(parts, axis=1)
        y_ref[rows, :] = x_ref[rows, :] + jnp.dot(act, w_ref[...], preferred_element_type=F32)


def _ret_out(x2d, r, proj, w):
    n = x2d.shape[0]
    tm = OUT_TILE
    row = lambda width: pl.BlockSpec((tm, width), lambda i: (i, 0))
    gate_blk = (2 * RET_Q + RET_V) // RET_V
    return pl.pallas_call(
        functools.partial(_ret_out_kernel, tm=tm),
        grid=(n // tm,),
        in_specs=[row(D_MODEL), row(RET_V), pl.BlockSpec((tm, RET_V), lambda i: (i, gate_blk)),
                  _resident((RET_V, D_MODEL))],
        out_specs=row(D_MODEL),
        out_shape=jax.ShapeDtypeStruct((n, D_MODEL), F32),
        compiler_params=_params(("parallel",)),
        name="ret_out",
    )(x2d, r, proj, w)


def _rope_tables(seq, dh, lane_freq, sign):
    inv = 1.0 / (ROPE_THETA ** (jnp.arange(0, dh, 2, dtype=F32) / dh))
    ang = jnp.arange(seq, dtype=F32)[:, None] * inv[None, :]
    return jnp.cos(ang)[:, lane_freq], jnp.sin(ang)[:, lane_freq] * sign[None, :]


def _pair_rotary_layout(w):
    rows = w.shape[0]
    w = w.reshape(rows, DIL_HEADS // 2, 2, 2, HEAD_DIM // 2)
    return w.transpose(0, 1, 3, 2, 4).reshape(rows, MIX)


def _prepare(attn_norm, even_w_in, na_rpb, even_w_out, ret_w_in, ret_decay_fwd, ret_decay_bwd, ret_w_out,
             ffn_norm, ffn_w_up, ffn_conv_w, ffn_conv_b, ffn_w_down, final_norm, seq):
    lane = jnp.arange(LANES)
    half = HEAD_DIM // 2
    w_in = even_w_in[0]
    blocks = [w_in[:, i * MIX:(i + 1) * MIX] for i in range(6)]
    scale = HEAD_DIM ** -0.5 * LOG2_E
    blocks[0] = blocks[0] * scale
    blocks[3] = _pair_rotary_layout(blocks[3]) * scale
    blocks[4] = _pair_rotary_layout(blocks[4])
    p = {}
    p["even_w_in"] = jnp.concatenate(blocks, axis=1).astype(BF16)
    p["even_cos"], p["even_sin"] = _rope_tables(seq, HEAD_DIM, lane % half,
                                                jnp.where(lane < LANES // 2, -1.0, 1.0))
    p["na_bias"] = _na_bias_table(na_rpb[0], seq // GRID_W)
    p["even_w_out"] = even_w_out[0].astype(BF16)

    rw = ret_w_in[0]
    rscale = RET_QK_DIM ** -0.5
    p["ret_w_in"] = jnp.concatenate([rw[:, :RET_Q] * rscale, rw[:, RET_Q:]], axis=1).astype(BF16)
    p["ret_cos"], p["ret_sin"] = _rope_tables(seq, RET_QK_DIM, lane, jnp.ones((LANES,), F32))
    p["ret_log_gamma"] = jnp.stack([-jax.nn.softplus(ret_decay_fwd[0].astype(F32)),
                                    -jax.nn.softplus(ret_decay_bwd[0].astype(F32))], axis=0)
    p["ret_w_out"] = ret_w_out[0].astype(BF16)

    u_half = jnp.concatenate([jnp.full((FFN_DIM,), 0.5, F32), jnp.ones((FFN_DIM,), F32)])
    p["ffn_w_up"], p["ffn_conv"] = [], []
    for layer in range(2):
        p["ffn_w_up"].append(ffn_w_up[layer].astype(BF16))
        taps = jnp.concatenate([ffn_conv_w[layer], ffn_conv_b[layer][None, :]], axis=0) * u_half[None, :]
        p["ffn_conv"].append(
            jnp.concatenate([taps, jnp.zeros((SUBLANES - CONV_WIDTH - 1, 2 * FFN_DIM), F32)], axis=0))
    p["ffn_w_down"] = [ffn_w_down[layer].astype(BF16) for layer in range(2)]
    p["attn_norm"] = [attn_norm[layer][None, :] for layer in range(2)]
    p["ffn_norm"] = [ffn_norm[layer][None, :] for layer in range(2)]
    p["final_norm"] = final_norm[None, :]
    return p


def _trunk(x, p):
    b, seq, _ = x.shape
    x2d = x.reshape(b * seq, D_MODEL)
    na_qkv, dil_qkv = _even_proj(x2d, p["attn_norm"][0], p["even_w_in"], p["even_cos"], p["even_sin"], seq)
    oa = _na_attention(na_qkv.reshape(b, seq, 3 * MIX), p["na_bias"], seq).reshape(b * seq, MIX)
    ob = _dil_attention(dil_qkv, seq).reshape(b * seq, MIX)
    x2d = _even_out(x2d, oa, ob, p["even_w_out"])
    x2d = _ffn(x2d, p["ffn_norm"][0], p["ffn_w_up"][0], p["ffn_conv"][0], p["ffn_w_down"][0],
               p["final_norm"], seq, final=False)
    rproj = _ret_proj(x2d, p["attn_norm"][1], p["ret_w_in"], p["ret_cos"], p["ret_sin"], seq)
    r = _retention(rproj.reshape(b, seq, RET_IN), p["ret_log_gamma"], seq)
    x2d = _ret_out(x2d, r.reshape(b * seq, RET_V), rproj, p["ret_w_out"])
    x2d = _ffn(x2d, p["ffn_norm"][1], p["ffn_w_up"][1], p["ffn_conv"][1], p["ffn_w_down"][1],
               p["final_norm"], seq, final=True)
    return x2d.reshape(b, seq, D_MODEL)


def kernel(x_prompt, x_sample, attn_norm, even_w_in, na_rpb, even_w_out, ret_w_in, ret_decay_fwd, ret_decay_bwd,
           ret_w_out, ffn_norm, ffn_w_up, ffn_conv_w, ffn_conv_b, ffn_w_down, final_norm):
    assert x_prompt.shape[1] == x_sample.shape[1]
    p = _prepare(attn_norm, even_w_in, na_rpb, even_w_out, ret_w_in, ret_decay_fwd, ret_decay_bwd, ret_w_out,
                 ffn_norm, ffn_w_up, ffn_conv_w, ffn_conv_b, ffn_w_down, final_norm, x_prompt.shape[1])
    return _trunk(x_prompt, p), _trunk(x_sample, p)
```

```python
import functools

import jax
import jax.numpy as jnp
from jax import lax
from jax.experimental import pallas as pl
from jax.experimental.pallas import tpu as pltpu

D_MODEL = 1024
GRID_W = 64
HEAD_DIM = 64
NA_HEADS = 8
NA_WIN_R = 8
NA_WIN_C = 16
DIL_HEADS = 8
DIL_BRANCHES = ((128, 1), (512, 4), (2048, 16))
DIL_QBLOCK = 128
RET_HEADS = 4
RET_QK_DIM = 256
RET_V_DIM = 512
FFN_DIM = 2816
CONV_WIDTH = 3
ROPE_THETA = 10000.0
NORM_EPS = 1e-6
NEG_INF = -1e30

LANES = 128
SUBLANES = 8
MIX = NA_HEADS * HEAD_DIM
RET_Q = RET_HEADS * RET_QK_DIM
RET_V = RET_HEADS * RET_V_DIM
RET_IN = 2 * RET_Q + 2 * RET_V
RET_BLOCK = 256
FFN_CHUNK = 256
N_FFN_CHUNKS = FFN_DIM // FFN_CHUNK
TOKEN_TILE = 1024
OUT_TILE = 1024
FFN_TILE = 512
ATTN_GROUP = 8
NA_GROUP = 16
HALO = SUBLANES
RESIDUES = max(d for _, d in DIL_BRANCHES)
SLAB_PITCH = RESIDUES + 4
VMEM_LIMIT = 56 * 1024 * 1024
LOG2_E = 1.4426950408889634
GELU_C0 = 0.7978845608028654
GELU_C1 = GELU_C0 * 0.044715

F32 = jnp.float32
BF16 = jnp.bfloat16
NT_DIMS = (((1,), (1,)), ((), ()))
TN_DIMS = (((0,), (0,)), ((), ()))


def _params(sem, vmem=VMEM_LIMIT):
    return pltpu.CompilerParams(dimension_semantics=sem, vmem_limit_bytes=vmem)


def _resident(shape):
    nd = len(shape)
    return pl.BlockSpec(shape, lambda *_: (0,) * nd, pipeline_mode=pl.Buffered(1))


def _rms(x, g):
    ms = jnp.mean(x * x, axis=-1, keepdims=True)
    return x * lax.rsqrt(ms + NORM_EPS) * g


def _software_pipeline(n_groups, scores, softmax, values):
    assert n_groups % 2 == 0
    scores(0, 0)
    scores(1, 1)
    softmax(0, 0)

    def body(gg, carry):
        g = 2 * gg
        scores(g + 2, 0)
        softmax(g + 1, 1)
        values(g, 0)
        scores(g + 3, 1)
        softmax(g + 2, 0)
        values(g + 1, 1)
        return carry

    lax.fori_loop(0, n_groups // 2 - 1, body, 0)
    softmax(n_groups - 1, 1)
    values(n_groups - 2, 0)
    values(n_groups - 1, 1)


def _even_proj_kernel(x_ref, g_ref, w_ref, cos_ref, sin_ref, na_ref, perm_ref, slab_ref, *, tm):
    h = _rms(x_ref[...], g_ref[...]).astype(BF16)
    per_res = tm // RESIDUES
    tiles = MIX // LANES
    for c in (3, 4, 5, 0, 1, 2):
        a = jnp.dot(h, w_ref[:, c * MIX:(c + 1) * MIX], preferred_element_type=F32)
        if c in (3, 4):
            cos, sin = cos_ref[...], sin_ref[...]
            parts = []
            for j in range(tiles):
                aj = a[:, j * LANES:(j + 1) * LANES]
                parts.append(aj * cos + pltpu.roll(aj, LANES // 2, axis=1) * sin)
            a = jnp.concatenate(parts, axis=1)
        if c < 3:
            na_ref[:, c * MIX:(c + 1) * MIX] = a.astype(BF16)
        else:
            slab = slab_ref.at[c % 2]
            for j in range(tiles):
                for m in range(per_res):
                    slab[j, m * SLAB_PITCH:m * SLAB_PITCH + RESIDUES, :] = (
                        a[m * RESIDUES:(m + 1) * RESIDUES, j * LANES:(j + 1) * LANES])
            for r in range(RESIDUES):
                for j in range(tiles):
                    col = (c - 3) * MIX + j * LANES
                    rows = slab[j, pl.ds(r, per_res, stride=SLAB_PITCH), :]
                    perm_ref[r, :, col:col + LANES] = rows.astype(BF16)


def _even_proj(x2d, g, w, cos, sin, seq):
    n = x2d.shape[0]
    b = n // seq
    tm = TOKEN_TILE
    tps = seq // tm
    per_res = tm // RESIDUES
    return pl.pallas_call(
        functools.partial(_even_proj_kernel, tm=tm),
        grid=(n // tm,),
        in_specs=[
            pl.BlockSpec((tm, D_MODEL), lambda i: (i, 0)),
            _resident((1, D_MODEL)),
            _resident((D_MODEL, 6 * MIX)),
            pl.BlockSpec((tm, LANES), lambda i: (i % tps, 0)),
            pl.BlockSpec((tm, LANES), lambda i: (i % tps, 0)),
        ],
        out_specs=[pl.BlockSpec((tm, 3 * MIX), lambda i: (i, 0)),
                   pl.BlockSpec((None, RESIDUES, per_res, 3 * MIX), lambda i: (i // tps, 0, i % tps, 0))],
        out_shape=[jax.ShapeDtypeStruct((n, 3 * MIX), BF16),
                   jax.ShapeDtypeStruct((b, RESIDUES, seq // RESIDUES, 3 * MIX), BF16)],
        scratch_shapes=[pltpu.VMEM((2, MIX // LANES, per_res * SLAB_PITCH, LANES), F32)],
        compiler_params=_params(("parallel",)),
        name="even_proj",
    )(x2d, g, w, cos, sin)


def _na_kernel(q_ref, k_ref, v_ref, b_ref, o_ref, s0_ref, s1_ref, p0_ref, p1_ref, *, rows):
    lane = lax.broadcasted_iota(jnp.int32, (GRID_W, LANES), 1)
    is_a = lane < HEAD_DIM
    win = NA_WIN_R * GRID_W
    s_bufs, p_bufs = (s0_ref, s1_ref), (p0_ref, p1_ref)

    def window_start(r):
        rs = jnp.clip(r - NA_WIN_R // 2, 0, rows - NA_WIN_R)
        return pl.multiple_of(rs * GRID_W, GRID_W)

    def scores(g, slot):
        for u in range(NA_GROUP):
            r = jnp.int32(g * NA_GROUP + u)
            pat = jnp.where(r < NA_WIN_R // 2, r,
                            jnp.where(r > rows - NA_WIN_R // 2, r - (rows - NA_WIN_R), NA_WIN_R // 2))
            q = q_ref[pl.ds(pl.multiple_of(r * GRID_W, GRID_W), GRID_W), :]
            zero = jnp.zeros_like(q)
            q2 = jnp.concatenate([jnp.where(is_a, q, zero), jnp.where(is_a, zero, q)], axis=0)
            kw = k_ref[pl.ds(window_start(r), win), :]
            s = lax.dot_general(q2, kw, NT_DIMS, preferred_element_type=F32)
            s_bufs[slot][u] = s + b_ref[pat]

    def softmax(g, slot):
        for u in range(NA_GROUP):
            s = s_bufs[slot][u]
            m = jnp.max(s, axis=1, keepdims=True)
            p_bufs[slot][u] = jnp.exp2(s - m).astype(BF16)

    ones = jnp.ones((win, LANES), BF16)

    def values(g, slot):
        for u in range(NA_GROUP):
            r = jnp.int32(g * NA_GROUP + u)
            vw = jnp.concatenate([v_ref[pl.ds(window_start(r), win), :], ones], axis=1)
            pv = jnp.dot(p_bufs[slot][u], vw, preferred_element_type=F32)
            pv = pv[:, :LANES] / pv[:, LANES:]
            o_ref[pl.ds(pl.multiple_of(r * GRID_W, GRID_W), GRID_W), :] = (
                jnp.where(is_a, pv[:GRID_W], pv[GRID_W:]).astype(BF16))

    _software_pipeline(rows // NA_GROUP, scores, softmax, values)


def _na_attention(qkv, bias, seq):
    b = qkv.shape[0]
    rows = seq // GRID_W
    npair = NA_HEADS // 2
    win = NA_WIN_R * GRID_W
    blk = lambda off: pl.BlockSpec((None, seq, LANES), lambda p, i: (i, 0, off + p))
    s_buf = pltpu.VMEM((NA_GROUP, 2 * GRID_W, win), F32)
    p_buf = pltpu.VMEM((NA_GROUP, 2 * GRID_W, win), BF16)
    return pl.pallas_call(
        functools.partial(_na_kernel, rows=rows),
        grid=(npair, b),
        in_specs=[blk(0), blk(npair), blk(2 * npair),
                  pl.BlockSpec((NA_WIN_R, None, 2 * GRID_W, win), lambda p, i: (0, p, 0, 0))],
        out_specs=pl.BlockSpec((None, seq, LANES), lambda p, i: (i, 0, p)),
        out_shape=jax.ShapeDtypeStruct((b, seq, MIX), BF16),
        scratch_shapes=[s_buf, s_buf, p_buf, p_buf],
        compiler_params=_params(("parallel", "parallel")),
        name="na_attention",
    )(qkv, qkv, qkv, bias)


def _na_bias_table(rpb, rows):
    c = jnp.arange(GRID_W)
    cs = jnp.clip(c - NA_WIN_C // 2, 0, GRID_W - NA_WIN_C)
    j = jnp.arange(GRID_W)
    valid = (j[None, :] >= cs[:, None]) & (j[None, :] < cs[:, None] + NA_WIN_C)
    k = jnp.arange(2 * NA_WIN_C - 1)
    pick = (k[None, None, :] == (j[None, :, None] - c[:, None, None] + NA_WIN_C - 1)).astype(F32)
    full = jnp.einsum("hrk,cjk->hcrj", rpb.astype(F32), pick, precision=lax.Precision.HIGHEST)
    half = NA_WIN_R // 2
    reps = list(range(half)) + [half] + list(range(rows - half + 1, rows))
    tabs = []
    for r in reps:
        rs = min(max(r - half, 0), rows - NA_WIN_R)
        first = rs - r + NA_WIN_R - 1
        t = jnp.where(valid[None, :, None, :], full[:, :, first:first + NA_WIN_R, :] * LOG2_E, NEG_INF)
        tabs.append(t.reshape(NA_HEADS // 2, 2 * GRID_W, NA_WIN_R * GRID_W))
    return jnp.stack(tabs, axis=0)


def _dil_kernel(q_ref, k_ref, v_ref, o_ref, qf_ref, kf_ref, vf_ref, acc_ref, m_ref, l_ref, nat_ref, mask_ref,
                s0_ref, s1_ref, p0_ref, p1_ref, m0_ref, m1_ref, *, slab_len):
    qb, kw_rows = DIL_QBLOCK, 2 * DIL_QBLOCK
    s_bufs, p_bufs, m_bufs = (s0_ref, s1_ref), (p0_ref, p1_ref), (m0_ref, m1_ref)
    ones = jnp.ones((kw_rows, LANES), BF16)
    lane = lax.broadcasted_iota(jnp.int32, (qb, LANES), 1)
    is_a_out = lane < HEAD_DIM
    is_a_rot = (lane & (HEAD_DIM - 1)) < HEAD_DIM // 2
    row = lax.broadcasted_iota(jnp.int32, (2 * qb, kw_rows), 0) & (qb - 1)
    col = lax.broadcasted_iota(jnp.int32, (2 * qb, kw_rows), 1)

    qf_ref[...] = q_ref[...].astype(F32)
    kf_ref[...] = k_ref[...].astype(F32)
    vf_ref[...] = v_ref[...].astype(F32)

    branches = sorted(DIL_BRANCHES, key=lambda wd: -wd[1])

    @pl.when((pl.program_id(0) == 0) & (pl.program_id(1) == 0))
    def _():
        for bi, (window, d) in enumerate(branches):
            g_cnt = RESIDUES // d
            rq, rk = qb // g_cnt, kw_rows // g_cnt
            base = g_cnt * ((col % rk) - (row % rq)) + (col // rk) - (row // rq)
            for pat, off in enumerate((0, (rq - rk) // 2, rq - rk)):
                on_band = jnp.abs(base + g_cnt * off) <= (window // 2) // d
                mask_ref[3 * bi + pat] = jnp.where(on_band, jnp.inf, NEG_INF)

    for bi, (window, d) in enumerate(branches):
        g_cnt = RESIDUES // d
        rq, rk = qb // g_cnt, kw_rows // g_cnt
        lead = (rk - rq) // 2
        nblk = slab_len // rq

        def tile(g, u, d=d, g_cnt=g_cnt, rq=rq, rk=rk, lead=lead, nblk=nblk):
            it = jnp.int32(g * ATTN_GROUP + u)
            i = it % nblk
            q0 = i * rq
            ws = jnp.clip(q0 - lead, 0, slab_len - rk)
            pat = jnp.where(i == 0, 0, jnp.where(i == nblk - 1, 2, 1))
            slabs = [it // nblk + d * g_ for g_ in range(g_cnt)]
            return slabs, q0, pl.multiple_of(q0, rq), ws, pat

        def window(ref, ref32, slabs, ws, rq=rq, rk=rk):
            if rq % 16 == 0:
                wsa = pl.multiple_of(ws, 16)
                return jnp.concatenate([ref[s_, pl.ds(wsa, rk), :] for s_ in slabs], axis=0)
            return jnp.concatenate([ref32[s_, pl.ds(ws, rk), :] for s_ in slabs], axis=0).astype(BF16)

        def scores(g, slot, bi=bi, rq=rq, tile=tile, window=window):
            for u in range(ATTN_GROUP):
                slabs, _, q0a, ws, pat = tile(g, u)
                if rq % 16 == 0:
                    q = jnp.concatenate([q_ref[s_, pl.ds(q0a, rq), :] for s_ in slabs], axis=0)
                else:
                    q = jnp.concatenate([qf_ref[s_, pl.ds(q0a, rq), :] for s_ in slabs], axis=0).astype(BF16)
                kw = window(k_ref, kf_ref, slabs, ws)
                zero = jnp.zeros_like(q)
                q2 = jnp.concatenate([jnp.where(is_a_rot, q, zero), jnp.where(is_a_rot, zero, q)], axis=0)
                s = lax.dot_general(q2, kw, NT_DIMS, preferred_element_type=F32)
                s_bufs[slot][u] = jnp.minimum(s, mask_ref[3 * bi + pat])

        def softmax(g, slot):
            for u in range(ATTN_GROUP):
                s = s_bufs[slot][u]
                m = jnp.max(s, axis=1, keepdims=True)
                p_bufs[slot][u] = jnp.exp2(s - m).astype(BF16)
                m_bufs[slot][u] = jnp.where(is_a_out, m[:qb], m[qb:])

        def values(g, slot, bi=bi, rq=rq, tile=tile, window=window):
            for u in range(ATTN_GROUP):
                slabs, q0, q0a, ws, _ = tile(g, u)
                vw = jnp.concatenate([window(v_ref, vf_ref, slabs, ws), ones], axis=1)
                pv = jnp.dot(p_bufs[slot][u], vw, preferred_element_type=F32)
                pv_t = jnp.where(is_a_out, pv[:qb, :LANES], pv[qb:, :LANES])
                l_t = jnp.where(is_a_out, pv[:qb, LANES:], pv[qb:, LANES:])
                m_t = m_bufs[slot][u]
                if bi > 0:
                    m_old = jnp.concatenate([m_ref[s_, pl.ds(q0a, rq), :] for s_ in slabs], axis=0)
                    l_old = jnp.concatenate([l_ref[s_, pl.ds(q0a, rq), :] for s_ in slabs], axis=0)
                    a_old = jnp.concatenate([acc_ref[s_, pl.ds(q0a, rq), :] for s_ in slabs], axis=0)
                    m_new = jnp.maximum(m_old, m_t)
                    w_old = jnp.exp2(m_old - m_new)
                    w_new = jnp.exp2(m_t - m_new)
                    l_t = l_old * w_old + l_t * w_new
                    pv_t = a_old * w_old + pv_t * w_new
                    m_t = m_new
                if bi < len(branches) - 1:
                    for g_, s_ in enumerate(slabs):
                        rows = slice(g_ * rq, (g_ + 1) * rq)
                        m_ref[s_, pl.ds(q0a, rq), :] = m_t[rows]
                        l_ref[s_, pl.ds(q0a, rq), :] = l_t[rows]
                        acc_ref[s_, pl.ds(q0a, rq), :] = pv_t[rows]
                else:
                    out = pv_t / l_t
                    t0 = q0 * RESIDUES
                    for g_, s_ in enumerate(slabs):
                        nat_ref[pl.ds(t0 + s_, rq, stride=RESIDUES), :] = out[g_ * rq:(g_ + 1) * rq]
                    t0a = pl.multiple_of(t0, qb)
                    o_ref[pl.ds(t0a, qb), :] = nat_ref[pl.ds(t0a, qb), :].astype(BF16)

        _software_pipeline(d * nblk // ATTN_GROUP, scores, softmax, values)


def _dil_attention(perm, seq):
    b = perm.shape[0]
    slab_len = seq // RESIDUES
    npair = DIL_HEADS // 2
    blk = lambda off: pl.BlockSpec((None, RESIDUES, slab_len, LANES), lambda i, p: (i, 0, 0, off + p))
    slab = pltpu.VMEM((RESIDUES, slab_len, LANES), F32)
    s_buf = pltpu.VMEM((ATTN_GROUP, 2 * DIL_QBLOCK, 2 * DIL_QBLOCK), F32)
    p_buf = pltpu.VMEM((ATTN_GROUP, 2 * DIL_QBLOCK, 2 * DIL_QBLOCK), BF16)
    stat = pltpu.VMEM((ATTN_GROUP, DIL_QBLOCK, LANES), F32)
    return pl.pallas_call(
        functools.partial(_dil_kernel, slab_len=slab_len),
        grid=(b, npair),
        in_specs=[blk(0), blk(npair), blk(2 * npair)],
        out_specs=pl.BlockSpec((None, seq, LANES), lambda i, p: (i, 0, p)),
        out_shape=jax.ShapeDtypeStruct((b, seq, MIX), BF16),
        scratch_shapes=[slab, slab, slab, slab, slab, slab,
                        pltpu.VMEM((seq, LANES), F32),
                        pltpu.VMEM((3 * len(DIL_BRANCHES), 2 * DIL_QBLOCK, 2 * DIL_QBLOCK), F32),
                        s_buf, s_buf, p_buf, p_buf, stat, stat],
        compiler_params=_params(("arbitrary", "arbitrary")),
        name="dilated_attention",
    )(perm, perm, perm)


def _even_out_kernel(x_ref, oa_ref, ob_ref, w_ref, y_ref):
    o = jnp.concatenate([oa_ref[...], ob_ref[...]], axis=1)
    y_ref[...] = x_ref[...] + jnp.dot(o, w_ref[...], preferred_element_type=F32)


def _even_out(x2d, oa, ob, w):
    n = x2d.shape[0]
    tm = OUT_TILE
    row = lambda width: pl.BlockSpec((tm, width), lambda i: (i, 0))
    return pl.pallas_call(
        _even_out_kernel,
        grid=(n // tm,),
        in_specs=[row(D_MODEL), row(MIX), row(MIX), _resident((2 * MIX, D_MODEL))],
        out_specs=row(D_MODEL),
        out_shape=jax.ShapeDtypeStruct((n, D_MODEL), F32),
        compiler_params=_params(("parallel",)),
        name="even_out",
    )(x2d, oa, ob, w)


def _ffn_kernel(x_ref, xn_ref, xp_ref, g_ref, wu_ref, cp_ref, wd_ref, fg_ref, o_ref, h_ref, a_ref, act_ref, *,
                tm, tiles_per_seq, final):
    pos = pl.program_id(0) % tiles_per_seq
    g = g_ref[...]
    keep_prev = jnp.where(pos == 0, 0.0, 1.0)
    keep_next = jnp.where(pos == tiles_per_seq - 1, 0.0, 1.0)
    h_ref[0:tm, :] = _rms(x_ref[...], g).astype(BF16)
    halo = jnp.concatenate([_rms(xn_ref[...], g) * keep_next, _rms(xp_ref[...], g) * keep_prev], axis=0)
    h_ref[tm:tm + 2 * HALO, :] = halo.astype(BF16)
    tiles = FFN_CHUNK // LANES
    for c in range(N_FFN_CHUNKS):
        buf = a_ref.at[c % 2]
        ys = []
        for part in range(2):
            first = part * FFN_DIM + c * FFN_CHUNK
            a = jnp.dot(h_ref[...], wu_ref[:, first:first + FFN_CHUNK], preferred_element_type=F32)
            for j in range(tiles):
                cols = slice(j * LANES, (j + 1) * LANES)
                buf[part * tiles + j, HALO:tm + 2 * HALO, :] = a[:tm + HALO, cols]
                buf[part * tiles + j, 0:HALO, :] = a[tm + HALO:, cols]
        for part in range(2):
            for j in range(tiles):
                slab = buf.at[part * tiles + j]
                first = part * FFN_DIM + c * FFN_CHUNK + j * LANES
                cols = slice(first, first + LANES)
                y = cp_ref[3:4, cols] + slab[HALO - 1:HALO - 1 + tm, :] * cp_ref[0:1, cols]
                y = y + slab[HALO:HALO + tm, :] * cp_ref[1:2, cols]
                y = y + slab[HALO + 1:HALO + 1 + tm, :] * cp_ref[2:3, cols]
                ys.append(y)
        half_u = jnp.concatenate(ys[:tiles], axis=1)
        gate = jnp.concatenate(ys[tiles:], axis=1)
        t = jnp.tanh(gate * (GELU_C0 + GELU_C1 * (gate * gate)))
        w = half_u * gate
        act_ref[:, c * FFN_CHUNK:(c + 1) * FFN_CHUNK] = (w + w * t).astype(BF16)
    out = x_ref[...] + jnp.dot(act_ref[...], wd_ref[...], preferred_element_type=F32)
    if final:
        out = _rms(out, fg_ref[...])
    o_ref[...] = out


def _ffn(x2d, g, wu, cp, wd, fg, seq, final):
    n = x2d.shape[0]
    tm = FFN_TILE
    tps = seq // tm
    hb = tm // HALO
    last = n // HALO - 1
    return pl.pallas_call(
        functools.partial(_ffn_kernel, tm=tm, tiles_per_seq=tps, final=final),
        grid=(n // tm,),
        in_specs=[
            pl.BlockSpec((tm, D_MODEL), lambda i: (i, 0)),
            pl.BlockSpec((HALO, D_MODEL), lambda i: (jnp.minimum((i + 1) * hb, last), 0)),
            pl.BlockSpec((HALO, D_MODEL), lambda i: (jnp.maximum(i * hb - 1, 0), 0)),
            _resident((1, D_MODEL)),
            _resident((D_MODEL, 2 * FFN_DIM)),
            _resident((SUBLANES, 2 * FFN_DIM)),
            _resident((FFN_DIM, D_MODEL)),
            _resident((1, D_MODEL)),
        ],
        out_specs=pl.BlockSpec((tm, D_MODEL), lambda i: (i, 0)),
        out_shape=jax.ShapeDtypeStruct((n, D_MODEL), F32),
        scratch_shapes=[pltpu.VMEM((tm + 2 * HALO, D_MODEL), BF16),
                        pltpu.VMEM((2, 2 * FFN_CHUNK // LANES, tm + 2 * HALO, LANES), F32),
                        pltpu.VMEM((tm, FFN_DIM), BF16)],
        compiler_params=_params(("parallel",)),
        name="conv_ffn",
    )(x2d, x2d, x2d, g, wu, cp, wd, fg)


def _ret_proj_kernel(x_ref, g_ref, w_ref, cos_ref, sin_ref, o_ref):
    h = _rms(x_ref[...], g_ref[...]).astype(BF16)
    for c in range(RET_IN // MIX):
        a = jnp.dot(h, w_ref[:, c * MIX:(c + 1) * MIX], preferred_element_type=F32)
        if c < 2 * RET_Q // MIX:
            cos, sin = cos_ref[...], sin_ref[...]
            parts = []
            for j in range(MIX // RET_QK_DIM):
                x1 = a[:, j * RET_QK_DIM:j * RET_QK_DIM + LANES]
                x2 = a[:, j * RET_QK_DIM + LANES:(j + 1) * RET_QK_DIM]
                parts += [x1 * cos - x2 * sin, x1 * sin + x2 * cos]
            a = jnp.concatenate(parts, axis=1)
        elif c >= (2 * RET_Q + RET_V) // MIX:
            half_gate = a * 0.5
            a = half_gate + half_gate * jnp.tanh(half_gate)
        o_ref[:, c * MIX:(c + 1) * MIX] = a.astype(BF16)


def _ret_proj(x2d, g, w, cos, sin, seq):
    n = x2d.shape[0]
    tm = TOKEN_TILE
    tps = seq // tm
    return pl.pallas_call(
        _ret_proj_kernel,
        grid=(n // tm,),
        in_specs=[
            pl.BlockSpec((tm, D_MODEL), lambda i: (i, 0)),
            _resident((1, D_MODEL)),
            _resident((D_MODEL, RET_IN)),
            pl.BlockSpec((tm, LANES), lambda i: (i % tps, 0)),
            pl.BlockSpec((tm, LANES), lambda i: (i % tps, 0)),
        ],
        out_specs=pl.BlockSpec((tm, RET_IN), lambda i: (i, 0)),
        out_shape=jax.ShapeDtypeStruct((n, RET_IN), BF16),
        compiler_params=_params(("parallel",)),
        name="ret_proj",
    )(x2d, g, w, cos, sin)


def _ret_kernel(lg_ref, q_ref, k_ref, v_ref, out_ref, o_ref, sf_ref, sb_ref, d_ref, *, seq):
    c = RET_BLOCK
    n = seq // c
    hd = pl.program_id(1)
    lgf = lg_ref[0, hd]
    lgb = lg_ref[1, hd]
    pr = lax.broadcasted_iota(jnp.int32, (c, c), 0).astype(F32)
    pc = lax.broadcasted_iota(jnp.int32, (c, c), 1).astype(F32)
    diff = pr - pc
    lower = diff >= 0
    d_f = jnp.where(lower, jnp.exp(lgf * jnp.where(lower, diff, 0.0)), 0.0)
    d_b = jnp.where(lower, 0.0, jnp.exp(lgb * jnp.where(lower, 0.0, -diff)))
    d_ref[...] = d_f + d_b
    pos = lax.broadcasted_iota(jnp.int32, (c, 1), 0).astype(F32)
    qd_f = jnp.exp(lgf * (pos + 1.0))
    kd_f = jnp.exp(lgf * (c - 1.0 - pos))
    qd_b = jnp.exp(lgb * (c - pos))
    kd_b = jnp.exp(lgb * pos)
    ones = jnp.ones((1, LANES), F32)
    sd_f = jnp.exp(lgf * c * ones)[:, :1]
    sd_b = jnp.exp(lgb * c * ones)[:, :1]

    def emit(r0, o, accumulate):
        if not accumulate:
            o_ref[pl.ds(r0, c), :] = o
            return
        out_ref[pl.ds(r0, c), :] = (o_ref[pl.ds(r0, c), :] + o).astype(BF16)

    def fwd_part(i, accumulate):
        r0 = pl.multiple_of(i * c, c)
        q = q_ref[pl.ds(r0, c), :]
        k = k_ref[pl.ds(r0, c), :]
        v = v_ref[pl.ds(r0, c), :]
        s = lax.dot_general(q, k, NT_DIMS, preferred_element_type=F32)
        inner = (s * d_ref[...]).astype(BF16)
        qd = (q.astype(F32) * qd_f).astype(BF16)
        o = (jnp.dot(inner, v, preferred_element_type=F32)
             + jnp.dot(qd, sf_ref[...].astype(BF16), preferred_element_type=F32))
        emit(r0, o, accumulate)
        kd = (k.astype(F32) * kd_f).astype(BF16)
        sf_ref[...] = sf_ref[...] * sd_f + lax.dot_general(kd, v, TN_DIMS, preferred_element_type=F32)

    def bwd_part(i, accumulate):
        r0 = pl.multiple_of(i * c, c)
        q = q_ref[pl.ds(r0, c), :]
        k = k_ref[pl.ds(r0, c), :]
        v = v_ref[pl.ds(r0, c), :]
        qd = (q.astype(F32) * qd_b).astype(BF16)
        o = jnp.dot(qd, sb_ref[...].astype(BF16), preferred_element_type=F32)
        emit(r0, o, accumulate)
        kd = (k.astype(F32) * kd_b).astype(BF16)
        sb_ref[...] = sb_ref[...] * sd_b + lax.dot_general(kd, v, TN_DIMS, preferred_element_type=F32)

    sf_ref[...] = jnp.zeros_like(sf_ref)
    sb_ref[...] = jnp.zeros_like(sb_ref)

    def first_half(j, carry):
        fwd_part(j, False)
        bwd_part(n - 1 - j, False)
        return carry

    def second_half(j, carry):
        fwd_part(j, True)
        bwd_part(n - 1 - j, True)
        return carry

    lax.fori_loop(0, n // 2, first_half, 0, unroll=True)
    lax.fori_loop(n // 2, n, second_half, 0, unroll=True)


def _retention(qkv, log_gamma, seq):
    b = qkv.shape[0]
    kblk = RET_Q // RET_QK_DIM
    vblk = 2 * RET_Q // RET_V_DIM
    state = pltpu.VMEM((RET_QK_DIM, RET_V_DIM), F32)
    return pl.pallas_call(
        functools.partial(_ret_kernel, seq=seq),
        grid=(b, RET_HEADS),
        in_specs=[
            pl.BlockSpec(memory_space=pltpu.SMEM),
            pl.BlockSpec((None, seq, RET_QK_DIM), lambda i, h: (i, 0, h)),
            pl.BlockSpec((None, seq, RET_QK_DIM), lambda i, h: (i, 0, kblk + h)),
            pl.BlockSpec((None, seq, RET_V_DIM), lambda i, h: (i, 0, vblk + h)),
        ],
        out_specs=pl.BlockSpec((None, seq, RET_V_DIM), lambda i, h: (i, 0, h)),
        out_shape=jax.ShapeDtypeStruct((b, seq, RET_V), BF16),
        scratch_shapes=[pltpu.VMEM((seq, RET_V_DIM), F32), state, state,
                        pltpu.VMEM((RET_BLOCK, RET_BLOCK), F32)],
        compiler_params=_params(("parallel", "parallel")),
        name="retention",
    )(log_gamma, qkv, qkv, qkv)


def _ret_out_kernel(x_ref, r_ref, g_ref, w_ref, y_ref, *, tm):
    for r0 in range(0, tm, tm // 2):
        rows = slice(r0, r0 + tm // 2)
        parts = []
        for hd in range(RET_HEADS):
            cols = slice(hd * RET_V_DIM, (hd + 1) * RET_V_DIM)
            r = r_ref[rows, cols].astype(F32)
            mu = jnp.mean(r, axis=-1, keepdims=True)
            var = jnp.mean(jnp.square(r - mu), axis=-1, keepdims=True)
            rn = (r - mu) * lax.rsqrt(var + NORM_EPS)
            parts.append((g_ref[rows, cols].astype(F32) * rn).astype(BF16))
        act = jnp.concatenate(parts, axis=1)
        y_ref[rows, :] = x_ref[rows, :] + jnp.dot(act, w_ref[...], preferred_element_type=F32)


def _ret_out(x2d, r, proj, w):
    n = x2d.shape[0]
    tm = OUT_TILE
    row = lambda width: pl.BlockSpec((tm, width), lambda i: (i, 0))
    gate_blk = (2 * RET_Q + RET_V) // RET_V
    return pl.pallas_call(
        functools.partial(_ret_out_kernel, tm=tm),
        grid=(n // tm,),
        in_specs=[row(D_MODEL), row(RET_V), pl.BlockSpec((tm, RET_V), lambda i: (i, gate_blk)),
                  _resident((RET_V, D_MODEL))],
        out_specs=row(D_MODEL),
        out_shape=jax.ShapeDtypeStruct((n, D_MODEL), F32),
        compiler_params=_params(("parallel",)),
        name="ret_out",
    )(x2d, r, proj, w)


def _rope_tables(seq, dh, lane_freq, sign):
    inv = 1.0 / (ROPE_THETA ** (jnp.arange(0, dh, 2, dtype=F32) / dh))
    ang = jnp.arange(seq, dtype=F32)[:, None] * inv[None, :]
    return jnp.cos(ang)[:, lane_freq], jnp.sin(ang)[:, lane_freq] * sign[None, :]


def _pair_rotary_layout(w):
    rows = w.shape[0]
    w = w.reshape(rows, DIL_HEADS // 2, 2, 2, HEAD_DIM // 2)
    return w.transpose(0, 1, 3, 2, 4).reshape(rows, MIX)


def _prepare(attn_norm, even_w_in, na_rpb, even_w_out, ret_w_in, ret_decay_fwd, ret_decay_bwd, ret_w_out,
             ffn_norm, ffn_w_up, ffn_conv_w, ffn_conv_b, ffn_w_down, final_norm, seq):
    lane = jnp.arange(LANES)
    half = HEAD_DIM // 2
    w_in = even_w_in[0]
    blocks = [w_in[:, i * MIX:(i + 1) * MIX] for i in range(6)]
    scale = HEAD_DIM ** -0.5 * LOG2_E
    blocks[0] = blocks[0] * scale
    blocks[3] = _pair_rotary_layout(blocks[3]) * scale
    blocks[4] = _pair_rotary_layout(blocks[4])
    p = {}
    p["even_w_in"] = jnp.concatenate(blocks, axis=1).astype(BF16)
    p["even_cos"], p["even_sin"] = _rope_tables(seq, HEAD_DIM, lane % half,
                                                jnp.where(lane < LANES // 2, -1.0, 1.0))
    p["na_bias"] = _na_bias_table(na_rpb[0], seq // GRID_W)
    p["even_w_out"] = even_w_out[0].astype(BF16)

    rw = ret_w_in[0]
    rscale = RET_QK_DIM ** -0.5
    p["ret_w_in"] = jnp.concatenate([rw[:, :RET_Q] * rscale, rw[:, RET_Q:]], axis=1).astype(BF16)
    p["ret_cos"], p["ret_sin"] = _rope_tables(seq, RET_QK_DIM, lane, jnp.ones((LANES,), F32))
    p["ret_log_gamma"] = jnp.stack([-jax.nn.softplus(ret_decay_fwd[0].astype(F32)),
                                    -jax.nn.softplus(ret_decay_bwd[0].astype(F32))], axis=0)
    p["ret_w_out"] = ret_w_out[0].astype(BF16)

    u_half = jnp.concatenate([jnp.full((FFN_DIM,), 0.5, F32), jnp.ones((FFN_DIM,), F32)])
    p["ffn_w_up"], p["ffn_conv"] = [], []
    for layer in range(2):
        p["ffn_w_up"].append(ffn_w_up[layer].astype(BF16))
        taps = jnp.concatenate([ffn_conv_w[layer], ffn_conv_b[layer][None, :]], axis=0) * u_half[None, :]
        p["ffn_conv"].append(
            jnp.concatenate([taps, jnp.zeros((SUBLANES - CONV_WIDTH - 1, 2 * FFN_DIM), F32)], axis=0))
    p["ffn_w_down"] = [ffn_w_down[layer].astype(BF16) for layer in range(2)]
    p["attn_norm"] = [attn_norm[layer][None, :] for layer in range(2)]
    p["ffn_norm"] = [ffn_norm[layer][None, :] for layer in range(2)]
    p["final_norm"] = final_norm[None, :]
    return p


def _trunk(x, p):
    b, seq, _ = x.shape
    x2d = x.reshape(b * seq, D_MODEL)
    na_qkv, dil_qkv = _even_proj(x2d, p["attn_norm"][0], p["even_w_in"], p["even_cos"], p["even_sin"], seq)
    oa = _na_attention(na_qkv.reshape(b, seq, 3 * MIX), p["na_bias"], seq).reshape(b * seq, MIX)
    ob = _dil_attention(dil_qkv, seq).reshape(b * seq, MIX)
    x2d = _even_out(x2d, oa, ob, p["even_w_out"])
    x2d = _ffn(x2d, p["ffn_norm"][0], p["ffn_w_up"][0], p["ffn_conv"][0], p["ffn_w_down"][0],
               p["final_norm"], seq, final=False)
    rproj = _ret_proj(x2d, p["attn_norm"][1], p["ret_w_in"], p["ret_cos"], p["ret_sin"], seq)
    r = _retention(rproj.reshape(b, seq, RET_IN), p["ret_log_gamma"], seq)
    x2d = _ret_out(x2d, r.reshape(b * seq, RET_V), rproj, p["ret_w_out"])
    x2d = _ffn(x2d, p["ffn_norm"][1], p["ffn_w_up"][1], p["ffn_conv"][1], p["ffn_w_down"][1],
               p["final_norm"], seq, final=True)
    return x2d.reshape(b, seq, D_MODEL)


def kernel(x_prompt, x_sample, attn_norm, even_w_in, na_rpb, even_w_out, ret_w_in, ret_decay_fwd, ret_decay_bwd,
           ret_w_out, ffn_norm, ffn_w_up, ffn_conv_w, ffn_conv_b, ffn_w_down, final_norm):
    assert x_prompt.shape[1] == x_sample.shape[1]
    p = _prepare(attn_norm, even_w_in, na_rpb, even_w_out, ret_w_in, ret_decay_fwd, ret_decay_bwd, ret_w_out,
                 ffn_norm, ffn_w_up, ffn_conv_w, ffn_conv_b, ffn_w_down, final_norm, x_prompt.shape[1])
    return _trunk(x_prompt, p), _trunk(x_sample, p)
```

```python
import functools

import jax
import jax.numpy as jnp
from jax import lax
from jax.experimental import pallas as pl
from jax.experimental.pallas import tpu as pltpu

D_MODEL = 1024
GRID_W = 64
HEAD_DIM = 64
NA_HEADS = 8
NA_WIN_R = 8
NA_WIN_C = 16
DIL_HEADS = 8
DIL_BRANCHES = ((128, 1), (512, 4), (2048, 16))
DIL_QBLOCK = 128
RET_HEADS = 4
RET_QK_DIM = 256
RET_V_DIM = 512
FFN_DIM = 2816
CONV_WIDTH = 3
ROPE_THETA = 10000.0
NORM_EPS = 1e-6
NEG_INF = -1e30

LANES = 128
SUBLANES = 8
MIX = NA_HEADS * HEAD_DIM
RET_Q = RET_HEADS * RET_QK_DIM
RET_V = RET_HEADS * RET_V_DIM
RET_IN = 2 * RET_Q + 2 * RET_V
RET_BLOCK = 256
FFN_CHUNK = 256
N_FFN_CHUNKS = FFN_DIM // FFN_CHUNK
TOKEN_TILE = 1024
OUT_TILE = 1024
FFN_TILE = 512
ATTN_GROUP = 16
NA_GROUP = 16
HALO = SUBLANES
RESIDUES = max(d for _, d in DIL_BRANCHES)
SLAB_PITCH = RESIDUES + 4
VMEM_LIMIT = 56 * 1024 * 1024
LOG2_E = 1.4426950408889634
GELU_C0 = 0.7978845608028654
GELU_C1 = GELU_C0 * 0.044715

F32 = jnp.float32
BF16 = jnp.bfloat16
NT_DIMS = (((1,), (1,)), ((), ()))
TN_DIMS = (((0,), (0,)), ((), ()))


def _params(sem, vmem=VMEM_LIMIT):
    return pltpu.CompilerParams(dimension_semantics=sem, vmem_limit_bytes=vmem)


def _resident(shape):
    nd = len(shape)
    return pl.BlockSpec(shape, lambda *_: (0,) * nd, pipeline_mode=pl.Buffered(1))


def _rms(x, g):
    ms = jnp.mean(x * x, axis=-1, keepdims=True)
    return x * lax.rsqrt(ms + NORM_EPS) * g


def _software_pipeline(n_groups, scores, softmax, values):
    assert n_groups % 2 == 0
    scores(0, 0)
    scores(1, 1)
    softmax(0, 0)

    def body(gg, carry):
        g = 2 * gg
        scores(g + 2, 0)
        softmax(g + 1, 1)
        values(g, 0)
        scores(g + 3, 1)
        softmax(g + 2, 0)
        values(g + 1, 1)
        return carry

    lax.fori_loop(0, n_groups // 2 - 1, body, 0)
    softmax(n_groups - 1, 1)
    values(n_groups - 2, 0)
    values(n_groups - 1, 1)


def _even_proj_kernel(x_ref, g_ref, w_ref, cos_ref, sin_ref, na_ref, perm_ref, slab_ref, *, tm):
    h = _rms(x_ref[...], g_ref[...]).astype(BF16)
    per_res = tm // RESIDUES
    tiles = MIX // LANES
    for c in (3, 4, 5, 0, 1, 2):
        a = jnp.dot(h, w_ref[:, c * MIX:(c + 1) * MIX], preferred_element_type=F32)
        if c in (3, 4):
            cos, sin = cos_ref[...], sin_ref[...]
            parts = []
            for j in range(tiles):
                aj = a[:, j * LANES:(j + 1) * LANES]
                parts.append(aj * cos + pltpu.roll(aj, LANES // 2, axis=1) * sin)
            a = jnp.concatenate(parts, axis=1)
        if c < 3:
            na_ref[:, c * MIX:(c + 1) * MIX] = a.astype(BF16)
        else:
            slab = slab_ref.at[c % 2]
            for j in range(tiles):
                for m in range(per_res):
                    slab[j, m * SLAB_PITCH:m * SLAB_PITCH + RESIDUES, :] = (
                        a[m * RESIDUES:(m + 1) * RESIDUES, j * LANES:(j + 1) * LANES])
            for r in range(RESIDUES):
                for j in range(tiles):
                    col = (c - 3) * MIX + j * LANES
                    rows = slab[j, pl.ds(r, per_res, stride=SLAB_PITCH), :]
                    perm_ref[r, :, col:col + LANES] = rows.astype(BF16)


def _even_proj(x2d, g, w, cos, sin, seq):
    n = x2d.shape[0]
    b = n // seq
    tm = TOKEN_TILE
    tps = seq // tm
    per_res = tm // RESIDUES
    return pl.pallas_call(
        functools.partial(_even_proj_kernel, tm=tm),
        grid=(n // tm,),
        in_specs=[
            pl.BlockSpec((tm, D_MODEL), lambda i: (i, 0)),
            _resident((1, D_MODEL)),
            _resident((D_MODEL, 6 * MIX)),
            pl.BlockSpec((tm, LANES), lambda i: (i % tps, 0)),
            pl.BlockSpec((tm, LANES), lambda i: (i % tps, 0)),
        ],
        out_specs=[pl.BlockSpec((tm, 3 * MIX), lambda i: (i, 0)),
                   pl.BlockSpec((None, RESIDUES, per_res, 3 * MIX), lambda i: (i // tps, 0, i % tps, 0))],
        out_shape=[jax.ShapeDtypeStruct((n, 3 * MIX), BF16),
                   jax.ShapeDtypeStruct((b, RESIDUES, seq // RESIDUES, 3 * MIX), BF16)],
        scratch_shapes=[pltpu.VMEM((2, MIX // LANES, per_res * SLAB_PITCH, LANES), F32)],
        compiler_params=_params(("parallel",)),
        name="even_proj",
    )(x2d, g, w, cos, sin)


def _na_kernel(q_ref, k_ref, v_ref, b_ref, o_ref, s0_ref, s1_ref, p0_ref, p1_ref, *, rows):
    lane = lax.broadcasted_iota(jnp.int32, (GRID_W, LANES), 1)
    is_a = lane < HEAD_DIM
    win = NA_WIN_R * GRID_W
    s_bufs, p_bufs = (s0_ref, s1_ref), (p0_ref, p1_ref)

    def window_start(r):
        rs = jnp.clip(r - NA_WIN_R // 2, 0, rows - NA_WIN_R)
        return pl.multiple_of(rs * GRID_W, GRID_W)

    def scores(g, slot):
        for u in range(NA_GROUP):
            r = jnp.int32(g * NA_GROUP + u)
            pat = jnp.where(r < NA_WIN_R // 2, r,
                            jnp.where(r > rows - NA_WIN_R // 2, r - (rows - NA_WIN_R), NA_WIN_R // 2))
            q = q_ref[pl.ds(pl.multiple_of(r * GRID_W, GRID_W), GRID_W), :]
            zero = jnp.zeros_like(q)
            q2 = jnp.concatenate([jnp.where(is_a, q, zero), jnp.where(is_a, zero, q)], axis=0)
            kw = k_ref[pl.ds(window_start(r), win), :]
            s = lax.dot_general(q2, kw, NT_DIMS, preferred_element_type=F32)
            s_bufs[slot][u] = s + b_ref[pat]

    def softmax(g, slot):
        for u in range(NA_GROUP):
            s = s_bufs[slot][u]
            m = jnp.max(s, axis=1, keepdims=True)
            p_bufs[slot][u] = jnp.exp2(s - m).astype(BF16)

    ones = jnp.ones((win, LANES), BF16)

    def values(g, slot):
        for u in range(NA_GROUP):
            r = jnp.int32(g * NA_GROUP + u)
            vw = jnp.concatenate([v_ref[pl.ds(window_start(r), win), :], ones], axis=1)
            pv = jnp.dot(p_bufs[slot][u], vw, preferred_element_type=F32)
            pv = pv[:, :LANES] / pv[:, LANES:]
            o_ref[pl.ds(pl.multiple_of(r * GRID_W, GRID_W), GRID_W), :] = (
                jnp.where(is_a, pv[:GRID_W], pv[GRID_W:]).astype(BF16))

    _software_pipeline(rows // NA_GROUP, scores, softmax, values)


def _na_attention(qkv, bias, seq):
    b = qkv.shape[0]
    rows = seq // GRID_W
    npair = NA_HEADS // 2
    win = NA_WIN_R * GRID_W
    blk = lambda off: pl.BlockSpec((None, seq, LANES), lambda p, i: (i, 0, off + p))
    s_buf = pltpu.VMEM((NA_GROUP, 2 * GRID_W, win), F32)
    p_buf = pltpu.VMEM((NA_GROUP, 2 * GRID_W, win), BF16)
    return pl.pallas_call(
        functools.partial(_na_kernel, rows=rows),
        grid=(npair, b),
        in_specs=[blk(0), blk(npair), blk(2 * npair),
                  pl.BlockSpec((NA_WIN_R, None, 2 * GRID_W, win), lambda p, i: (0, p, 0, 0))],
        out_specs=pl.BlockSpec((None, seq, LANES), lambda p, i: (i, 0, p)),
        out_shape=jax.ShapeDtypeStruct((b, seq, MIX), BF16),
        scratch_shapes=[s_buf, s_buf, p_buf, p_buf],
        compiler_params=_params(("parallel", "parallel")),
        name="na_attention",
    )(qkv, qkv, qkv, bias)


def _na_bias_table(rpb, rows):
    c = jnp.arange(GRID_W)
    cs = jnp.clip(c - NA_WIN_C // 2, 0, GRID_W - NA_WIN_C)
    j = jnp.arange(GRID_W)
    valid = (j[None, :] >= cs[:, None]) & (j[None, :] < cs[:, None] + NA_WIN_C)
    k = jnp.arange(2 * NA_WIN_C - 1)
    pick = (k[None, None, :] == (j[None, :, None] - c[:, None, None] + NA_WIN_C - 1)).astype(F32)
    full = jnp.einsum("hrk,cjk->hcrj", rpb.astype(F32), pick, precision=lax.Precision.HIGHEST)
    half = NA_WIN_R // 2
    reps = list(range(half)) + [half] + list(range(rows - half + 1, rows))
    tabs = []
    for r in reps:
        rs = min(max(r - half, 0), rows - NA_WIN_R)
        first = rs - r + NA_WIN_R - 1
        t = jnp.where(valid[None, :, None, :], full[:, :, first:first + NA_WIN_R, :] * LOG2_E, NEG_INF)
        tabs.append(t.reshape(NA_HEADS // 2, 2 * GRID_W, NA_WIN_R * GRID_W))
    return jnp.stack(tabs, axis=0)


def _dil_kernel(q_ref, k_ref, v_ref, o_ref, qf_ref, kf_ref, vf_ref, acc_ref, m_ref, l_ref, nat_ref, mask_ref,
                s0_ref, s1_ref, p0_ref, p1_ref, m0_ref, m1_ref, *, slab_len):
    qb, kw_rows = DIL_QBLOCK, 2 * DIL_QBLOCK
    s_bufs, p_bufs, m_bufs = (s0_ref, s1_ref), (p0_ref, p1_ref), (m0_ref, m1_ref)
    ones = jnp.ones((kw_rows, LANES), BF16)
    lane = lax.broadcasted_iota(jnp.int32, (qb, LANES), 1)
    is_a_out = lane < HEAD_DIM
    is_a_rot = (lane & (HEAD_DIM - 1)) < HEAD_DIM // 2
    row = lax.broadcasted_iota(jnp.int32, (2 * qb, kw_rows), 0) & (qb - 1)
    col = lax.broadcasted_iota(jnp.int32, (2 * qb, kw_rows), 1)

    qf_ref[...] = q_ref[...].astype(F32)
    kf_ref[...] = k_ref[...].astype(F32)
    vf_ref[...] = v_ref[...].astype(F32)

    branches = sorted(DIL_BRANCHES, key=lambda wd: -wd[1])

    @pl.when((pl.program_id(0) == 0) & (pl.program_id(1) == 0))
    def _():
        for bi, (window, d) in enumerate(branches):
            g_cnt = RESIDUES // d
            rq, rk = qb // g_cnt, kw_rows // g_cnt
            base = g_cnt * ((col % rk) - (row % rq)) + (col // rk) - (row // rq)
            for pat, off in enumerate((0, (rq - rk) // 2, rq - rk)):
                on_band = jnp.abs(base + g_cnt * off) <= (window // 2) // d
                mask_ref[3 * bi + pat] = jnp.where(on_band, jnp.inf, NEG_INF)

    for bi, (window, d) in enumerate(branches):
        g_cnt = RESIDUES // d
        rq, rk = qb // g_cnt, kw_rows // g_cnt
        lead = (rk - rq) // 2
        nblk = slab_len // rq

        def tile(g, u, d=d, g_cnt=g_cnt, rq=rq, rk=rk, lead=lead, nblk=nblk):
            it = jnp.int32(g * ATTN_GROUP + u)
            i = it % nblk
            q0 = i * rq
            ws = jnp.clip(q0 - lead, 0, slab_len - rk)
            pat = jnp.where(i == 0, 0, jnp.where(i == nblk - 1, 2, 1))
            slabs = [it // nblk + d * g_ for g_ in range(g_cnt)]
            return slabs, q0, pl.multiple_of(q0, rq), ws, pat

        def window(ref, ref32, slabs, ws, rq=rq, rk=rk):
            if rq % 16 == 0:
                wsa = pl.multiple_of(ws, 16)
                return jnp.concatenate([ref[s_, pl.ds(wsa, rk), :] for s_ in slabs], axis=0)
            return jnp.concatenate([ref32[s_, pl.ds(ws, rk), :] for s_ in slabs], axis=0).astype(BF16)

        def scores(g, slot, bi=bi, rq=rq, tile=tile, window=window):
            for u in range(ATTN_GROUP):
                slabs, _, q0a, ws, pat = tile(g, u)
                if rq % 16 == 0:
                    q = jnp.concatenate([q_ref[s_, pl.ds(q0a, rq), :] for s_ in slabs], axis=0)
                else:
                    q = jnp.concatenate([qf_ref[s_, pl.ds(q0a, rq), :] for s_ in slabs], axis=0).astype(BF16)
                kw = window(k_ref, kf_ref, slabs, ws)
                zero = jnp.zeros_like(q)
                q2 = jnp.concatenate([jnp.where(is_a_rot, q, zero), jnp.where(is_a_rot, zero, q)], axis=0)
                s = lax.dot_general(q2, kw, NT_DIMS, preferred_element_type=F32)
                s_bufs[slot][u] = jnp.minimum(s, mask_ref[3 * bi + pat])

        def softmax(g, slot):
            for u in range(ATTN_GROUP):
                s = s_bufs[slot][u]
                m = jnp.max(s, axis=1, keepdims=True)
                p_bufs[slot][u] = jnp.exp2(s - m).astype(BF16)
                m_bufs[slot][u] = jnp.where(is_a_out, m[:qb], m[qb:])

        def values(g, slot, bi=bi, rq=rq, tile=tile, window=window):
            for u in range(ATTN_GROUP):
                slabs, q0, q0a, ws, _ = tile(g, u)
                vw = jnp.concatenate([window(v_ref, vf_ref, slabs, ws), ones], axis=1)
                pv = jnp.dot(p_bufs[slot][u], vw, preferred_element_type=F32)
                pv_t = jnp.where(is_a_out, pv[:qb, :LANES], pv[qb:, :LANES])
                l_t = jnp.where(is_a_out, pv[:qb, LANES:], pv[qb:, LANES:])
                m_t = m_bufs[slot][u]
                if bi > 0:
                    m_old = jnp.concatenate([m_ref[s_, pl.ds(q0a, rq), :] for s_ in slabs], axis=0)
                    l_old = jnp.concatenate([l_ref[s_, pl.ds(q0a, rq), :] for s_ in slabs], axis=0)
                    a_old = jnp.concatenate([acc_ref[s_, pl.ds(q0a, rq), :] for s_ in slabs], axis=0)
                    m_new = jnp.maximum(m_old, m_t)
                    w_old = jnp.exp2(m_old - m_new)
                    w_new = jnp.exp2(m_t - m_new)
                    l_t = l_old * w_old + l_t * w_new
                    pv_t = a_old * w_old + pv_t * w_new
                    m_t = m_new
                if bi < len(branches) - 1:
                    for g_, s_ in enumerate(slabs):
                        rows = slice(g_ * rq, (g_ + 1) * rq)
                        m_ref[s_, pl.ds(q0a, rq), :] = m_t[rows]
                        l_ref[s_, pl.ds(q0a, rq), :] = l_t[rows]
                        acc_ref[s_, pl.ds(q0a, rq), :] = pv_t[rows]
                else:
                    out = pv_t / l_t
                    t0 = q0 * RESIDUES
                    for g_, s_ in enumerate(slabs):
                        nat_ref[pl.ds(t0 + s_, rq, stride=RESIDUES), :] = out[g_ * rq:(g_ + 1) * rq]
                    t0a = pl.multiple_of(t0, qb)
                    o_ref[pl.ds(t0a, qb), :] = nat_ref[pl.ds(t0a, qb), :].astype(BF16)

        _software_pipeline(d * nblk // ATTN_GROUP, scores, softmax, values)


def _dil_attention(perm, seq):
    b = perm.shape[0]
    slab_len = seq // RESIDUES
    npair = DIL_HEADS // 2
    blk = lambda off: pl.BlockSpec((None, RESIDUES, slab_len, LANES), lambda i, p: (i, 0, 0, off + p))
    slab = pltpu.VMEM((RESIDUES, slab_len, LANES), F32)
    s_buf = pltpu.VMEM((ATTN_GROUP, 2 * DIL_QBLOCK, 2 * DIL_QBLOCK), F32)
    p_buf = pltpu.VMEM((ATTN_GROUP, 2 * DIL_QBLOCK, 2 * DIL_QBLOCK), BF16)
    stat = pltpu.VMEM((ATTN_GROUP, DIL_QBLOCK, LANES), F32)
    return pl.pallas_call(
        functools.partial(_dil_kernel, slab_len=slab_len),
        grid=(b, npair),
        in_specs=[blk(0), blk(npair), blk(2 * npair)],
        out_specs=pl.BlockSpec((None, seq, LANES), lambda i, p: (i, 0, p)),
        out_shape=jax.ShapeDtypeStruct((b, seq, MIX), BF16),
        scratch_shapes=[slab, slab, slab, slab, slab, slab,
                        pltpu.VMEM((seq, LANES), F32),
                        pltpu.VMEM((3 * len(DIL_BRANCHES), 2 * DIL_QBLOCK, 2 * DIL_QBLOCK), F32),
                        s_buf, s_buf, p_buf, p_buf, stat, stat],
        compiler_params=_params(("arbitrary", "arbitrary")),
        name="dilated_attention",
    )(perm, perm, perm)


def _even_out_kernel(x_ref, oa_ref, ob_ref, w_ref, y_ref):
    o = jnp.concatenate([oa_ref[...], ob_ref[...]], axis=1)
    y_ref[...] = x_ref[...] + jnp.dot(o, w_ref[...], preferred_element_type=F32)


def _even_out(x2d, oa, ob, w):
    n = x2d.shape[0]
    tm = OUT_TILE
    row = lambda width: pl.BlockSpec((tm, width), lambda i: (i, 0))
    return pl.pallas_call(
        _even_out_kernel,
        grid=(n // tm,),
        in_specs=[row(D_MODEL), row(MIX), row(MIX), _resident((2 * MIX, D_MODEL))],
        out_specs=row(D_MODEL),
        out_shape=jax.ShapeDtypeStruct((n, D_MODEL), F32),
        compiler_params=_params(("parallel",)),
        name="even_out",
    )(x2d, oa, ob, w)


def _ffn_kernel(x_ref, xn_ref, xp_ref, g_ref, wu_ref, cp_ref, wd_ref, fg_ref, o_ref, h_ref, a_ref, act_ref, *,
                tm, tiles_per_seq, final):
    pos = pl.program_id(0) % tiles_per_seq
    g = g_ref[...]
    keep_prev = jnp.where(pos == 0, 0.0, 1.0)
    keep_next = jnp.where(pos == tiles_per_seq - 1, 0.0, 1.0)
    h_ref[0:tm, :] = _rms(x_ref[...], g).astype(BF16)
    halo = jnp.concatenate([_rms(xn_ref[...], g) * keep_next, _rms(xp_ref[...], g) * keep_prev], axis=0)
    h_ref[tm:tm + 2 * HALO, :] = halo.astype(BF16)
    tiles = FFN_CHUNK // LANES
    for c in range(N_FFN_CHUNKS):
        buf = a_ref.at[c % 2]
        ys = []
        for part in range(2):
            first = part * FFN_DIM + c * FFN_CHUNK
            a = jnp.dot(h_ref[...], wu_ref[:, first:first + FFN_CHUNK], preferred_element_type=F32)
            for j in range(tiles):
                cols = slice(j * LANES, (j + 1) * LANES)
                buf[part * tiles + j, HALO:tm + 2 * HALO, :] = a[:tm + HALO, cols]
                buf[part * tiles + j, 0:HALO, :] = a[tm + HALO:, cols]
        for part in range(2):
            for j in range(tiles):
                slab = buf.at[part * tiles + j]
                first = part * FFN_DIM + c * FFN_CHUNK + j * LANES
                cols = slice(first, first + LANES)
                y = cp_ref[3:4, cols] + slab[HALO - 1:HALO - 1 + tm, :] * cp_ref[0:1, cols]
                y = y + slab[HALO:HALO + tm, :] * cp_ref[1:2, cols]
                y = y + slab[HALO + 1:HALO + 1 + tm, :] * cp_ref[2:3, cols]
                ys.append(y)
        half_u = jnp.concatenate(ys[:tiles], axis=1)
        gate = jnp.concatenate(ys[tiles:], axis=1)
        t = jnp.tanh(gate * (GELU_C0 + GELU_C1 * (gate * gate)))
        w = half_u * gate
        act_ref[:, c * FFN_CHUNK:(c + 1) * FFN_CHUNK] = (w + w * t).astype(BF16)
    out = x_ref[...] + jnp.dot(act_ref[...], wd_ref[...], preferred_element_type=F32)
    if final:
        out = _rms(out, fg_ref[...])
    o_ref[...] = out


def _ffn(x2d, g, wu, cp, wd, fg, seq, final):
    n = x2d.shape[0]
    tm = FFN_TILE
    tps = seq // tm
    hb = tm // HALO
    last = n // HALO - 1
    return pl.pallas_call(
        functools.partial(_ffn_kernel, tm=tm, tiles_per_seq=tps, final=final),
        grid=(n // tm,),
        in_specs=[
            pl.BlockSpec((tm, D_MODEL), lambda i: (i, 0)),
            pl.BlockSpec((HALO, D_MODEL), lambda i: (jnp.minimum((i + 1) * hb, last), 0)),
            pl.BlockSpec((HALO, D_MODEL), lambda i: (jnp.maximum(i * hb - 1, 0), 0)),
            _resident((1, D_MODEL)),
            _resident((D_MODEL, 2 * FFN_DIM)),
            _resident((SUBLANES, 2 * FFN_DIM)),
            _resident((FFN_DIM, D_MODEL)),
            _resident((1, D_MODEL)),
        ],
        out_specs=pl.BlockSpec((tm, D_MODEL), lambda i: (i, 0)),
        out_shape=jax.ShapeDtypeStruct((n, D_MODEL), F32),
        scratch_shapes=[pltpu.VMEM((tm + 2 * HALO, D_MODEL), BF16),
                        pltpu.VMEM((2, 2 * FFN_CHUNK // LANES, tm + 2 * HALO, LANES), F32),
                        pltpu.VMEM((tm, FFN_DIM), BF16)],
        compiler_params=_params(("parallel",)),
        name="conv_ffn",
    )(x2d, x2d, x2d, g, wu, cp, wd, fg)


def _ret_proj_kernel(x_ref, g_ref, w_ref, cos_ref, sin_ref, o_ref):
    h = _rms(x_ref[...], g_ref[...]).astype(BF16)
    for c in range(RET_IN // MIX):
        a = jnp.dot(h, w_ref[:, c * MIX:(c + 1) * MIX], preferred_element_type=F32)
        if c < 2 * RET_Q // MIX:
            cos, sin = cos_ref[...], sin_ref[...]
            parts = []
            for j in range(MIX // RET_QK_DIM):
                x1 = a[:, j * RET_QK_DIM:j * RET_QK_DIM + LANES]
                x2 = a[:, j * RET_QK_DIM + LANES:(j + 1) * RET_QK_DIM]
                parts += [x1 * cos - x2 * sin, x1 * sin + x2 * cos]
            a = jnp.concatenate(parts, axis=1)
        elif c >= (2 * RET_Q + RET_V) // MIX:
            half_gate = a * 0.5
            a = half_gate + half_gate * jnp.tanh(half_gate)
        o_ref[:, c * MIX:(c + 1) * MIX] = a.astype(BF16)


def _ret_proj(x2d, g, w, cos, sin, seq):
    n = x2d.shape[0]
    tm = TOKEN_TILE
    tps = seq // tm
    return pl.pallas_call(
        _ret_proj_kernel,
        grid=(n // tm,),
        in_specs=[
            pl.BlockSpec((tm, D_MODEL), lambda i: (i, 0)),
            _resident((1, D_MODEL)),
            _resident((D_MODEL, RET_IN)),
            pl.BlockSpec((tm, LANES), lambda i: (i % tps, 0)),
            pl.BlockSpec((tm, LANES), lambda i: (i % tps, 0)),
        ],
        out_specs=pl.BlockSpec((tm, RET_IN), lambda i: (i, 0)),
        out_shape=jax.ShapeDtypeStruct((n, RET_IN), BF16),
        compiler_params=_params(("parallel",)),
        name="ret_proj",
    )(x2d, g, w, cos, sin)


def _ret_kernel(lg_ref, q_ref, k_ref, v_ref, out_ref, o_ref, sf_ref, sb_ref, d_ref, *, seq):
    c = RET_BLOCK
    n = seq // c
    hd = pl.program_id(1)
    lgf = lg_ref[0, hd]
    lgb = lg_ref[1, hd]
    pr = lax.broadcasted_iota(jnp.int32, (c, c), 0).astype(F32)
    pc = lax.broadcasted_iota(jnp.int32, (c, c), 1).astype(F32)
    diff = pr - pc
    lower = diff >= 0
    d_f = jnp.where(lower, jnp.exp(lgf * jnp.where(lower, diff, 0.0)), 0.0)
    d_b = jnp.where(lower, 0.0, jnp.exp(lgb * jnp.where(lower, 0.0, -diff)))
    d_ref[...] = d_f + d_b
    pos = lax.broadcasted_iota(jnp.int32, (c, 1), 0).astype(F32)
    qd_f = jnp.exp(lgf * (pos + 1.0))
    kd_f = jnp.exp(lgf * (c - 1.0 - pos))
    qd_b = jnp.exp(lgb * (c - pos))
    kd_b = jnp.exp(lgb * pos)
    ones = jnp.ones((1, LANES), F32)
    sd_f = jnp.exp(lgf * c * ones)[:, :1]
    sd_b = jnp.exp(lgb * c * ones)[:, :1]

    def emit(r0, o, accumulate):
        if not accumulate:
            o_ref[pl.ds(r0, c), :] = o
            return
        out_ref[pl.ds(r0, c), :] = (o_ref[pl.ds(r0, c), :] + o).astype(BF16)

    def fwd_part(i, accumulate):
        r0 = pl.multiple_of(i * c, c)
        q = q_ref[pl.ds(r0, c), :]
        k = k_ref[pl.ds(r0, c), :]
        v = v_ref[pl.ds(r0, c), :]
        s = lax.dot_general(q, k, NT_DIMS, preferred_element_type=F32)
        inner = (s * d_ref[...]).astype(BF16)
        qd = (q.astype(F32) * qd_f).astype(BF16)
        o = (jnp.dot(inner, v, preferred_element_type=F32)
             + jnp.dot(qd, sf_ref[...].astype(BF16), preferred_element_type=F32))
        emit(r0, o, accumulate)
        kd = (k.astype(F32) * kd_f).astype(BF16)
        sf_ref[...] = sf_ref[...] * sd_f + lax.dot_general(kd, v, TN_DIMS, preferred_element_type=F32)

    def bwd_part(i, accumulate):
        r0 = pl.multiple_of(i * c, c)
        q = q_ref[pl.ds(r0, c), :]
        k = k_ref[pl.ds(r0, c), :]
        v = v_ref[pl.ds(r0, c), :]
        qd = (q.astype(F32) * qd_b).astype(BF16)
        o = jnp.dot(qd, sb_ref[...].astype(BF16), preferred_element_type=F32)
        emit(r0, o, accumulate)
        kd = (k.astype(F32) * kd_b).astype(BF16)
        sb_ref[...] = sb_ref[...] * sd_b + lax.dot_general(kd, v, TN_DIMS, preferred_element_type=F32)

    sf_ref[...] = jnp.zeros_like(sf_ref)
    sb_ref[...] = jnp.zeros_like(sb_ref)

    def first_half(j, carry):
        fwd_part(j, False)
        bwd_part(n - 1 - j, False)
        return carry

    def second_half(j, carry):
        fwd_part(j, True)
        bwd_part(n - 1 - j, True)
        return carry

    lax.fori_loop(0, n // 2, first_half, 0, unroll=True)
    lax.fori_loop(n // 2, n, second_half, 0, unroll=True)


def _retention(qkv, log_gamma, seq):
    b = qkv.shape[0]
    kblk = RET_Q // RET_QK_DIM
    vblk = 2 * RET_Q // RET_V_DIM
    state = pltpu.VMEM((RET_QK_DIM, RET_V_DIM), F32)
    return pl.pallas_call(
        functools.partial(_ret_kernel, seq=seq),
        grid=(b, RET_HEADS),
        in_specs=[
            pl.BlockSpec(memory_space=pltpu.SMEM),
            pl.BlockSpec((None, seq, RET_QK_DIM), lambda i, h: (i, 0, h)),
            pl.BlockSpec((None, seq, RET_QK_DIM), lambda i, h: (i, 0, kblk + h)),
            pl.BlockSpec((None, seq, RET_V_DIM), lambda i, h: (i, 0, vblk + h)),
        ],
        out_specs=pl.BlockSpec((None, seq, RET_V_DIM), lambda i, h: (i, 0, h)),
        out_shape=jax.ShapeDtypeStruct((b, seq, RET_V), BF16),
        scratch_shapes=[pltpu.VMEM((seq, RET_V_DIM), F32), state, state,
                        pltpu.VMEM((RET_BLOCK, RET_BLOCK), F32)],
        compiler_params=_params(("parallel", "parallel")),
        name="retention",
    )(log_gamma, qkv, qkv, qkv)


def _ret_out_kernel(x_ref, r_ref, g_ref, w_ref, y_ref, *, tm):
    for r0 in range(0, tm, tm // 2):
        rows = slice(r0, r0 + tm // 2)
        parts = []
        for hd in range(RET_HEADS):
            cols = slice(hd * RET_V_DIM, (hd + 1) * RET_V_DIM)
            r = r_ref[rows, cols].astype(F32)
            mu = jnp.mean(r, axis=-1, keepdims=True)
            var = jnp.mean(jnp.square(r - mu), axis=-1, keepdims=True)
            rn = (r - mu) * lax.rsqrt(var + NORM_EPS)
            parts.append((g_ref[rows, cols].astype(F32) * rn).astype(BF16))
        act = jnp.concatenate(parts, axis=1)
        y_ref[rows, :] = x_ref[rows, :] + jnp.dot(act, w_ref[...], preferred_element_type=F32)


def _ret_out(x2d, r, proj, w):
    n = x2d.shape[0]
    tm = OUT_TILE
    row = lambda width: pl.BlockSpec((tm, width), lambda i: (i, 0))
    gate_blk = (2 * RET_Q + RET_V) // RET_V
    return pl.pallas_call(
        functools.partial(_ret_out_kernel, tm=tm),
        grid=(n // tm,),
        in_specs=[row(D_MODEL), row(RET_V), pl.BlockSpec((tm, RET_V), lambda i: (i, gate_blk)),
                  _resident((RET_V, D_MODEL))],
        out_specs=row(D_MODEL),
        out_shape=jax.ShapeDtypeStruct((n, D_MODEL), F32),
        compiler_params=_params(("parallel",)),
        name="ret_out",
    )(x2d, r, proj, w)


def _rope_tables(seq, dh, lane_freq, sign):
    inv = 1.0 / (ROPE_THETA ** (jnp.arange(0, dh, 2, dtype=F32) / dh))
    ang = jnp.arange(seq, dtype=F32)[:, None] * inv[None, :]
    return jnp.cos(ang)[:, lane_freq], jnp.sin(ang)[:, lane_freq] * sign[None, :]


def _pair_rotary_layout(w):
    rows = w.shape[0]
    w = w.reshape(rows, DIL_HEADS // 2, 2, 2, HEAD_DIM // 2)
    return w.transpose(0, 1, 3, 2, 4).reshape(rows, MIX)


def _prepare(attn_norm, even_w_in, na_rpb, even_w_out, ret_w_in, ret_decay_fwd, ret_decay_bwd, ret_w_out,
             ffn_norm, ffn_w_up, ffn_conv_w, ffn_conv_b, ffn_w_down, final_norm, seq):
    lane = jnp.arange(LANES)
    half = HEAD_DIM // 2
    w_in = even_w_in[0]
    blocks = [w_in[:, i * MIX:(i + 1) * MIX] for i in range(6)]
    scale = HEAD_DIM ** -0.5 * LOG2_E
    blocks[0] = blocks[0] * scale
    blocks[3] = _pair_rotary_layout(blocks[3]) * scale
    blocks[4] = _pair_rotary_layout(blocks[4])
    p = {}
    p["even_w_in"] = jnp.concatenate(blocks, axis=1).astype(BF16)
    p["even_cos"], p["even_sin"] = _rope_tables(seq, HEAD_DIM, lane % half,
                                                jnp.where(lane < LANES // 2, -1.0, 1.0))
    p["na_bias"] = _na_bias_table(na_rpb[0], seq // GRID_W)
    p["even_w_out"] = even_w_out[0].astype(BF16)

    rw = ret_w_in[0]
    rscale = RET_QK_DIM ** -0.5
    p["ret_w_in"] = jnp.concatenate([rw[:, :RET_Q] * rscale, rw[:, RET_Q:]], axis=1).astype(BF16)
    p["ret_cos"], p["ret_sin"] = _rope_tables(seq, RET_QK_DIM, lane, jnp.ones((LANES,), F32))
    p["ret_log_gamma"] = jnp.stack([-jax.nn.softplus(ret_decay_fwd[0].astype(F32)),
                                    -jax.nn.softplus(ret_decay_bwd[0].astype(F32))], axis=0)
    p["ret_w_out"] = ret_w_out[0].astype(BF16)

    u_half = jnp.concatenate([jnp.full((FFN_DIM,), 0.5, F32), jnp.ones((FFN_DIM,), F32)])
    p["ffn_w_up"], p["ffn_conv"] = [], []
    for layer in range(2):
        p["ffn_w_up"].append(ffn_w_up[layer].astype(BF16))
        taps = jnp.concatenate([ffn_conv_w[layer], ffn_conv_b[layer][None, :]], axis=0) * u_half[None, :]
        p["ffn_conv"].append(
            jnp.concatenate([taps, jnp.zeros((SUBLANES - CONV_WIDTH - 1, 2 * FFN_DIM), F32)], axis=0))
    p["ffn_w_down"] = [ffn_w_down[layer].astype(BF16) for layer in range(2)]
    p["attn_norm"] = [attn_norm[layer][None, :] for layer in range(2)]
    p["ffn_norm"] = [ffn_norm[layer][None, :] for layer in range(2)]
    p["final_norm"] = final_norm[None, :]
    return p


def _trunk(x, p):
    b, seq, _ = x.shape
    x2d = x.reshape(b * seq, D_MODEL)
    na_qkv, dil_qkv = _even_proj(x2d, p["attn_norm"][0], p["even_w_in"], p["even_cos"], p["even_sin"], seq)
    oa = _na_attention(na_qkv.reshape(b, seq, 3 * MIX), p["na_bias"], seq).reshape(b * seq, MIX)
    ob = _dil_attention(dil_qkv, seq).reshape(b * seq, MIX)
    x2d = _even_out(x2d, oa, ob, p["even_w_out"])
    x2d = _ffn(x2d, p["ffn_norm"][0], p["ffn_w_up"][0], p["ffn_conv"][0], p["ffn_w_down"][0],
               p["final_norm"], seq, final=False)
    rproj = _ret_proj(x2d, p["attn_norm"][1], p["ret_w_in"], p["ret_cos"], p["ret_sin"], seq)
    r = _retention(rproj.reshape(b, seq, RET_IN), p["ret_log_gamma"], seq)
    x2d = _ret_out(x2d, r.reshape(b * seq, RET_V), rproj, p["ret_w_out"])
    x2d = _ffn(x2d, p["ffn_norm"][1], p["ffn_w_up"][1], p["ffn_conv"][1], p["ffn_w_down"][1],
               p["final_norm"], seq, final=True)
    return x2d.reshape(b, seq, D_MODEL)


def kernel(x_prompt, x_sample, attn_norm, even_w_in, na_rpb, even_w_out, ret_w_in, ret_decay_fwd, ret_decay_bwd,
           ret_w_out, ffn_norm, ffn_w_up, ffn_conv_w, ffn_conv_b, ffn_w_down, final_norm):
    assert x_prompt.shape[1] == x_sample.shape[1]
    p = _prepare(attn_norm, even_w_in, na_rpb, even_w_out, ret_w_in, ret_decay_fwd, ret_decay_bwd, ret_w_out,
                 ffn_norm, ffn_w_up, ffn_conv_w, ffn_conv_b, ffn_w_down, final_norm, x_prompt.shape[1])
    return _trunk(x_prompt, p), _trunk(x_sample, p)
```

```python
import functools

import jax
import jax.numpy as jnp
from jax import lax
from jax.experimental import pallas as pl
from jax.experimental.pallas import tpu as pltpu

D_MODEL = 1024
GRID_W = 64
HEAD_DIM = 64
NA_HEADS = 8
NA_WIN_R = 8
NA_WIN_C = 16
DIL_HEADS = 8
DIL_BRANCHES = ((128, 1), (512, 4), (2048, 16))
DIL_QBLOCK = 128
RET_HEADS = 4
RET_QK_DIM = 256
RET_V_DIM = 512
FFN_DIM = 2816
CONV_WIDTH = 3
ROPE_THETA = 10000.0
NORM_EPS = 1e-6
NEG_INF = -1e30

LANES = 128
SUBLANES = 8
MIX = NA_HEADS * HEAD_DIM
RET_Q = RET_HEADS * RET_QK_DIM
RET_V = RET_HEADS * RET_V_DIM
RET_IN = 2 * RET_Q + 2 * RET_V
RET_BLOCK = 256
FFN_CHUNK = 256
N_FFN_CHUNKS = FFN_DIM // FFN_CHUNK
TOKEN_TILE = 1024
OUT_TILE = 1024
FFN_TILE = 512
ATTN_GROUP = 16
NA_GROUP = 16
HALO = SUBLANES
RESIDUES = max(d for _, d in DIL_BRANCHES)
SLAB_PITCH = RESIDUES + 4
VMEM_LIMIT = 56 * 1024 * 1024
LOG2_E = 1.4426950408889634
GELU_C0 = 0.7978845608028654
GELU_C1 = GELU_C0 * 0.044715

F32 = jnp.float32
BF16 = jnp.bfloat16
NT_DIMS = (((1,), (1,)), ((), ()))
TN_DIMS = (((0,), (0,)), ((), ()))


def _params(sem, vmem=VMEM_LIMIT):
    return pltpu.CompilerParams(dimension_semantics=sem, vmem_limit_bytes=vmem)


def _resident(shape):
    nd = len(shape)
    return pl.BlockSpec(shape, lambda *_: (0,) * nd, pipeline_mode=pl.Buffered(1))


def _rms(x, g):
    ms = jnp.mean(x * x, axis=-1, keepdims=True)
    return x * lax.rsqrt(ms + NORM_EPS) * g


def _software_pipeline(n_groups, scores, softmax, values):
    assert n_groups % 2 == 0
    scores(0, 0)
    scores(1, 1)
    softmax(0, 0)

    def body(gg, carry):
        g = 2 * gg
        scores(g + 2, 0)
        softmax(g + 1, 1)
        values(g, 0)
        scores(g + 3, 1)
        softmax(g + 2, 0)
        values(g + 1, 1)
        return carry

    lax.fori_loop(0, n_groups // 2 - 1, body, 0)
    softmax(n_groups - 1, 1)
    values(n_groups - 2, 0)
    values(n_groups - 1, 1)


def _even_proj_kernel(x_ref, g_ref, w_ref, cos_ref, sin_ref, na_ref, perm_ref, slab_ref, *, tm):
    h = _rms(x_ref[...], g_ref[...]).astype(BF16)
    per_res = tm // RESIDUES
    tiles = MIX // LANES
    for c in (3, 0, 4, 1, 5, 2):
        a = jnp.dot(h, w_ref[:, c * MIX:(c + 1) * MIX], preferred_element_type=F32)
        if c in (3, 4):
            cos, sin = cos_ref[...], sin_ref[...]
            parts = []
            for j in range(tiles):
                aj = a[:, j * LANES:(j + 1) * LANES]
                parts.append(aj * cos + pltpu.roll(aj, LANES // 2, axis=1) * sin)
            a = jnp.concatenate(parts, axis=1)
        if c < 3:
            na_ref[:, c * MIX:(c + 1) * MIX] = a.astype(BF16)
        else:
            slab = slab_ref.at[c % 2]
            for j in range(tiles):
                for m in range(per_res):
                    slab[j, m * SLAB_PITCH:m * SLAB_PITCH + RESIDUES, :] = (
                        a[m * RESIDUES:(m + 1) * RESIDUES, j * LANES:(j + 1) * LANES])
            for r in range(RESIDUES):
                for j in range(tiles):
                    col = (c - 3) * MIX + j * LANES
                    rows = slab[j, pl.ds(r, per_res, stride=SLAB_PITCH), :]
                    perm_ref[r, :, col:col + LANES] = rows.astype(BF16)


def _even_proj(x2d, g, w, cos, sin, seq):
    n = x2d.shape[0]
    b = n // seq
    tm = TOKEN_TILE
    tps = seq // tm
    per_res = tm // RESIDUES
    return pl.pallas_call(
        functools.partial(_even_proj_kernel, tm=tm),
        grid=(n // tm,),
        in_specs=[
            pl.BlockSpec((tm, D_MODEL), lambda i: (i, 0)),
            _resident((1, D_MODEL)),
            _resident((D_MODEL, 6 * MIX)),
            pl.BlockSpec((tm, LANES), lambda i: (i % tps, 0)),
            pl.BlockSpec((tm, LANES), lambda i: (i % tps, 0)),
        ],
        out_specs=[pl.BlockSpec((tm, 3 * MIX), lambda i: (i, 0)),
                   pl.BlockSpec((None, RESIDUES, per_res, 3 * MIX), lambda i: (i // tps, 0, i % tps, 0))],
        out_shape=[jax.ShapeDtypeStruct((n, 3 * MIX), BF16),
                   jax.ShapeDtypeStruct((b, RESIDUES, seq // RESIDUES, 3 * MIX), BF16)],
        scratch_shapes=[pltpu.VMEM((2, MIX // LANES, per_res * SLAB_PITCH, LANES), F32)],
        compiler_params=_params(("parallel",)),
        name="even_proj",
    )(x2d, g, w, cos, sin)


def _na_kernel(q_ref, k_ref, v_ref, b_ref, o_ref, s0_ref, s1_ref, p0_ref, p1_ref, *, rows):
    lane = lax.broadcasted_iota(jnp.int32, (GRID_W, LANES), 1)
    is_a = lane < HEAD_DIM
    win = NA_WIN_R * GRID_W
    s_bufs, p_bufs = (s0_ref, s1_ref), (p0_ref, p1_ref)

    def window_start(r):
        rs = jnp.clip(r - NA_WIN_R // 2, 0, rows - NA_WIN_R)
        return pl.multiple_of(rs * GRID_W, GRID_W)

    def scores(g, slot):
        for u in range(NA_GROUP):
            r = jnp.int32(g * NA_GROUP + u)
            pat = jnp.where(r < NA_WIN_R // 2, r,
                            jnp.where(r > rows - NA_WIN_R // 2, r - (rows - NA_WIN_R), NA_WIN_R // 2))
            q = q_ref[pl.ds(pl.multiple_of(r * GRID_W, GRID_W), GRID_W), :]
            zero = jnp.zeros_like(q)
            q2 = jnp.concatenate([jnp.where(is_a, q, zero), jnp.where(is_a, zero, q)], axis=0)
            kw = k_ref[pl.ds(window_start(r), win), :]
            s = lax.dot_general(q2, kw, NT_DIMS, preferred_element_type=F32)
            s_bufs[slot][u] = s + b_ref[pat]

    def softmax(g, slot):
        for u in range(NA_GROUP):
            s = s_bufs[slot][u]
            m = jnp.max(s, axis=1, keepdims=True)
            p_bufs[slot][u] = jnp.exp2(s - m).astype(BF16)

    ones = jnp.ones((win, LANES), BF16)

    def values(g, slot):
        for u in range(NA_GROUP):
            r = jnp.int32(g * NA_GROUP + u)
            vw = jnp.concatenate([v_ref[pl.ds(window_start(r), win), :], ones], axis=1)
            pv = jnp.dot(p_bufs[slot][u], vw, preferred_element_type=F32)
            pv = pv[:, :LANES] / pv[:, LANES:]
            o_ref[pl.ds(pl.multiple_of(r * GRID_W, GRID_W), GRID_W), :] = (
                jnp.where(is_a, pv[:GRID_W], pv[GRID_W:]).astype(BF16))

    _software_pipeline(rows // NA_GROUP, scores, softmax, values)


def _na_attention(qkv, bias, seq):
    b = qkv.shape[0]
    rows = seq // GRID_W
    npair = NA_HEADS // 2
    win = NA_WIN_R * GRID_W
    blk = lambda off: pl.BlockSpec((None, seq, LANES), lambda p, i: (i, 0, off + p))
    s_buf = pltpu.VMEM((NA_GROUP, 2 * GRID_W, win), F32)
    p_buf = pltpu.VMEM((NA_GROUP, 2 * GRID_W, win), BF16)
    return pl.pallas_call(
        functools.partial(_na_kernel, rows=rows),
        grid=(npair, b),
        in_specs=[blk(0), blk(npair), blk(2 * npair),
                  pl.BlockSpec((NA_WIN_R, None, 2 * GRID_W, win), lambda p, i: (0, p, 0, 0))],
        out_specs=pl.BlockSpec((None, seq, LANES), lambda p, i: (i, 0, p)),
        out_shape=jax.ShapeDtypeStruct((b, seq, MIX), BF16),
        scratch_shapes=[s_buf, s_buf, p_buf, p_buf],
        compiler_params=_params(("parallel", "parallel")),
        name="na_attention",
    )(qkv, qkv, qkv, bias)


def _na_bias_table(rpb, rows):
    c = jnp.arange(GRID_W)
    cs = jnp.clip(c - NA_WIN_C // 2, 0, GRID_W - NA_WIN_C)
    j = jnp.arange(GRID_W)
    valid = (j[None, :] >= cs[:, None]) & (j[None, :] < cs[:, None] + NA_WIN_C)
    k = jnp.arange(2 * NA_WIN_C - 1)
    pick = (k[None, None, :] == (j[None, :, None] - c[:, None, None] + NA_WIN_C - 1)).astype(F32)
    full = jnp.einsum("hrk,cjk->hcrj", rpb.astype(F32), pick, precision=lax.Precision.HIGHEST)
    half = NA_WIN_R // 2
    reps = list(range(half)) + [half] + list(range(rows - half + 1, rows))
    tabs = []
    for r in reps:
        rs = min(max(r - half, 0), rows - NA_WIN_R)
        first = rs - r + NA_WIN_R - 1
        t = jnp.where(valid[None, :, None, :], full[:, :, first:first + NA_WIN_R, :] * LOG2_E, NEG_INF)
        tabs.append(t.reshape(NA_HEADS // 2, 2 * GRID_W, NA_WIN_R * GRID_W))
    return jnp.stack(tabs, axis=0)


def _dil_kernel(q_ref, k_ref, v_ref, o_ref, qf_ref, kf_ref, vf_ref, acc_ref, m_ref, l_ref, nat_ref, mask_ref,
                s0_ref, s1_ref, p0_ref, p1_ref, m0_ref, m1_ref, *, slab_len):
    qb, kw_rows = DIL_QBLOCK, 2 * DIL_QBLOCK
    s_bufs, p_bufs, m_bufs = (s0_ref, s1_ref), (p0_ref, p1_ref), (m0_ref, m1_ref)
    ones = jnp.ones((kw_rows, LANES), BF16)
    lane = lax.broadcasted_iota(jnp.int32, (qb, LANES), 1)
    is_a_out = lane < HEAD_DIM
    is_a_rot = (lane & (HEAD_DIM - 1)) < HEAD_DIM // 2
    row = lax.broadcasted_iota(jnp.int32, (2 * qb, kw_rows), 0) & (qb - 1)
    col = lax.broadcasted_iota(jnp.int32, (2 * qb, kw_rows), 1)

    qf_ref[...] = q_ref[...].astype(F32)
    kf_ref[...] = k_ref[...].astype(F32)
    vf_ref[...] = v_ref[...].astype(F32)

    branches = sorted(DIL_BRANCHES, key=lambda wd: -wd[1])

    @pl.when((pl.program_id(0) == 0) & (pl.program_id(1) == 0))
    def _():
        for bi, (window, d) in enumerate(branches):
            g_cnt = RESIDUES // d
            rq, rk = qb // g_cnt, kw_rows // g_cnt
            base = g_cnt * ((col % rk) - (row % rq)) + (col // rk) - (row // rq)
            for pat, off in enumerate((0, (rq - rk) // 2, rq - rk)):
                on_band = jnp.abs(base + g_cnt * off) <= (window // 2) // d
                mask_ref[3 * bi + pat] = jnp.where(on_band, jnp.inf, NEG_INF)

    for bi, (window, d) in enumerate(branches):
        g_cnt = RESIDUES // d
        rq, rk = qb // g_cnt, kw_rows // g_cnt
        lead = (rk - rq) // 2
        nblk = slab_len // rq

        def tile(g, u, d=d, g_cnt=g_cnt, rq=rq, rk=rk, lead=lead, nblk=nblk):
            it = jnp.int32(g * ATTN_GROUP + u)
            i = it % nblk
            q0 = i * rq
            ws = jnp.clip(q0 - lead, 0, slab_len - rk)
            pat = jnp.where(i == 0, 0, jnp.where(i == nblk - 1, 2, 1))
            slabs = [it // nblk + d * g_ for g_ in range(g_cnt)]
            return slabs, q0, pl.multiple_of(q0, rq), ws, pat

        def window(ref, ref32, slabs, ws, rq=rq, rk=rk):
            if rq % 16 == 0:
                wsa = pl.multiple_of(ws, 16)
                return jnp.concatenate([ref[s_, pl.ds(wsa, rk), :] for s_ in slabs], axis=0)
            return jnp.concatenate([ref32[s_, pl.ds(ws, rk), :] for s_ in slabs], axis=0).astype(BF16)

        def scores(g, slot, bi=bi, rq=rq, tile=tile, window=window):
            for u in range(ATTN_GROUP):
                slabs, _, q0a, ws, pat = tile(g, u)
                if rq % 16 == 0:
                    q = jnp.concatenate([q_ref[s_, pl.ds(q0a, rq), :] for s_ in slabs], axis=0)
                else:
                    q = jnp.concatenate([qf_ref[s_, pl.ds(q0a, rq), :] for s_ in slabs], axis=0).astype(BF16)
                kw = window(k_ref, kf_ref, slabs, ws)
                zero = jnp.zeros_like(q)
                q2 = jnp.concatenate([jnp.where(is_a_rot, q, zero), jnp.where(is_a_rot, zero, q)], axis=0)
                s = lax.dot_general(q2, kw, NT_DIMS, preferred_element_type=F32)
                s_bufs[slot][u] = jnp.minimum(s, mask_ref[3 * bi + pat])

        def softmax(g, slot):
            for u in range(ATTN_GROUP):
                s = s_bufs[slot][u]
                m = jnp.max(s, axis=1, keepdims=True)
                p_bufs[slot][u] = jnp.exp2(s - m).astype(BF16)
                m_bufs[slot][u] = jnp.where(is_a_out, m[:qb], m[qb:])

        def values(g, slot, bi=bi, rq=rq, tile=tile, window=window):
            for u in range(ATTN_GROUP):
                slabs, q0, q0a, ws, _ = tile(g, u)
                vw = jnp.concatenate([window(v_ref, vf_ref, slabs, ws), ones], axis=1)
                pv = jnp.dot(p_bufs[slot][u], vw, preferred_element_type=F32)
                pv_t = jnp.where(is_a_out, pv[:qb, :LANES], pv[qb:, :LANES])
                l_t = jnp.where(is_a_out, pv[:qb, LANES:], pv[qb:, LANES:])
                m_t = m_bufs[slot][u]
                if bi > 0:
                    m_old = jnp.concatenate([m_ref[s_, pl.ds(q0a, rq), :] for s_ in slabs], axis=0)
                    l_old = jnp.concatenate([l_ref[s_, pl.ds(q0a, rq), :] for s_ in slabs], axis=0)
                    a_old = jnp.concatenate([acc_ref[s_, pl.ds(q0a, rq), :] for s_ in slabs], axis=0)
                    m_new = jnp.maximum(m_old, m_t)
                    w_old = jnp.exp2(m_old - m_new)
                    w_new = jnp.exp2(m_t - m_new)
                    l_t = l_old * w_old + l_t * w_new
                    pv_t = a_old * w_old + pv_t * w_new
                    m_t = m_new
                if bi < len(branches) - 1:
                    for g_, s_ in enumerate(slabs):
                        rows = slice(g_ * rq, (g_ + 1) * rq)
                        m_ref[s_, pl.ds(q0a, rq), :] = m_t[rows]
                        l_ref[s_, pl.ds(q0a, rq), :] = l_t[rows]
                        acc_ref[s_, pl.ds(q0a, rq), :] = pv_t[rows]
                else:
                    out = pv_t / l_t
                    t0 = q0 * RESIDUES
                    for g_, s_ in enumerate(slabs):
                        nat_ref[pl.ds(t0 + s_, rq, stride=RESIDUES), :] = out[g_ * rq:(g_ + 1) * rq]
                    t0a = pl.multiple_of(t0, qb)
                    o_ref[pl.ds(t0a, qb), :] = nat_ref[pl.ds(t0a, qb), :].astype(BF16)

        _software_pipeline(d * nblk // ATTN_GROUP, scores, softmax, values)


def _dil_attention(perm, seq):
    b = perm.shape[0]
    slab_len = seq // RESIDUES
    npair = DIL_HEADS // 2
    blk = lambda off: pl.BlockSpec((None, RESIDUES, slab_len, LANES), lambda i, p: (i, 0, 0, off + p))
    slab = pltpu.VMEM((RESIDUES, slab_len, LANES), F32)
    s_buf = pltpu.VMEM((ATTN_GROUP, 2 * DIL_QBLOCK, 2 * DIL_QBLOCK), F32)
    p_buf = pltpu.VMEM((ATTN_GROUP, 2 * DIL_QBLOCK, 2 * DIL_QBLOCK), BF16)
    stat = pltpu.VMEM((ATTN_GROUP, DIL_QBLOCK, LANES), F32)
    return pl.pallas_call(
        functools.partial(_dil_kernel, slab_len=slab_len),
        grid=(b, npair),
        in_specs=[blk(0), blk(npair), blk(2 * npair)],
        out_specs=pl.BlockSpec((None, seq, LANES), lambda i, p: (i, 0, p)),
        out_shape=jax.ShapeDtypeStruct((b, seq, MIX), BF16),
        scratch_shapes=[slab, slab, slab, slab, slab, slab,
                        pltpu.VMEM((seq, LANES), F32),
                        pltpu.VMEM((3 * len(DIL_BRANCHES), 2 * DIL_QBLOCK, 2 * DIL_QBLOCK), F32),
                        s_buf, s_buf, p_buf, p_buf, stat, stat],
        compiler_params=_params(("arbitrary", "arbitrary")),
        name="dilated_attention",
    )(perm, perm, perm)


def _even_out_kernel(x_ref, oa_ref, ob_ref, w_ref, y_ref):
    o = jnp.concatenate([oa_ref[...], ob_ref[...]], axis=1)
    y_ref[...] = x_ref[...] + jnp.dot(o, w_ref[...], preferred_element_type=F32)


def _even_out(x2d, oa, ob, w):
    n = x2d.shape[0]
    tm = OUT_TILE
    row = lambda width: pl.BlockSpec((tm, width), lambda i: (i, 0))
    return pl.pallas_call(
        _even_out_kernel,
        grid=(n // tm,),
        in_specs=[row(D_MODEL), row(MIX), row(MIX), _resident((2 * MIX, D_MODEL))],
        out_specs=row(D_MODEL),
        out_shape=jax.ShapeDtypeStruct((n, D_MODEL), F32),
        compiler_params=_params(("parallel",)),
        name="even_out",
    )(x2d, oa, ob, w)


def _ffn_kernel(x_ref, xn_ref, xp_ref, g_ref, wu_ref, cp_ref, wd_ref, fg_ref, o_ref, h_ref, a_ref, act_ref, *,
                tm, tiles_per_seq, final):
    pos = pl.program_id(0) % tiles_per_seq
    g = g_ref[...]
    keep_prev = jnp.where(pos == 0, 0.0, 1.0)
    keep_next = jnp.where(pos == tiles_per_seq - 1, 0.0, 1.0)
    h_ref[0:tm, :] = _rms(x_ref[...], g).astype(BF16)
    halo = jnp.concatenate([_rms(xn_ref[...], g) * keep_next, _rms(xp_ref[...], g) * keep_prev], axis=0)
    h_ref[tm:tm + 2 * HALO, :] = halo.astype(BF16)
    tiles = FFN_CHUNK // LANES
    for c in range(N_FFN_CHUNKS):
        buf = a_ref.at[c % 2]
        ys = []
        for part in range(2):
            first = part * FFN_DIM + c * FFN_CHUNK
            a = jnp.dot(h_ref[...], wu_ref[:, first:first + FFN_CHUNK], preferred_element_type=F32)
            for j in range(tiles):
                cols = slice(j * LANES, (j + 1) * LANES)
                buf[part * tiles + j, HALO:tm + 2 * HALO, :] = a[:tm + HALO, cols]
                buf[part * tiles + j, 0:HALO, :] = a[tm + HALO:, cols]
        for part in range(2):
            for j in range(tiles):
                slab = buf.at[part * tiles + j]
                first = part * FFN_DIM + c * FFN_CHUNK + j * LANES
                cols = slice(first, first + LANES)
                y = cp_ref[3:4, cols] + slab[HALO - 1:HALO - 1 + tm, :] * cp_ref[0:1, cols]
                y = y + slab[HALO:HALO + tm, :] * cp_ref[1:2, cols]
                y = y + slab[HALO + 1:HALO + 1 + tm, :] * cp_ref[2:3, cols]
                ys.append(y)
        half_u = jnp.concatenate(ys[:tiles], axis=1)
        gate = jnp.concatenate(ys[tiles:], axis=1)
        t = jnp.tanh(gate * (GELU_C0 + GELU_C1 * (gate * gate)))
        w = half_u * gate
        act_ref[:, c * FFN_CHUNK:(c + 1) * FFN_CHUNK] = (w + w * t).astype(BF16)
    out = x_ref[...] + jnp.dot(act_ref[...], wd_ref[...], preferred_element_type=F32)
    if final:
        out = _rms(out, fg_ref[...])
    o_ref[...] = out


def _ffn(x2d, g, wu, cp, wd, fg, seq, final):
    n = x2d.shape[0]
    tm = FFN_TILE
    tps = seq // tm
    hb = tm // HALO
    last = n // HALO - 1
    return pl.pallas_call(
        functools.partial(_ffn_kernel, tm=tm, tiles_per_seq=tps, final=final),
        grid=(n // tm,),
        in_specs=[
            pl.BlockSpec((tm, D_MODEL), lambda i: (i, 0)),
            pl.BlockSpec((HALO, D_MODEL), lambda i: (jnp.minimum((i + 1) * hb, last), 0)),
            pl.BlockSpec((HALO, D_MODEL), lambda i: (jnp.maximum(i * hb - 1, 0), 0)),
            _resident((1, D_MODEL)),
            _resident((D_MODEL, 2 * FFN_DIM)),
            _resident((SUBLANES, 2 * FFN_DIM)),
            _resident((FFN_DIM, D_MODEL)),
            _resident((1, D_MODEL)),
        ],
        out_specs=pl.BlockSpec((tm, D_MODEL), lambda i: (i, 0)),
        out_shape=jax.ShapeDtypeStruct((n, D_MODEL), F32),
        scratch_shapes=[pltpu.VMEM((tm + 2 * HALO, D_MODEL), BF16),
                        pltpu.VMEM((2, 2 * FFN_CHUNK // LANES, tm + 2 * HALO, LANES), F32),
                        pltpu.VMEM((tm, FFN_DIM), BF16)],
        compiler_params=_params(("parallel",)),
        name="conv_ffn",
    )(x2d, x2d, x2d, g, wu, cp, wd, fg)


def _ret_proj_kernel(x_ref, g_ref, w_ref, cos_ref, sin_ref, o_ref):
    h = _rms(x_ref[...], g_ref[...]).astype(BF16)
    for c in range(RET_IN // MIX):
        a = jnp.dot(h, w_ref[:, c * MIX:(c + 1) * MIX], preferred_element_type=F32)
        if c < 2 * RET_Q // MIX:
            cos, sin = cos_ref[...], sin_ref[...]
            parts = []
            for j in range(MIX // RET_QK_DIM):
                x1 = a[:, j * RET_QK_DIM:j * RET_QK_DIM + LANES]
                x2 = a[:, j * RET_QK_DIM + LANES:(j + 1) * RET_QK_DIM]
                parts += [x1 * cos - x2 * sin, x1 * sin + x2 * cos]
            a = jnp.concatenate(parts, axis=1)
        elif c >= (2 * RET_Q + RET_V) // MIX:
            half_gate = a * 0.5
            a = half_gate + half_gate * jnp.tanh(half_gate)
        o_ref[:, c * MIX:(c + 1) * MIX] = a.astype(BF16)


def _ret_proj(x2d, g, w, cos, sin, seq):
    n = x2d.shape[0]
    tm = TOKEN_TILE
    tps = seq // tm
    return pl.pallas_call(
        _ret_proj_kernel,
        grid=(n // tm,),
        in_specs=[
            pl.BlockSpec((tm, D_MODEL), lambda i: (i, 0)),
            _resident((1, D_MODEL)),
            _resident((D_MODEL, RET_IN)),
            pl.BlockSpec((tm, LANES), lambda i: (i % tps, 0)),
            pl.BlockSpec((tm, LANES), lambda i: (i % tps, 0)),
        ],
        out_specs=pl.BlockSpec((tm, RET_IN), lambda i: (i, 0)),
        out_shape=jax.ShapeDtypeStruct((n, RET_IN), BF16),
        compiler_params=_params(("parallel",)),
        name="ret_proj",
    )(x2d, g, w, cos, sin)


def _ret_kernel(lg_ref, q_ref, k_ref, v_ref, out_ref, o_ref, sf_ref, sb_ref, d_ref, *, seq):
    c = RET_BLOCK
    n = seq // c
    hd = pl.program_id(1)
    lgf = lg_ref[0, hd]
    lgb = lg_ref[1, hd]
    pr = lax.broadcasted_iota(jnp.int32, (c, c), 0).astype(F32)
    pc = lax.broadcasted_iota(jnp.int32, (c, c), 1).astype(F32)
    diff = pr - pc
    lower = diff >= 0
    d_f = jnp.where(lower, jnp.exp(lgf * jnp.where(lower, diff, 0.0)), 0.0)
    d_b = jnp.where(lower, 0.0, jnp.exp(lgb * jnp.where(lower, 0.0, -diff)))
    d_ref[...] = d_f + d_b
    pos = lax.broadcasted_iota(jnp.int32, (c, 1), 0).astype(F32)
    qd_f = jnp.exp(lgf * (pos + 1.0))
    kd_f = jnp.exp(lgf * (c - 1.0 - pos))
    qd_b = jnp.exp(lgb * (c - pos))
    kd_b = jnp.exp(lgb * pos)
    ones = jnp.ones((1, LANES), F32)
    sd_f = jnp.exp(lgf * c * ones)[:, :1]
    sd_b = jnp.exp(lgb * c * ones)[:, :1]

    def emit(r0, o, accumulate):
        if not accumulate:
            o_ref[pl.ds(r0, c), :] = o
            return
        out_ref[pl.ds(r0, c), :] = (o_ref[pl.ds(r0, c), :] + o).astype(BF16)

    def fwd_part(i, accumulate):
        r0 = pl.multiple_of(i * c, c)
        q = q_ref[pl.ds(r0, c), :]
        k = k_ref[pl.ds(r0, c), :]
        v = v_ref[pl.ds(r0, c), :]
        s = lax.dot_general(q, k, NT_DIMS, preferred_element_type=F32)
        inner = (s * d_ref[...]).astype(BF16)
        qd = (q.astype(F32) * qd_f).astype(BF16)
        o = (jnp.dot(inner, v, preferred_element_type=F32)
             + jnp.dot(qd, sf_ref[...].astype(BF16), preferred_element_type=F32))
        emit(r0, o, accumulate)
        kd = (k.astype(F32) * kd_f).astype(BF16)
        sf_ref[...] = sf_ref[...] * sd_f + lax.dot_general(kd, v, TN_DIMS, preferred_element_type=F32)

    def bwd_part(i, accumulate):
        r0 = pl.multiple_of(i * c, c)
        q = q_ref[pl.ds(r0, c), :]
        k = k_ref[pl.ds(r0, c), :]
        v = v_ref[pl.ds(r0, c), :]
        qd = (q.astype(F32) * qd_b).astype(BF16)
        o = jnp.dot(qd, sb_ref[...].astype(BF16), preferred_element_type=F32)
        emit(r0, o, accumulate)
        kd = (k.astype(F32) * kd_b).astype(BF16)
        sb_ref[...] = sb_ref[...] * sd_b + lax.dot_general(kd, v, TN_DIMS, preferred_element_type=F32)

    sf_ref[...] = jnp.zeros_like(sf_ref)
    sb_ref[...] = jnp.zeros_like(sb_ref)

    def first_half(j, carry):
        fwd_part(j, False)
        bwd_part(n - 1 - j, False)
        return carry

    def second_half(j, carry):
        fwd_part(j, True)
        bwd_part(n - 1 - j, True)
        return carry

    lax.fori_loop(0, n // 2, first_half, 0, unroll=True)
    lax.fori_loop(n // 2, n, second_half, 0, unroll=True)


def _retention(qkv, log_gamma, seq):
    b = qkv.shape[0]
    kblk = RET_Q // RET_QK_DIM
    vblk = 2 * RET_Q // RET_V_DIM
    state = pltpu.VMEM((RET_QK_DIM, RET_V_DIM), F32)
    return pl.pallas_call(
        functools.partial(_ret_kernel, seq=seq),
        grid=(b, RET_HEADS),
        in_specs=[
            pl.BlockSpec(memory_space=pltpu.SMEM),
            pl.BlockSpec((None, seq, RET_QK_DIM), lambda i, h: (i, 0, h)),
            pl.BlockSpec((None, seq, RET_QK_DIM), lambda i, h: (i, 0, kblk + h)),
            pl.BlockSpec((None, seq, RET_V_DIM), lambda i, h: (i, 0, vblk + h)),
        ],
        out_specs=pl.BlockSpec((None, seq, RET_V_DIM), lambda i, h: (i, 0, h)),
        out_shape=jax.ShapeDtypeStruct((b, seq, RET_V), BF16),
        scratch_shapes=[pltpu.VMEM((seq, RET_V_DIM), F32), state, state,
                        pltpu.VMEM((RET_BLOCK, RET_BLOCK), F32)],
        compiler_params=_params(("parallel", "parallel")),
        name="retention",
    )(log_gamma, qkv, qkv, qkv)


def _ret_out_kernel(x_ref, r_ref, g_ref, w_ref, y_ref, *, tm):
    for r0 in range(0, tm, tm // 2):
        rows = slice(r0, r0 + tm // 2)
        parts = []
        for hd in range(RET_HEADS):
            cols = slice(hd * RET_V_DIM, (hd + 1) * RET_V_DIM)
            r = r_ref[rows, cols].astype(F32)
            mu = jnp.mean(r, axis=-1, keepdims=True)
            var = jnp.mean(jnp.square(r - mu), axis=-1, keepdims=True)
            rn = (r - mu) * lax.rsqrt(var + NORM_EPS)
            parts.append((g_ref[rows, cols].astype(F32) * rn).astype(BF16))
        act = jnp.concatenate(parts, axis=1)
        y_ref[rows, :] = x_ref[rows, :] + jnp.dot(act, w_ref[...], preferred_element_type=F32)


def _ret_out(x2d, r, proj, w):
    n = x2d.shape[0]
    tm = OUT_TILE
    row = lambda width: pl.BlockSpec((tm, width), lambda i: (i, 0))
    gate_blk = (2 * RET_Q + RET_V) // RET_V
    return pl.pallas_call(
        functools.partial(_ret_out_kernel, tm=tm),
        grid=(n // tm,),
        in_specs=[row(D_MODEL), row(RET_V), pl.BlockSpec((tm, RET_V), lambda i: (i, gate_blk)),
                  _resident((RET_V, D_MODEL))],
        out_specs=row(D_MODEL),
        out_shape=jax.ShapeDtypeStruct((n, D_MODEL), F32),
        compiler_params=_params(("parallel",)),
        name="ret_out",
    )(x2d, r, proj, w)


def _rope_tables(seq, dh, lane_freq, sign):
    inv = 1.0 / (ROPE_THETA ** (jnp.arange(0, dh, 2, dtype=F32) / dh))
    ang = jnp.arange(seq, dtype=F32)[:, None] * inv[None, :]
    return jnp.cos(ang)[:, lane_freq], jnp.sin(ang)[:, lane_freq] * sign[None, :]


def _pair_rotary_layout(w):
    rows = w.shape[0]
    w = w.reshape(rows, DIL_HEADS // 2, 2, 2, HEAD_DIM // 2)
    return w.transpose(0, 1, 3, 2, 4).reshape(rows, MIX)


def _prepare(attn_norm, even_w_in, na_rpb, even_w_out, ret_w_in, ret_decay_fwd, ret_decay_bwd, ret_w_out,
             ffn_norm, ffn_w_up, ffn_conv_w, ffn_conv_b, ffn_w_down, final_norm, seq):
    lane = jnp.arange(LANES)
    half = HEAD_DIM // 2
    w_in = even_w_in[0]
    blocks = [w_in[:, i * MIX:(i + 1) * MIX] for i in range(6)]
    scale = HEAD_DIM ** -0.5 * LOG2_E
    blocks[0] = blocks[0] * scale
    blocks[3] = _pair_rotary_layout(blocks[3]) * scale
    blocks[4] = _pair_rotary_layout(blocks[4])
    p = {}
    p["even_w_in"] = jnp.concatenate(blocks, axis=1).astype(BF16)
    p["even_cos"], p["even_sin"] = _rope_tables(seq, HEAD_DIM, lane % half,
                                                jnp.where(lane < LANES // 2, -1.0, 1.0))
    p["na_bias"] = _na_bias_table(na_rpb[0], seq // GRID_W)
    p["even_w_out"] = even_w_out[0].astype(BF16)

    rw = ret_w_in[0]
    rscale = RET_QK_DIM ** -0.5
    p["ret_w_in"] = jnp.concatenate([rw[:, :RET_Q] * rscale, rw[:, RET_Q:]], axis=1).astype(BF16)
    p["ret_cos"], p["ret_sin"] = _rope_tables(seq, RET_QK_DIM, lane, jnp.ones((LANES,), F32))
    p["ret_log_gamma"] = jnp.stack([-jax.nn.softplus(ret_decay_fwd[0].astype(F32)),
                                    -jax.nn.softplus(ret_decay_bwd[0].astype(F32))], axis=0)
    p["ret_w_out"] = ret_w_out[0].astype(BF16)

    u_half = jnp.concatenate([jnp.full((FFN_DIM,), 0.5, F32), jnp.ones((FFN_DIM,), F32)])
    p["ffn_w_up"], p["ffn_conv"] = [], []
    for layer in range(2):
        p["ffn_w_up"].append(ffn_w_up[layer].astype(BF16))
        taps = jnp.concatenate([ffn_conv_w[layer], ffn_conv_b[layer][None, :]], axis=0) * u_half[None, :]
        p["ffn_conv"].append(
            jnp.concatenate([taps, jnp.zeros((SUBLANES - CONV_WIDTH - 1, 2 * FFN_DIM), F32)], axis=0))
    p["ffn_w_down"] = [ffn_w_down[layer].astype(BF16) for layer in range(2)]
    p["attn_norm"] = [attn_norm[layer][None, :] for layer in range(2)]
    p["ffn_norm"] = [ffn_norm[layer][None, :] for layer in range(2)]
    p["final_norm"] = final_norm[None, :]
    return p


def _trunk(x, p):
    b, seq, _ = x.shape
    x2d = x.reshape(b * seq, D_MODEL)
    na_qkv, dil_qkv = _even_proj(x2d, p["attn_norm"][0], p["even_w_in"], p["even_cos"], p["even_sin"], seq)
    oa = _na_attention(na_qkv.reshape(b, seq, 3 * MIX), p["na_bias"], seq).reshape(b * seq, MIX)
    ob = _dil_attention(dil_qkv, seq).reshape(b * seq, MIX)
    x2d = _even_out(x2d, oa, ob, p["even_w_out"])
    x2d = _ffn(x2d, p["ffn_norm"][0], p["ffn_w_up"][0], p["ffn_conv"][0], p["ffn_w_down"][0],
               p["final_norm"], seq, final=False)
    rproj = _ret_proj(x2d, p["attn_norm"][1], p["ret_w_in"], p["ret_cos"], p["ret_sin"], seq)
    r = _retention(rproj.reshape(b, seq, RET_IN), p["ret_log_gamma"], seq)
    x2d = _ret_out(x2d, r.reshape(b * seq, RET_V), rproj, p["ret_w_out"])
    x2d = _ffn(x2d, p["ffn_norm"][1], p["ffn_w_up"][1], p["ffn_conv"][1], p["ffn_w_down"][1],
               p["final_norm"], seq, final=True)
    return x2d.reshape(b, seq, D_MODEL)


def kernel(x_prompt, x_sample, attn_norm, even_w_in, na_rpb, even_w_out, ret_w_in, ret_decay_fwd, ret_decay_bwd,
           ret_w_out, ffn_norm, ffn_w_up, ffn_conv_w, ffn_conv_b, ffn_w_down, final_norm):
    assert x_prompt.shape[1] == x_sample.shape[1]
    p = _prepare(attn_norm, even_w_in, na_rpb, even_w_out, ret_w_in, ret_decay_fwd, ret_decay_bwd, ret_w_out,
                 ffn_norm, ffn_w_up, ffn_conv_w, ffn_conv_b, ffn_w_down, final_norm, x_prompt.shape[1])
    return _trunk(x_prompt, p), _trunk(x_sample, p)
```
